```python
import jax
import jax.numpy as jnp
from jax import lax
import numpy as np

D_MODEL = 2048
BATCH = 4
SEQ = 4096
DEPTH = 1

HEAD_DIM = 128
GDN_HEADS = 8
MOBA_HEADS = 8
GDN_WIDTH = GDN_HEADS * HEAD_DIM
MOBA_WIDTH = MOBA_HEADS * HEAD_DIM
MIX_WIDTH = GDN_WIDTH + MOBA_WIDTH
CONV_WIDTH = 4
GDN_CHUNK = 64
MOBA_BLOCK = 256
MOBA_TOPK = 3
MOBA_Q_CHUNK = 32
N_GROUPS = 4
EXPERTS_PER_GROUP = 8
N_EXPERTS = N_GROUPS * EXPERTS_PER_GROUP
TOPK_IN_GROUP = 2
D_EXPERT = 512
EXPERT_ROW_BLOCK = 128
RMS_EPS = 1e-6
NEG_INF = -1e30
IN_SPLITS = (GDN_WIDTH, GDN_WIDTH, GDN_WIDTH, GDN_WIDTH, GDN_HEADS, GDN_HEADS, MOBA_WIDTH, MOBA_WIDTH, MOBA_WIDTH)
IN_PROJ_WIDTH = 4 * GDN_WIDTH + 2 * GDN_HEADS + 3 * MOBA_WIDTH

kernel_name = 'hybrid_gdn_moba_hier_moe_block'


def _rms(x):
    return x * lax.rsqrt(jnp.mean(x * x, axis=-1, keepdims=True) + RMS_EPS)


def rmsnorm(x, w):
    return (_rms(x.astype(jnp.float32)) * w.astype(jnp.float32)).astype(x.dtype)


def l2norm(x):
    return x * lax.rsqrt(jnp.sum(x * x, axis=-1, keepdims=True) + 1e-6)


def alibi_slopes(n_heads):
    return jnp.exp2(-8.0 * jnp.arange(1, n_heads + 1, dtype=jnp.float32) / n_heads)


def split_columns(p, sizes):
    offs = np.cumsum(np.array(sizes))[:-1].tolist()
    return jnp.split(p, offs, axis=-1)


def causal_depthwise_conv(x, w):
    return lax.conv_general_dilated(
        x, w[:, None, :].astype(x.dtype), window_strides=(1,),
        padding=[(CONV_WIDTH - 1, 0)], dimension_numbers=('NWC', 'WIO', 'NWC'),
        feature_group_count=x.shape[-1])


def gated_delta_rule_chunked(q, k, v, g, beta):
    B, H, T, Dk = q.shape
    Dv = v.shape[-1]
    C = GDN_CHUNK
    N = T // C
    q = q.reshape(B, H, N, C, Dk)
    k = k.reshape(B, H, N, C, Dk)
    v = v.reshape(B, H, N, C, Dv)
    g = jnp.cumsum(g.reshape(B, H, N, C), axis=-1)
    beta = beta.reshape(B, H, N, C)
    causal = jnp.tril(jnp.ones((C, C), bool))
    strict = jnp.tril(jnp.ones((C, C), bool), -1)
    decay = jnp.exp(jnp.where(causal, g[..., :, None] - g[..., None, :], NEG_INF))
    kb = k * beta[..., None]
    L = jnp.where(strict, jnp.einsum('bhnid,bhnjd->bhnij', kb, k) * decay, 0.0)
    eye = jnp.eye(C, dtype=jnp.float32)
    rhs = jnp.concatenate([v * beta[..., None], kb * jnp.exp(g)[..., None]], axis=-1)
    sol = lax.linalg.triangular_solve(eye + L, rhs, left_side=True, lower=True)
    u, w = sol[..., :Dv], sol[..., Dv:]
    attn = jnp.where(causal, jnp.einsum('bhnid,bhnjd->bhnij', q, k) * decay, 0.0)
    q_dec = q * jnp.exp(g)[..., None]
    k_dec = k * jnp.exp(g[..., -1:] - g)[..., None]
    g_tot = jnp.exp(g[..., -1])
    xs = tuple(jnp.moveaxis(t, 2, 0) for t in (u, w, q_dec, k_dec, attn, g_tot))

    def step(S, inp):
        u_i, w_i, qd_i, kd_i, a_i, gt_i = inp
        v_new = u_i - jnp.einsum('bhck,bhkv->bhcv', w_i, S)
        o_i = jnp.einsum('bhck,bhkv->bhcv', qd_i, S) + jnp.einsum('bhcj,bhjv->bhcv', a_i, v_new)
        S = S * gt_i[..., None, None] + jnp.einsum('bhck,bhcv->bhkv', kd_i, v_new)
        return S, o_i

    S0 = jnp.zeros((B, H, Dk, Dv), jnp.float32)
    _, o = lax.scan(step, S0, xs)
    return jnp.moveaxis(o, 0, 2).reshape(B, H, T, Dv)


def gdn_mixer(q, k, v, z, b, a, conv_w, A_log, dt_bias, out_norm_w):
    B, T, _ = q.shape
    qkv = jax.nn.silu(causal_depthwise_conv(jnp.concatenate([q, k, v], axis=-1), conv_w))
    q, k, v = jnp.split(qkv, 3, axis=-1)

    def heads(t):
        return t.reshape(B, T, GDN_HEADS, HEAD_DIM).transpose(0, 2, 1, 3).astype(jnp.float32)

    qh = l2norm(heads(q)) * (HEAD_DIM ** -0.5)
    kh = l2norm(heads(k))
    vh = heads(v)
    beta = jax.nn.sigmoid(b.astype(jnp.float32)).transpose(0, 2, 1)
    g = -(jnp.exp(A_log.astype(jnp.float32))
          * jax.nn.softplus(a.astype(jnp.float32) + dt_bias.astype(jnp.float32)))
    g = g.transpose(0, 2, 1)
    o = gated_delta_rule_chunked(qh, kh, vh, g, beta).transpose(0, 2, 1, 3)
    zf = z.reshape(B, T, GDN_HEADS, HEAD_DIM).astype(jnp.float32)
    o = rmsnorm(o, out_norm_w) * jax.nn.silu(zf)
    return o.reshape(B, T, GDN_WIDTH).astype(q.dtype)


def moba_mixer(q, k, v, out_norm_w):
    B, T, _ = q.shape
    H, D = MOBA_HEADS, HEAD_DIM

    def heads(t):
        return t.reshape(B, T, H, D).transpose(0, 2, 1, 3).astype(jnp.float32)

    qh, kh, vh = heads(q), heads(k), heads(v)
    n_blocks = -(-T // MOBA_BLOCK)
    t_pad = n_blocks * MOBA_BLOCK
    pad = [(0, 0), (0, 0), (0, t_pad - T), (0, 0)]
    k_blk = jnp.pad(kh, pad).reshape(B, H, n_blocks, MOBA_BLOCK, D)
    v_blk = jnp.pad(vh, pad).reshape(B, H, n_blocks, MOBA_BLOCK, D)
    k_mean = jnp.mean(k_blk, axis=3)
    n_sel = min(MOBA_TOPK, n_blocks)
    q_block = jnp.arange(T) // MOBA_BLOCK
    gate = jnp.einsum('bhtd,bhnd->bhtn', qh, k_mean)
    gate = jnp.where(jnp.arange(n_blocks)[None, :] < q_block[:, None], gate, NEG_INF)
    _, sel_idx = lax.top_k(gate, n_sel)
    slopes = alibi_slopes(H)
    scale = HEAD_DIM ** -0.5
    QC = MOBA_Q_CHUNK
    n_qc = T // QC
    q_c = qh.reshape(B, H, n_qc, QC, D).transpose(2, 0, 1, 3, 4)
    idx_c = sel_idx.reshape(B, H, n_qc, QC, n_sel).transpose(2, 0, 1, 3, 4)
    b_ix = jnp.arange(B)[:, None, None, None]
    h_ix = jnp.arange(H)[None, :, None, None]
    offs = jnp.arange(MOBA_BLOCK)

    def chunk(args):
        qi, idx, c = args
        start = c * QC
        blk = start // MOBA_BLOCK
        t_pos = start + jnp.arange(QC)
        k_own = lax.dynamic_index_in_dim(k_blk, blk, axis=2, keepdims=False)
        v_own = lax.dynamic_index_in_dim(v_blk, blk, axis=2, keepdims=False)
        dist_own = (t_pos[:, None] - (blk * MOBA_BLOCK + offs)[None, :]).astype(jnp.float32)
        s_own = (jnp.einsum('bhqd,bhkd->bhqk', qi, k_own) * scale
                 - slopes[None, :, None, None] * dist_own)
        s_own = jnp.where(dist_own >= 0, s_own, NEG_INF)
        k_sel = k_blk[b_ix, h_ix, idx]
        v_sel = v_blk[b_ix, h_ix, idx]
        dist_sel = (t_pos[None, None, :, None, None]
                    - (idx[..., None] * MOBA_BLOCK + offs)).astype(jnp.float32)
        s_sel = (jnp.einsum('bhqd,bhqskd->bhqsk', qi, k_sel) * scale
                 - slopes[None, :, None, None, None] * dist_sel)
        valid = jnp.arange(n_sel) < blk
        s_sel = jnp.where(valid[:, None], s_sel, NEG_INF).reshape(B, H, QC, n_sel * MOBA_BLOCK)
        p = jax.nn.softmax(jnp.concatenate([s_own, s_sel], axis=-1), axis=-1)
        p_own = p[..., :MOBA_BLOCK]
        p_sel = p[..., MOBA_BLOCK:].reshape(B, H, QC, n_sel, MOBA_BLOCK)
        return (jnp.einsum('bhqk,bhkd->bhqd', p_own, v_own)
                + jnp.einsum('bhqsk,bhqskd->bhqd', p_sel, v_sel))

    o = lax.map(chunk, (q_c, idx_c, jnp.arange(n_qc)))
    o = o.transpose(1, 0, 3, 2, 4).reshape(B, T, H, D)
    o = rmsnorm(o, out_norm_w)
    return o.reshape(B, T, MOBA_WIDTH).astype(q.dtype)


def grouped_expert_ffn(xf, expert_id, w_gate, w_up, w_down):
    n_tok, k = expert_id.shape
    d = xf.shape[-1]
    RB = EXPERT_ROW_BLOCK
    n_assign = n_tok * k
    e_flat = expert_id.reshape(-1)
    tok = jnp.arange(n_assign) // k
    order = jnp.argsort(e_flat)
    e_sorted = e_flat[order]
    counts = jnp.bincount(e_flat, length=N_EXPERTS)
    padded = (counts + RB - 1) // RB * RB
    pad_end = jnp.cumsum(padded)
    pad_start = pad_end - padded
    seg_start = jnp.cumsum(counts) - counts
    dest = pad_start[e_sorted] + jnp.arange(n_assign) - seg_start[e_sorted]
    n_rb = -(-n_assign // RB) + N_EXPERTS
    src_tok = jnp.zeros((n_rb * RB,), jnp.int32).at[dest].set(tok[order])
    blk_expert = jnp.minimum(jnp.searchsorted(pad_end, jnp.arange(n_rb) * RB, side='right'), N_EXPERTS - 1)
    xb = xf[src_tok].reshape(n_rb, RB, d)

    def run(args):
        xi, e = args
        h = jax.nn.silu(xi @ w_gate[e]) * (xi @ w_up[e])
        return h @ w_down[e]

    yb = lax.map(run, (xb, blk_expert)).reshape(n_rb * RB, d)
    y = jnp.zeros((n_assign, d), yb.dtype).at[order].set(yb[dest])
    return y.reshape(n_tok, k, d)


def hierarchical_moe(x, w_rg, b_rg, w_re, b_re, w_gate, w_up, w_down):
    B, T, D = x.shape
    xf = x.reshape(-1, D)
    n_tok = xf.shape[0]
    p_g = jax.nn.softmax((xf @ w_rg).astype(jnp.float32) + b_rg.astype(jnp.float32), axis=-1)
    p_top_g, g_idx = lax.top_k(p_g, 1)
    logits_e = ((xf @ w_re).astype(jnp.float32) + b_re.astype(jnp.float32)).reshape(
        n_tok, N_GROUPS, EXPERTS_PER_GROUP)
    logits_e = jnp.take_along_axis(logits_e, g_idx[:, :, None], axis=1)[:, 0]
    p_e = jax.nn.softmax(logits_e, axis=-1)
    p_top_e, e_idx = lax.top_k(p_e, TOPK_IN_GROUP)
    gates = p_top_g * p_top_e / jnp.sum(p_top_e, axis=-1, keepdims=True)
    expert_id = g_idx * EXPERTS_PER_GROUP + e_idx
    y = grouped_expert_ffn(xf, expert_id, w_gate, w_up, w_down)
    out = jnp.einsum('nk,nkd->nd', gates, y.astype(jnp.float32))
    return out.reshape(B, T, D).astype(x.dtype)


def setup_inputs(seed: int = 0) -> dict:
    key = jax.random.key(seed)
    ks = jax.random.split(key, 20)
    f32 = jnp.float32

    def nrm(k, shape, scale):
        return jax.random.normal(k, shape, f32) * scale

    x = nrm(ks[0], (BATCH, SEQ, D_MODEL), 1.0)
    norm_mix_w = 1.0 + nrm(ks[1], (DEPTH, D_MODEL), 0.02)
    w_in = nrm(ks[2], (DEPTH, D_MODEL, IN_PROJ_WIDTH), D_MODEL ** -0.5)
    gdn_conv_w = nrm(ks[3], (DEPTH, CONV_WIDTH, 3 * GDN_WIDTH), CONV_WIDTH ** -0.5)
    gdn_A_log = jnp.log(jax.random.uniform(ks[4], (DEPTH, GDN_HEADS), f32, 1.0, 16.0))
    dt = jnp.exp(jax.random.uniform(ks[5], (DEPTH, GDN_HEADS), f32, float(np.log(1e-3)), float(np.log(1e-1))))
    gdn_dt_bias = dt + jnp.log(-jnp.expm1(-dt))
    gdn_out_norm_w = 1.0 + nrm(ks[6], (DEPTH, HEAD_DIM), 0.02)
    moba_out_norm_w = 1.0 + nrm(ks[7], (DEPTH, HEAD_DIM), 0.02)
    w_out = nrm(ks[8], (DEPTH, MIX_WIDTH, D_MODEL), MIX_WIDTH ** -0.5)
    norm_ffn_w = 1.0 + nrm(ks[9], (DEPTH, D_MODEL), 0.02)
    w_router_group = nrm(ks[10], (DEPTH, D_MODEL, N_GROUPS), D_MODEL ** -0.5)
    b_router_group = nrm(ks[11], (DEPTH, N_GROUPS), 0.01)
    w_router_expert = nrm(ks[12], (DEPTH, D_MODEL, N_EXPERTS), D_MODEL ** -0.5)
    b_router_expert = nrm(ks[13], (DEPTH, N_EXPERTS), 0.01)
    w_expert_gate = nrm(ks[14], (DEPTH, N_EXPERTS, D_MODEL, D_EXPERT), D_MODEL ** -0.5)
    w_expert_up = nrm(ks[15], (DEPTH, N_EXPERTS, D_MODEL, D_EXPERT), D_MODEL ** -0.5)
    w_expert_down = nrm(ks[16], (DEPTH, N_EXPERTS, D_EXPERT, D_MODEL), D_EXPERT ** -0.5)
    norm_final_w = 1.0 + nrm(ks[17], (D_MODEL,), 0.02)
    return {'x': x, 'norm_mix_w': norm_mix_w, 'w_in': w_in, 'gdn_conv_w': gdn_conv_w,
            'gdn_A_log': gdn_A_log, 'gdn_dt_bias': gdn_dt_bias, 'gdn_out_norm_w': gdn_out_norm_w,
            'moba_out_norm_w': moba_out_norm_w, 'w_out': w_out, 'norm_ffn_w': norm_ffn_w,
            'w_router_group': w_router_group, 'b_router_group': b_router_group,
            'w_router_expert': w_router_expert, 'b_router_expert': b_router_expert,
            'w_expert_gate': w_expert_gate, 'w_expert_up': w_expert_up,
            'w_expert_down': w_expert_down, 'norm_final_w': norm_final_w}


def reference(x, norm_mix_w, w_in, gdn_conv_w, gdn_A_log, gdn_dt_bias, gdn_out_norm_w,
              moba_out_norm_w, w_out, norm_ffn_w, w_router_group, b_router_group,
              w_router_expert, b_router_expert, w_expert_gate, w_expert_up, w_expert_down,
              norm_final_w):
    for l in range(DEPTH):
        h = rmsnorm(x, norm_mix_w[l])
        proj = h @ w_in[l]
        gq, gk, gv, gz, gb, ga, mq, mk, mv = split_columns(proj, IN_SPLITS)
        o_gdn = gdn_mixer(gq, gk, gv, gz, gb, ga, gdn_conv_w[l], gdn_A_log[l],
                          gdn_dt_bias[l], gdn_out_norm_w[l])
        o_moba = moba_mixer(mq, mk, mv, moba_out_norm_w[l])
        mix = jnp.concatenate([o_gdn, o_moba], axis=-1)
        x = x + (mix @ w_out[l]).astype(x.dtype)
        h = rmsnorm(x, norm_ffn_w[l])
        x = x + hierarchical_moe(h, w_router_group[l], b_router_group[l], w_router_expert[l],
                                 b_router_expert[l], w_expert_gate[l], w_expert_up[l],
                                 w_expert_down[l])
    return rmsnorm(x, norm_final_w)
```

```python
import functools

import jax
import jax.numpy as jnp
import numpy as np
from jax import lax
from jax.experimental import pallas as pl
from jax.experimental.pallas import tpu as pltpu

F32 = jnp.float32
BF16 = jnp.bfloat16

HEAD_DIM = 128
GDN_HEADS = 8
MOBA_HEADS = 8
GDN_WIDTH = GDN_HEADS * HEAD_DIM
MOBA_WIDTH = MOBA_HEADS * HEAD_DIM
CONV_WIDTH = 4
GDN_CHUNK = 64
MOBA_BLOCK = 256
MOBA_TOPK = 3
N_GROUPS = 4
EXPERTS_PER_GROUP = 8
N_EXPERTS = N_GROUPS * EXPERTS_PER_GROUP
TOPK_IN_GROUP = 2
RMS_EPS = 1e-6
NEG_INF = -1e30
LANES = 128
VMEM_LIMIT = 56 * 1024 * 1024

GDN_TILE = 256
MOE_ROWS = 128
COMBINE_ROWS = 128
HI = lax.Precision.HIGHEST


def _cparams(sem):
    return pltpu.CompilerParams(dimension_semantics=sem, vmem_limit_bytes=VMEM_LIMIT)


def _dot(a, b):
    return jnp.dot(a, b, preferred_element_type=F32)


def _dot_nt(a, b, precision=None):
    return lax.dot_general(a, b, (((1,), (1,)), ((), ())), preferred_element_type=F32,
                           precision=precision)


def _sigmoid(x):
    return 1.0 / (1.0 + jnp.exp(-x))


def _norm_matmul_body(x_ref, nw_ref, w_ref, o_ref):
    x = x_ref[...]
    ms = jnp.mean(x * x, axis=-1, keepdims=True)
    h = (x * lax.rsqrt(ms + RMS_EPS) * nw_ref[...]).astype(BF16)
    o_ref[...] = _dot(h, w_ref[...]).astype(o_ref.dtype)


def norm_matmul(x, nw, w, out_dtype, tm, tn):
    n, d = x.shape
    width = w.shape[1]
    return pl.pallas_call(
        _norm_matmul_body,
        grid=(width // tn, n // tm),
        in_specs=[pl.BlockSpec((tm, d), lambda j, i: (i, 0)),
                  pl.BlockSpec((1, d), lambda j, i: (0, 0)),
                  pl.BlockSpec((d, tn), lambda j, i: (0, j))],
        out_specs=pl.BlockSpec((tm, tn), lambda j, i: (i, j)),
        out_shape=jax.ShapeDtypeStruct((n, width), out_dtype),
        compiler_params=_cparams(("arbitrary", "arbitrary")),
        name="norm_in_proj",
    )(x, nw.reshape(1, d), w)


def _gdn_body(alog_ref, dtb_ref, q_ref, k_ref, v_ref, z_ref, ba_ref, cwq_ref, cwk_ref, cwv_ref,
              onw_ref, o_ref, s_ref, prev_ref, vn_ref, oacc_ref):
    tt = GDN_TILE
    c = GDN_CHUNK
    h = pl.program_id(1)
    t = pl.program_id(2)

    @pl.when(t == 0)
    def _():
        s_ref[...] = jnp.zeros_like(s_ref)
        prev_ref[...] = jnp.zeros_like(prev_ref)
        vn_ref[...] = jnp.zeros_like(vn_ref)

    row_8 = lax.broadcasted_iota(jnp.int32, (8, LANES), 0)

    def conv_silu(x_ref, cw_ref, idx):
        x = x_ref[...].astype(F32)
        p = prev_ref[idx]
        w = cw_ref[...]
        acc = x * w[CONV_WIDTH - 1:CONV_WIDTH, :]
        for s in range(1, CONV_WIDTH):
            xs = pltpu.roll(x, s, axis=0)
            head = jnp.where(row_8 < s, pltpu.roll(p, s, axis=0), xs[:8])
            xs = jnp.concatenate([head, xs[8:]], axis=0)
            acc = acc + xs * w[CONV_WIDTH - 1 - s:CONV_WIDTH - s, :]
        prev_ref[idx] = x[tt - 8:tt, :]
        return acc * _sigmoid(acc)

    q = conv_silu(q_ref, cwq_ref, 0)
    k = conv_silu(k_ref, cwk_ref, 1)
    v = conv_silu(v_ref, cwv_ref, 2)
    q = q * lax.rsqrt(jnp.sum(q * q, axis=-1, keepdims=True) + 1e-6) * (HEAD_DIM ** -0.5)
    k = k * lax.rsqrt(jnp.sum(k * k, axis=-1, keepdims=True) + 1e-6)

    ba = ba_ref[...]
    lane_t = lax.broadcasted_iota(jnp.int32, (tt, LANES), 1)
    b_col = jnp.sum(jnp.where(lane_t == h, ba, 0.0), axis=-1, keepdims=True)
    a_col = jnp.sum(jnp.where(lane_t == h + GDN_HEADS, ba, 0.0), axis=-1, keepdims=True)
    beta = _sigmoid(b_col)
    xa = a_col + dtb_ref[h]
    softplus = jnp.maximum(xa, 0.0) + jnp.log(1.0 + jnp.exp(-jnp.abs(xa)))
    g = -jnp.exp(jnp.full((1, 1), alog_ref[h], F32)) * softplus

    row = lax.broadcasted_iota(jnp.int32, (tt, tt), 0)
    col = lax.broadcasted_iota(jnp.int32, (tt, tt), 1)
    same = (row // c) == (col // c)
    causal = jnp.logical_and(same, col <= row)
    strict = jnp.logical_and(same, col < row)

    g_b = jnp.broadcast_to(g, (tt, LANES))
    gc_b = jnp.dot(jnp.where(causal, 1.0, 0.0), g_b, preferred_element_type=F32, precision=HI)
    glast_b = jnp.dot(jnp.where(same, 1.0, 0.0), g_b, preferred_element_type=F32, precision=HI)
    gc_col = gc_b[:, 0:1]
    gc_row = jnp.transpose(gc_b)[0:1, :]
    decay = jnp.exp(jnp.where(causal, gc_col - gc_row, NEG_INF))

    kb = k * beta
    k16 = k.astype(BF16)
    kk = _dot_nt(kb.astype(BF16), k16)
    lmat = jnp.where(strict, kk * decay, 0.0)
    attn = _dot_nt(q.astype(BF16), k16) * decay

    eye = jnp.where(row == col, 1.0, 0.0)
    tinv = eye - lmat
    m = lmat
    for _ in range(5):
        m16 = m.astype(BF16)
        m = _dot(m16, m16)
        tinv = tinv + _dot(tinv.astype(BF16), m.astype(BF16))

    egc = jnp.exp(gc_b)
    rhs = jnp.concatenate([v * beta, kb * egc], axis=-1)
    sol = _dot(tinv.astype(BF16), rhs.astype(BF16))
    u = sol[:, :HEAD_DIM]
    w16 = sol[:, HEAD_DIM:].astype(BF16)
    qd16 = (q * egc).astype(BF16)
    kd = k * jnp.exp(glast_b - gc_b)
    kdt = jnp.transpose(kd)
    gtot_b = jnp.exp(glast_b)
    attn16 = attn.astype(BF16)
    colk = lax.broadcasted_iota(jnp.int32, (HEAD_DIM, tt), 1)

    s = s_ref[...]
    for n in range(tt // c):
        sl = slice(n * c, (n + 1) * c)
        s16 = s.astype(BF16)
        v_new = u[sl] - _dot(w16[sl], s16)
        vn_ref[sl, :] = v_new
        vn16 = vn_ref[...].astype(BF16)
        oacc_ref[sl, :] = _dot(qd16[sl], s16) + _dot(attn16[sl], vn16)
        kdt_n = jnp.where((colk // c) == n, kdt, 0.0).astype(BF16)
        s = s * gtot_b[n * c:n * c + 1, :] + _dot(kdt_n, vn16)
    s_ref[...] = s

    o = oacc_ref[...]
    o = o * lax.rsqrt(jnp.mean(o * o, axis=-1, keepdims=True) + RMS_EPS) * onw_ref[...]
    z = z_ref[...].astype(F32)
    o_ref[...] = (o * (z * _sigmoid(z))).astype(o_ref.dtype)


def gdn_heads(proj, ba, conv_w, a_log, dt_bias, out_norm_w, batch, seq):
    n = batch * seq
    tt = GDN_TILE
    nt = seq // tt
    hh = GDN_HEADS

    def col_spec(off):
        return pl.BlockSpec((tt, HEAD_DIM), lambda b, h, t, *_: (b * nt + t, off + h))

    def cw_spec(off):
        return pl.BlockSpec((CONV_WIDTH, HEAD_DIM), lambda b, h, t, *_: (0, off + h))

    grid_spec = pltpu.PrefetchScalarGridSpec(
        num_scalar_prefetch=2,
        grid=(batch, hh, nt),
        in_specs=[col_spec(0), col_spec(hh), col_spec(2 * hh), col_spec(3 * hh),
                  pl.BlockSpec((tt, LANES), lambda b, h, t, *_: (b * nt + t, 0)),
                  cw_spec(0), cw_spec(hh), cw_spec(2 * hh),
                  pl.BlockSpec((1, HEAD_DIM), lambda b, h, t, *_: (0, 0))],
        out_specs=pl.BlockSpec((tt, HEAD_DIM), lambda b, h, t, *_: (b * nt + t, h)),
        scratch_shapes=[pltpu.VMEM((HEAD_DIM, HEAD_DIM), F32),
                        pltpu.VMEM((3, 8, HEAD_DIM), F32),
                        pltpu.VMEM((tt, HEAD_DIM), F32),
                        pltpu.VMEM((tt, HEAD_DIM), F32)])
    return pl.pallas_call(
        _gdn_body,
        grid_spec=grid_spec,
        out_shape=jax.ShapeDtypeStruct((n, GDN_WIDTH), BF16),
        compiler_params=_cparams(("arbitrary", "arbitrary", "arbitrary")),
        name="gdn_heads",
    )(a_log.astype(F32), dt_bias.astype(F32), proj, proj, proj, proj, ba,
      conv_w, conv_w, conv_w, out_norm_w.reshape(1, HEAD_DIM))


def _moba_body(slope_ref, q_ref, k_ref, v_ref, onw_ref, o_ref, kmean_ref, *, n_blocks):
    blk = MOBA_BLOCK
    h = pl.program_id(1)
    c = pl.program_id(2)
    scale = HEAD_DIM ** -0.5
    slope = slope_ref[h]

    @pl.when(c == 0)
    def _():
        kmean_ref[...] = jnp.zeros_like(kmean_ref)
        for j in range(n_blocks):
            kj = k_ref[j * blk:(j + 1) * blk, :].astype(F32)
            kmean_ref[j:j + 1, :] = jnp.mean(kj, axis=0, keepdims=True)

    q16 = q_ref[...]
    gate = _dot_nt(q16.astype(F32), kmean_ref[...], precision=HI)
    lane = lax.broadcasted_iota(jnp.int32, (blk, LANES), 1)
    gm = jnp.where(lane < c, gate, NEG_INF)
    sel = jnp.zeros((blk, LANES), F32)
    for s in range(MOBA_TOPK):
        mx = jnp.max(gm, axis=-1, keepdims=True)
        idx = jnp.min(jnp.where(gm == mx, lane, LANES), axis=-1, keepdims=True)
        pick = lane == idx
        sel = jnp.where(jnp.logical_and(pick, s < c), 1.0, sel)
        gm = jnp.where(pick, -3e38, gm)

    row = lax.broadcasted_iota(jnp.int32, (blk, blk), 0)
    col = lax.broadcasted_iota(jnp.int32, (blk, blk), 1)
    rel = (row - col).astype(F32)

    k_own = k_ref[pl.ds(pl.multiple_of(c * blk, blk), blk), :]
    v_own = v_ref[pl.ds(pl.multiple_of(c * blk, blk), blk), :]
    s_own = _dot_nt(q16, k_own) * scale - slope * rel
    s_own = jnp.where(col <= row, s_own, NEG_INF)
    m0 = jnp.max(s_own, axis=-1, keepdims=True)
    p0 = jnp.exp(s_own - m0)
    l0 = jnp.sum(p0, axis=-1, keepdims=True)
    acc0 = _dot(p0.astype(BF16), v_own)

    def body(j, carry):
        m_i, l_i, acc = carry
        off = pl.multiple_of(j * blk, blk)
        kj = k_ref[pl.ds(off, blk), :]
        vj = v_ref[pl.ds(off, blk), :]
        sel_j = jnp.sum(jnp.where(lane == j, sel, 0.0), axis=-1, keepdims=True)
        dist0 = ((c - j) * blk).astype(F32)
        sj = _dot_nt(q16, kj) * scale - slope * (rel + dist0)
        sj = jnp.where(sel_j > 0.5, sj, NEG_INF)
        m_new = jnp.maximum(m_i, jnp.max(sj, axis=-1, keepdims=True))
        alpha = jnp.exp(m_i - m_new)
        p = jnp.exp(sj - m_new)
        l_new = l_i * alpha + jnp.sum(p, axis=-1, keepdims=True)
        acc_new = acc * alpha + _dot(p.astype(BF16), vj)
        return m_new, l_new, acc_new

    _, l_f, acc_f = lax.fori_loop(0, c, body, (m0, l0, acc0))
    o = acc_f / l_f
    o = o * lax.rsqrt(jnp.mean(o * o, axis=-1, keepdims=True) + RMS_EPS) * onw_ref[...]
    o_ref[...] = o.astype(o_ref.dtype)


def moba_heads(proj, out_norm_w, batch, seq, col_off):
    n = batch * seq
    blk = MOBA_BLOCK
    nb = seq // blk
    hh = MOBA_HEADS
    slopes = jnp.exp2(-8.0 * jnp.arange(1, hh + 1, dtype=F32) / hh)
    grid_spec = pltpu.PrefetchScalarGridSpec(
        num_scalar_prefetch=1,
        grid=(batch, hh, nb),
        in_specs=[pl.BlockSpec((blk, HEAD_DIM), lambda b, h, c, *_: (b * nb + c, col_off + h)),
                  pl.BlockSpec((seq, HEAD_DIM), lambda b, h, c, *_: (b, col_off + hh + h)),
                  pl.BlockSpec((seq, HEAD_DIM), lambda b, h, c, *_: (b, col_off + 2 * hh + h)),
                  pl.BlockSpec((1, HEAD_DIM), lambda b, h, c, *_: (0, 0))],
        out_specs=pl.BlockSpec((blk, HEAD_DIM), lambda b, h, c, *_: (b * nb + c, h)),
        scratch_shapes=[pltpu.VMEM((LANES, HEAD_DIM), F32)])
    return pl.pallas_call(
        functools.partial(_moba_body, n_blocks=nb),
        grid_spec=grid_spec,
        out_shape=jax.ShapeDtypeStruct((n, MOBA_WIDTH), BF16),
        compiler_params=_cparams(("arbitrary", "arbitrary", "arbitrary")),
        name="moba_heads",
    )(slopes, proj, proj, proj, out_norm_w.reshape(1, HEAD_DIM))


def _out_proj_body(x_ref, og_ref, om_ref, wg_ref, wm_ref, nw_ref, wr_ref, x2_ref, h2_ref, lg_ref):
    x2 = x_ref[...] + _dot(og_ref[...], wg_ref[...]) + _dot(om_ref[...], wm_ref[...])
    x2_ref[...] = x2
    ms = jnp.mean(x2 * x2, axis=-1, keepdims=True)
    h2 = x2 * lax.rsqrt(ms + RMS_EPS) * nw_ref[...]
    h2_ref[...] = h2
    lg_ref[...] = jnp.dot(h2, wr_ref[...], preferred_element_type=F32, precision=HI)


def out_proj(x, og, om, w_g, w_m, nw, w_router, tm):
    n, d = x.shape
    return pl.pallas_call(
        _out_proj_body,
        grid=(n // tm,),
        in_specs=[pl.BlockSpec((tm, d), lambda i: (i, 0)),
                  pl.BlockSpec((tm, GDN_WIDTH), lambda i: (i, 0)),
                  pl.BlockSpec((tm, MOBA_WIDTH), lambda i: (i, 0)),
                  pl.BlockSpec((GDN_WIDTH, d), lambda i: (0, 0)),
                  pl.BlockSpec((MOBA_WIDTH, d), lambda i: (0, 0)),
                  pl.BlockSpec((1, d), lambda i: (0, 0)),
                  pl.BlockSpec((d, LANES), lambda i: (0, 0))],
        out_specs=[pl.BlockSpec((tm, d), lambda i: (i, 0)),
                   pl.BlockSpec((tm, d), lambda i: (i, 0)),
                   pl.BlockSpec((tm, LANES), lambda i: (i, 0))],
        out_shape=[jax.ShapeDtypeStruct((n, d), F32),
                   jax.ShapeDtypeStruct((n, d), F32),
                   jax.ShapeDtypeStruct((n, LANES), F32)],
        compiler_params=_cparams(("arbitrary",)),
        name="out_proj_router",
    )(x, og, om, w_g, w_m, nw.reshape(1, d), w_router)


def _gather_rows(idx_ref, base, src_hbm, dst, sem, rows):
    def body(r, carry):
        tok = idx_ref[base + r]
        pltpu.make_async_copy(src_hbm.at[pl.ds(tok, 1)], dst.at[pl.ds(r, 1)], sem).start()
        return carry
    lax.fori_loop(0, rows, body, 0)


def _wait_rows(src_hbm, dst, sem, rows):
    pltpu.make_async_copy(src_hbm.at[pl.ds(0, rows)], dst, sem).wait()


def _moe_body(blk_e_ref, src_ref, nblk_ref, h_hbm, wg_ref, wu_ref, wd_ref, y_ref, buf, sem):
    rb = MOE_ROWS
    i = pl.program_id(0)
    n_steps = pl.num_programs(0)
    slot = lax.rem(i, 2)

    @pl.when(i == 0)
    def _():
        _gather_rows(src_ref, 0, h_hbm, buf.at[0], sem.at[0], rb)

    @pl.when(i + 1 < n_steps)
    def _():
        _gather_rows(src_ref, (i + 1) * rb, h_hbm, buf.at[1 - slot], sem.at[1 - slot], rb)

    _wait_rows(h_hbm, buf.at[slot], sem.at[slot], rb)

    @pl.when(i < nblk_ref[0])
    def _():
        xb = buf[slot].astype(BF16)
        g = _dot(xb, wg_ref[0])
        u = _dot(xb, wu_ref[0])
        hm = (g * _sigmoid(g) * u).astype(BF16)
        y_ref[...] = _dot(hm, wd_ref[0])

    @pl.when(i >= nblk_ref[0])
    def _():
        y_ref[...] = jnp.zeros_like(y_ref)


def moe_experts(h2, blk_expert, src_tok, n_used, w_gate, w_up, w_down):
    n, d = h2.shape
    rb = MOE_ROWS
    n_rb = blk_expert.shape[0]
    de = w_gate.shape[-1]
    grid_spec = pltpu.PrefetchScalarGridSpec(
        num_scalar_prefetch=3,
        grid=(n_rb,),
        in_specs=[pl.BlockSpec(memory_space=pl.ANY),
                  pl.BlockSpec((1, d, de), lambda i, be, st, nu: (be[i], 0, 0)),
                  pl.BlockSpec((1, d, de), lambda i, be, st, nu: (be[i], 0, 0)),
                  pl.BlockSpec((1, de, d), lambda i, be, st, nu: (be[i], 0, 0))],
        out_specs=pl.BlockSpec((rb, d), lambda i, be, st, nu: (i, 0)),
        scratch_shapes=[pltpu.VMEM((2, rb, d), F32), pltpu.SemaphoreType.DMA((2,))])
    return pl.pallas_call(
        _moe_body,
        grid_spec=grid_spec,
        out_shape=jax.ShapeDtypeStruct((n_rb * rb, d), F32),
        compiler_params=_cparams(("arbitrary",)),
        name="moe_experts",
    )(blk_expert, src_tok, n_used, h2, w_gate, w_up, w_down)


def _combine_body(pos0_ref, pos1_ref, y_hbm, x2_ref, gt_ref, nw_ref, o_ref, buf, sem):
    tf = COMBINE_ROWS
    i = pl.program_id(0)
    n_steps = pl.num_programs(0)
    slot = lax.rem(i, 2)

    def issue(step, sl):
        _gather_rows(pos0_ref, step * tf, y_hbm, buf.at[sl, 0], sem.at[sl], tf)
        _gather_rows(pos1_ref, step * tf, y_hbm, buf.at[sl, 1], sem.at[sl], tf)

    @pl.when(i == 0)
    def _():
        issue(0, 0)

    @pl.when(i + 1 < n_steps)
    def _():
        issue(i + 1, 1 - slot)

    _wait_rows(y_hbm, buf.at[slot, 0], sem.at[slot], tf)
    _wait_rows(y_hbm, buf.at[slot, 1], sem.at[slot], tf)

    gt = gt_ref[...]
    xo = x2_ref[...] + gt[:, 0:1] * buf[slot, 0] + gt[:, 1:2] * buf[slot, 1]
    ms = jnp.mean(xo * xo, axis=-1, keepdims=True)
    o_ref[...] = xo * lax.rsqrt(ms + RMS_EPS) * nw_ref[...]


def moe_combine(yb, pos0, pos1, x2, gates_pad, nw):
    n, d = x2.shape
    tf = COMBINE_ROWS
    grid_spec = pltpu.PrefetchScalarGridSpec(
        num_scalar_prefetch=2,
        grid=(n // tf,),
        in_specs=[pl.BlockSpec(memory_space=pl.ANY),
                  pl.BlockSpec((tf, d), lambda i, *_: (i, 0)),
                  pl.BlockSpec((tf, LANES), lambda i, *_: (i, 0)),
                  pl.BlockSpec((1, d), lambda i, *_: (0, 0))],
        out_specs=pl.BlockSpec((tf, d), lambda i, *_: (i, 0)),
        scratch_shapes=[pltpu.VMEM((2, 2, tf, d), F32), pltpu.SemaphoreType.DMA((2,))])
    return pl.pallas_call(
        _combine_body,
        grid_spec=grid_spec,
        out_shape=jax.ShapeDtypeStruct((n, d), F32),
        compiler_params=_cparams(("arbitrary",)),
        name="moe_combine",
    )(pos0, pos1, yb, x2, gates_pad, nw.reshape(1, d))


def _route(logits, b_rg, b_re):
    n_tok = logits.shape[0]
    p_g = jax.nn.softmax(logits[:, :N_GROUPS] + b_rg.astype(F32), axis=-1)
    p_top_g, g_idx = lax.top_k(p_g, 1)
    logits_e = (logits[:, N_GROUPS:N_GROUPS + N_EXPERTS] + b_re.astype(F32)).reshape(
        n_tok, N_GROUPS, EXPERTS_PER_GROUP)
    logits_e = jnp.take_along_axis(logits_e, g_idx[:, :, None], axis=1)[:, 0]
    p_e = jax.nn.softmax(logits_e, axis=-1)
    p_top_e, e_idx = lax.top_k(p_e, TOPK_IN_GROUP)
    gates = p_top_g * p_top_e / jnp.sum(p_top_e, axis=-1, keepdims=True)
    expert_id = g_idx * EXPERTS_PER_GROUP + e_idx
    return gates, expert_id


def _dispatch_plan(expert_id):
    n_tok, k = expert_id.shape
    rb = MOE_ROWS
    n_assign = n_tok * k
    e_flat = expert_id.reshape(-1).astype(jnp.int32)
    order = jnp.argsort(e_flat)
    e_sorted = e_flat[order]
    counts = jnp.bincount(e_flat, length=N_EXPERTS)
    padded = (counts + rb - 1) // rb * rb
    pad_end = jnp.cumsum(padded)
    pad_start = pad_end - padded
    seg_start = jnp.cumsum(counts) - counts
    dest = pad_start[e_sorted] + jnp.arange(n_assign) - seg_start[e_sorted]
    n_rb = -(-n_assign // rb) + N_EXPERTS
    src_tok = jnp.zeros((n_rb * rb,), jnp.int32).at[dest].set((order // k).astype(jnp.int32))
    blk_expert = jnp.minimum(jnp.searchsorted(pad_end, jnp.arange(n_rb) * rb, side='right'),
                             N_EXPERTS - 1).astype(jnp.int32)
    pos = jnp.zeros((n_assign,), jnp.int32).at[order].set(dest.astype(jnp.int32)).reshape(n_tok, k)
    n_used = (pad_end[-1] // rb).astype(jnp.int32).reshape(1)
    return src_tok, blk_expert, pos, n_used


def kernel(x, norm_mix_w, w_in, gdn_conv_w, gdn_A_log, gdn_dt_bias, gdn_out_norm_w, moba_out_norm_w, w_out, norm_ffn_w, w_router_group, b_router_group, w_router_expert, b_router_expert, w_expert_gate, w_expert_up, w_expert_down, norm_final_w):
    batch, seq, d = x.shape
    n = batch * seq
    depth = w_in.shape[0]
    assert depth == 1, "the final norm is fused into the last layer's combine; one layer supported"
    xf = x.reshape(n, d).astype(F32)
    gw = 4 * GDN_WIDTH
    for l in range(depth):
        w_l = w_in[l]
        w_main = jnp.concatenate([w_l[:, :gw], w_l[:, gw + 2 * GDN_HEADS:]], axis=1).astype(BF16)
        w_ba = jnp.pad(w_l[:, gw:gw + 2 * GDN_HEADS], ((0, 0), (0, LANES - 2 * GDN_HEADS))).astype(BF16)
        proj = norm_matmul(xf, norm_mix_w[l], w_main, BF16, 512, w_main.shape[1] // 4)
        ba = norm_matmul(xf, norm_mix_w[l], w_ba, F32, 512, LANES)
        og = gdn_heads(proj, ba, gdn_conv_w[l].astype(F32), gdn_A_log[l], gdn_dt_bias[l],
                       gdn_out_norm_w[l].astype(F32), batch, seq)
        om = moba_heads(proj, moba_out_norm_w[l].astype(F32), batch, seq, gw // HEAD_DIM)
        w_o = w_out[l].astype(BF16)
        w_router = jnp.pad(jnp.concatenate([w_router_group[l], w_router_expert[l]], axis=1).astype(F32),
                           ((0, 0), (0, LANES - N_GROUPS - N_EXPERTS)))
        x2, h2, logits = out_proj(xf, og, om, w_o[:GDN_WIDTH], w_o[GDN_WIDTH:], norm_ffn_w[l].astype(F32),
                                  w_router, 256)
        gates, expert_id = _route(logits, b_router_group[l], b_router_expert[l])
        src_tok, blk_expert, pos, n_used = _dispatch_plan(expert_id)
        yb = moe_experts(h2, blk_expert, src_tok, n_used, w_expert_gate[l].astype(BF16),
                         w_expert_up[l].astype(BF16), w_expert_down[l].astype(BF16))
        gates_pad = jnp.pad(gates.astype(F32), ((0, 0), (0, LANES - TOPK_IN_GROUP)))
        xf = moe_combine(yb, pos[:, 0], pos[:, 1], x2, gates_pad, norm_final_w.astype(F32))
    return xf.reshape(batch, seq, d).astype(x.dtype)
```

```python
import functools

import jax
import jax.numpy as jnp
from jax import lax
from jax.experimental import pallas as pl
from jax.experimental.pallas import tpu as pltpu

F32 = jnp.float32
BF16 = jnp.bfloat16
U32 = jnp.uint32
I32 = jnp.int32

HEAD_DIM = 128
GDN_HEADS = 8
MOBA_HEADS = 8
GDN_WIDTH = GDN_HEADS * HEAD_DIM
MOBA_WIDTH = MOBA_HEADS * HEAD_DIM
CONV_WIDTH = 4
GDN_CHUNK = 64
MOBA_BLOCK = 256
MOBA_TOPK = 3
N_GROUPS = 4
EXPERTS_PER_GROUP = 8
N_EXPERTS = N_GROUPS * EXPERTS_PER_GROUP
TOPK_IN_GROUP = 2
RMS_EPS = 1e-6
NEG_INF = -1e30
LANES = 128
VMEM_LIMIT = 56 * 1024 * 1024

GDN_TILE = 256
MOE_ROWS = 128
DISPATCH_ROWS = 256
COMBINE_ROWS = 128
HI = lax.Precision.HIGHEST


def _cparams(sem, **kw):
    return pltpu.CompilerParams(dimension_semantics=sem, vmem_limit_bytes=VMEM_LIMIT, **kw)


def _dot(a, b):
    return jnp.dot(a, b, preferred_element_type=F32)


def _dot_nt(a, b, precision=None):
    return lax.dot_general(a, b, (((1,), (1,)), ((), ())), preferred_element_type=F32,
                           precision=precision)


def _sigmoid(x):
    return 1.0 / (1.0 + jnp.exp(-x))


def _norm_matmul_body(x_ref, nw_ref, w_ref, o_ref):
    x = x_ref[...]
    ms = jnp.mean(x * x, axis=-1, keepdims=True)
    h = (x * lax.rsqrt(ms + RMS_EPS) * nw_ref[...]).astype(BF16)
    o_ref[...] = _dot(h, w_ref[...]).astype(o_ref.dtype)


def norm_matmul(x, nw, w, out_dtype, tm, tn):
    n, d = x.shape
    width = w.shape[1]
    return pl.pallas_call(
        _norm_matmul_body,
        grid=(width // tn, n // tm),
        in_specs=[pl.BlockSpec((tm, d), lambda j, i: (i, 0)),
                  pl.BlockSpec((1, d), lambda j, i: (0, 0)),
                  pl.BlockSpec((d, tn), lambda j, i: (0, j))],
        out_specs=pl.BlockSpec((tm, tn), lambda j, i: (i, j)),
        out_shape=jax.ShapeDtypeStruct((n, width), out_dtype),
        compiler_params=_cparams(("arbitrary", "arbitrary")),
        name="norm_in_proj",
    )(x, nw.reshape(1, d), w)


def _gdn_body(alog_ref, dtb_ref, q_ref, k_ref, v_ref, z_ref, ba_ref, cwq_ref, cwk_ref, cwv_ref,
              onw_ref, o_ref, s_ref, prev_ref, vn_ref, oacc_ref):
    tt = GDN_TILE
    c = GDN_CHUNK
    h = pl.program_id(1)
    t = pl.program_id(2)

    @pl.when(t == 0)
    def _():
        s_ref[...] = jnp.zeros_like(s_ref)
        prev_ref[...] = jnp.zeros_like(prev_ref)
        vn_ref[...] = jnp.zeros_like(vn_ref)

    row_8 = lax.broadcasted_iota(jnp.int32, (8, LANES), 0)

    def conv_silu(x_ref, cw_ref, idx):
        x = x_ref[...].astype(F32)
        p = prev_ref[idx]
        w = cw_ref[...]
        acc = x * w[CONV_WIDTH - 1:CONV_WIDTH, :]
        for s in range(1, CONV_WIDTH):
            xs = pltpu.roll(x, s, axis=0)
            head = jnp.where(row_8 < s, pltpu.roll(p, s, axis=0), xs[:8])
            xs = jnp.concatenate([head, xs[8:]], axis=0)
            acc = acc + xs * w[CONV_WIDTH - 1 - s:CONV_WIDTH - s, :]
        prev_ref[idx] = x[tt - 8:tt, :]
        return acc * _sigmoid(acc)

    q = conv_silu(q_ref, cwq_ref, 0)
    k = conv_silu(k_ref, cwk_ref, 1)
    v = conv_silu(v_ref, cwv_ref, 2)
    q = q * lax.rsqrt(jnp.sum(q * q, axis=-1, keepdims=True) + 1e-6) * (HEAD_DIM ** -0.5)
    k = k * lax.rsqrt(jnp.sum(k * k, axis=-1, keepdims=True) + 1e-6)

    ba = ba_ref[...]
    lane_t = lax.broadcasted_iota(jnp.int32, (tt, LANES), 1)
    b_col = jnp.sum(jnp.where(lane_t == h, ba, 0.0), axis=-1, keepdims=True)
    a_col = jnp.sum(jnp.where(lane_t == h + GDN_HEADS, ba, 0.0), axis=-1, keepdims=True)
    beta = _sigmoid(b_col)
    xa = a_col + dtb_ref[h]
    softplus = jnp.maximum(xa, 0.0) + jnp.log(1.0 + jnp.exp(-jnp.abs(xa)))
    g = -jnp.exp(jnp.full((1, 1), alog_ref[h], F32)) * softplus

    row = lax.broadcasted_iota(jnp.int32, (tt, tt), 0)
    col = lax.broadcasted_iota(jnp.int32, (tt, tt), 1)
    same = (row // c) == (col // c)
    causal = jnp.logical_and(same, col <= row)
    strict = jnp.logical_and(same, col < row)

    g_b = jnp.broadcast_to(g, (tt, LANES))
    gc_b = jnp.dot(jnp.where(causal, 1.0, 0.0), g_b, preferred_element_type=F32, precision=HI)
    glast_b = jnp.dot(jnp.where(same, 1.0, 0.0), g_b, preferred_element_type=F32, precision=HI)
    gc_col = gc_b[:, 0:1]
    gc_row = jnp.transpose(gc_b)[0:1, :]
    decay = jnp.exp(jnp.where(causal, gc_col - gc_row, NEG_INF))

    kb = k * beta
    k16 = k.astype(BF16)
    kk = _dot_nt(kb.astype(BF16), k16)
    lmat = jnp.where(strict, kk * decay, 0.0)
    attn = _dot_nt(q.astype(BF16), k16) * decay

    eye = jnp.where(row == col, 1.0, 0.0)
    tinv = eye - lmat
    m = lmat
    for _ in range(5):
        m16 = m.astype(BF16)
        m = _dot(m16, m16)
        tinv = tinv + _dot(tinv.astype(BF16), m.astype(BF16))

    egc = jnp.exp(gc_b)
    rhs = jnp.concatenate([v * beta, kb * egc], axis=-1)
    sol = _dot(tinv.astype(BF16), rhs.astype(BF16))
    u = sol[:, :HEAD_DIM]
    w16 = sol[:, HEAD_DIM:].astype(BF16)
    qd16 = (q * egc).astype(BF16)
    kd = k * jnp.exp(glast_b - gc_b)
    kdt = jnp.transpose(kd)
    gtot_b = jnp.exp(glast_b)
    attn16 = attn.astype(BF16)
    colk = lax.broadcasted_iota(jnp.int32, (HEAD_DIM, tt), 1)

    s = s_ref[...]
    for n in range(tt // c):
        sl = slice(n * c, (n + 1) * c)
        s16 = s.astype(BF16)
        v_new = u[sl] - _dot(w16[sl], s16)
        vn_ref[sl, :] = v_new
        vn16 = vn_ref[...].astype(BF16)
        oacc_ref[sl, :] = _dot(qd16[sl], s16) + _dot(attn16[sl], vn16)
        kdt_n = jnp.where((colk // c) == n, kdt, 0.0).astype(BF16)
        s = s * gtot_b[n * c:n * c + 1, :] + _dot(kdt_n, vn16)
    s_ref[...] = s

    o = oacc_ref[...]
    o = o * lax.rsqrt(jnp.mean(o * o, axis=-1, keepdims=True) + RMS_EPS) * onw_ref[...]
    z = z_ref[...].astype(F32)
    o_ref[...] = (o * (z * _sigmoid(z))).astype(o_ref.dtype)


def gdn_heads(proj, ba, conv_w, a_log, dt_bias, out_norm_w, batch, seq):
    n = batch * seq
    tt = GDN_TILE
    nt = seq // tt
    hh = GDN_HEADS

    def col_spec(off):
        return pl.BlockSpec((tt, HEAD_DIM), lambda b, h, t, *_: (b * nt + t, off + h))

    def cw_spec(off):
        return pl.BlockSpec((CONV_WIDTH, HEAD_DIM), lambda b, h, t, *_: (0, off + h))

    grid_spec = pltpu.PrefetchScalarGridSpec(
        num_scalar_prefetch=2,
        grid=(batch, hh, nt),
        in_specs=[col_spec(0), col_spec(hh), col_spec(2 * hh), col_spec(3 * hh),
                  pl.BlockSpec((tt, LANES), lambda b, h, t, *_: (b * nt + t, 0)),
                  cw_spec(0), cw_spec(hh), cw_spec(2 * hh),
                  pl.BlockSpec((1, HEAD_DIM), lambda b, h, t, *_: (0, 0))],
        out_specs=pl.BlockSpec((tt, HEAD_DIM), lambda b, h, t, *_: (b * nt + t, h)),
        scratch_shapes=[pltpu.VMEM((HEAD_DIM, HEAD_DIM), F32),
                        pltpu.VMEM((3, 8, HEAD_DIM), F32),
                        pltpu.VMEM((tt, HEAD_DIM), F32),
                        pltpu.VMEM((tt, HEAD_DIM), F32)])
    return pl.pallas_call(
        _gdn_body,
        grid_spec=grid_spec,
        out_shape=jax.ShapeDtypeStruct((n, GDN_WIDTH), BF16),
        compiler_params=_cparams(("arbitrary", "arbitrary", "arbitrary")),
        name="gdn_heads",
    )(a_log.astype(F32), dt_bias.astype(F32), proj, proj, proj, proj, ba,
      conv_w, conv_w, conv_w, out_norm_w.reshape(1, HEAD_DIM))


def _moba_body(slope_ref, q_ref, k_ref, v_ref, onw_ref, o_ref, kmean_ref, *, n_blocks):
    blk = MOBA_BLOCK
    h = pl.program_id(1)
    c = pl.program_id(2)
    scale = HEAD_DIM ** -0.5
    slope = slope_ref[h]

    @pl.when(c == 0)
    def _():
        kmean_ref[...] = jnp.zeros_like(kmean_ref)
        for j in range(n_blocks):
            kj = k_ref[j * blk:(j + 1) * blk, :].astype(F32)
            kmean_ref[j:j + 1, :] = jnp.mean(kj, axis=0, keepdims=True)

    q16 = q_ref[...]
    gate = _dot_nt(q16.astype(F32), kmean_ref[...], precision=HI)
    lane = lax.broadcasted_iota(jnp.int32, (blk, LANES), 1)
    gm = jnp.where(lane < c, gate, NEG_INF)
    sel = jnp.zeros((blk, LANES), F32)
    for s in range(MOBA_TOPK):
        mx = jnp.max(gm, axis=-1, keepdims=True)
        idx = jnp.min(jnp.where(gm == mx, lane, LANES), axis=-1, keepdims=True)
        pick = lane == idx
        sel = jnp.where(jnp.logical_and(pick, s < c), 1.0, sel)
        gm = jnp.where(pick, -3e38, gm)

    row = lax.broadcasted_iota(jnp.int32, (blk, blk), 0)
    col = lax.broadcasted_iota(jnp.int32, (blk, blk), 1)
    rel = (row - col).astype(F32)

    k_own = k_ref[pl.ds(pl.multiple_of(c * blk, blk), blk), :]
    v_own = v_ref[pl.ds(pl.multiple_of(c * blk, blk), blk), :]
    s_own = _dot_nt(q16, k_own) * scale - slope * rel
    s_own = jnp.where(col <= row, s_own, NEG_INF)
    m0 = jnp.max(s_own, axis=-1, keepdims=True)
    p0 = jnp.exp(s_own - m0)
    l0 = jnp.sum(p0, axis=-1, keepdims=True)
    acc0 = _dot(p0.astype(BF16), v_own)

    def body(j, carry):
        m_i, l_i, acc = carry
        off = pl.multiple_of(j * blk, blk)
        kj = k_ref[pl.ds(off, blk), :]
        vj = v_ref[pl.ds(off, blk), :]
        sel_j = jnp.sum(jnp.where(lane == j, sel, 0.0), axis=-1, keepdims=True)
        dist0 = ((c - j) * blk).astype(F32)
        sj = _dot_nt(q16, kj) * scale - slope * (rel + dist0)
        sj = jnp.where(sel_j > 0.5, sj, NEG_INF)
        m_new = jnp.maximum(m_i, jnp.max(sj, axis=-1, keepdims=True))
        alpha = jnp.exp(m_i - m_new)
        p = jnp.exp(sj - m_new)
        l_new = l_i * alpha + jnp.sum(p, axis=-1, keepdims=True)
        acc_new = acc * alpha + _dot(p.astype(BF16), vj)
        return m_new, l_new, acc_new

    _, l_f, acc_f = lax.fori_loop(0, c, body, (m0, l0, acc0))
    o = acc_f / l_f
    o = o * lax.rsqrt(jnp.mean(o * o, axis=-1, keepdims=True) + RMS_EPS) * onw_ref[...]
    o_ref[...] = o.astype(o_ref.dtype)


def moba_heads(proj, out_norm_w, batch, seq, col_off):
    n = batch * seq
    blk = MOBA_BLOCK
    nb = seq // blk
    hh = MOBA_HEADS
    slopes = jnp.exp2(-8.0 * jnp.arange(1, hh + 1, dtype=F32) / hh)
    grid_spec = pltpu.PrefetchScalarGridSpec(
        num_scalar_prefetch=1,
        grid=(batch, hh, nb),
        in_specs=[pl.BlockSpec((blk, HEAD_DIM), lambda b, h, c, *_: (b * nb + c, col_off + h)),
                  pl.BlockSpec((seq, HEAD_DIM), lambda b, h, c, *_: (b, col_off + hh + h)),
                  pl.BlockSpec((seq, HEAD_DIM), lambda b, h, c, *_: (b, col_off + 2 * hh + h)),
                  pl.BlockSpec((1, HEAD_DIM), lambda b, h, c, *_: (0, 0))],
        out_specs=pl.BlockSpec((blk, HEAD_DIM), lambda b, h, c, *_: (b * nb + c, h)),
        scratch_shapes=[pltpu.VMEM((LANES, HEAD_DIM), F32)])
    return pl.pallas_call(
        functools.partial(_moba_body, n_blocks=nb),
        grid_spec=grid_spec,
        out_shape=jax.ShapeDtypeStruct((n, MOBA_WIDTH), BF16),
        compiler_params=_cparams(("arbitrary", "arbitrary", "arbitrary")),
        name="moba_heads",
    )(slopes, proj, proj, proj, out_norm_w.reshape(1, HEAD_DIM))


def _out_proj_body(x_ref, og_ref, om_ref, wg_ref, wm_ref, nw_ref, wrh_ref, wrl_ref, rb_ref,
                   x2_ref, hp_ref, gt_ref, id_ref):
    tm = x_ref.shape[0]
    x2 = x_ref[...] + _dot(og_ref[...], wg_ref[...]) + _dot(om_ref[...], wm_ref[...])
    x2_ref[...] = x2
    ms = jnp.mean(x2 * x2, axis=-1, keepdims=True)
    h2 = x2 * lax.rsqrt(ms + RMS_EPS) * nw_ref[...]
    h_hi = h2.astype(BF16)
    hp_ref[...] = h2
    h_lo = (h2 - h_hi.astype(F32)).astype(BF16)
    lg = _dot(h_hi, wrh_ref[...]) + _dot(h_hi, wrl_ref[...]) + _dot(h_lo, wrh_ref[...]) + rb_ref[...]

    lane = lax.broadcasted_iota(I32, (tm, LANES), 1)
    is_g = lane < N_GROUPS
    mg = jnp.max(jnp.where(is_g, lg, NEG_INF), axis=-1, keepdims=True)
    g_idx = jnp.min(jnp.where(jnp.logical_and(is_g, lg == mg), lane, LANES), axis=-1, keepdims=True)
    sum_g = jnp.sum(jnp.where(is_g, jnp.exp(lg - mg), 0.0), axis=-1, keepdims=True)
    p_top_g = 1.0 / sum_g
    lo = N_GROUPS + g_idx * EXPERTS_PER_GROUP
    in_grp = jnp.logical_and(lane >= lo, lane < lo + EXPERTS_PER_GROUP)
    m1 = jnp.max(jnp.where(in_grp, lg, NEG_INF), axis=-1, keepdims=True)
    i1 = jnp.min(jnp.where(jnp.logical_and(in_grp, lg == m1), lane, LANES), axis=-1, keepdims=True)
    rest = jnp.logical_and(in_grp, lane != i1)
    m2 = jnp.max(jnp.where(rest, lg, NEG_INF), axis=-1, keepdims=True)
    i2 = jnp.min(jnp.where(jnp.logical_and(rest, lg == m2), lane, LANES), axis=-1, keepdims=True)
    e2 = jnp.exp(m2 - m1)
    gate1 = p_top_g / (1.0 + e2)
    gate2 = p_top_g * e2 / (1.0 + e2)
    gt_ref[...] = jnp.where(lane == 0, gate1, jnp.where(lane == 1, gate2, 0.0))
    id_ref[...] = jnp.where(lane == 0, i1 - N_GROUPS, jnp.where(lane == 1, i2 - N_GROUPS, 0))


def out_proj(x, og, om, w_g, w_m, nw, wr_hi, wr_lo, r_bias, tm):
    n, d = x.shape
    const = lambda i: (0, 0)
    rows = lambda i: (i, 0)
    return pl.pallas_call(
        _out_proj_body,
        grid=(n // tm,),
        in_specs=[pl.BlockSpec((tm, d), rows),
                  pl.BlockSpec((tm, GDN_WIDTH), rows),
                  pl.BlockSpec((tm, MOBA_WIDTH), rows),
                  pl.BlockSpec((GDN_WIDTH, d), const),
                  pl.BlockSpec((MOBA_WIDTH, d), const),
                  pl.BlockSpec((1, d), const),
                  pl.BlockSpec((d, LANES), const),
                  pl.BlockSpec((d, LANES), const),
                  pl.BlockSpec((1, LANES), const)],
        out_specs=[pl.BlockSpec((tm, d), rows),
                   pl.BlockSpec((tm, d), rows),
                   pl.BlockSpec((tm, LANES), rows),
                   pl.BlockSpec((tm, LANES), rows)],
        out_shape=[jax.ShapeDtypeStruct((n, d), F32),
                   jax.ShapeDtypeStruct((n, d), F32),
                   jax.ShapeDtypeStruct((n, LANES), F32),
                   jax.ShapeDtypeStruct((n, LANES), I32)],
        compiler_params=_cparams(("arbitrary",)),
        name="out_proj_router",
    )(x, og, om, w_g, w_m, nw.reshape(1, d), wr_hi, wr_lo, r_bias)


def _dispatch_body(pos0_ref, pos1_ref, hp_ref, xb_in_hbm, xb_hbm, sem):
    del xb_in_hbm
    ts = DISPATCH_ROWS
    base = pl.program_id(0) * ts

    def body(r, carry):
        src = hp_ref.at[pl.ds(r, 1)]
        pltpu.make_async_copy(src, xb_hbm.at[pl.ds(pos0_ref[base + r], 1)], sem).start()
        pltpu.make_async_copy(src, xb_hbm.at[pl.ds(pos1_ref[base + r], 1)], sem).start()
        return carry

    lax.fori_loop(0, ts, body, 0, unroll=8)
    for _ in range(TOPK_IN_GROUP):
        pltpu.make_async_copy(hp_ref, xb_hbm.at[pl.ds(0, ts)], sem).wait()


def moe_dispatch(hp, pos0, pos1, n_rows):
    n, dh = hp.shape
    ts = DISPATCH_ROWS
    grid_spec = pltpu.PrefetchScalarGridSpec(
        num_scalar_prefetch=2,
        grid=(n // ts,),
        in_specs=[pl.BlockSpec((ts, dh), lambda i, *_: (i, 0)),
                  pl.BlockSpec(memory_space=pl.ANY)],
        out_specs=pl.BlockSpec(memory_space=pl.ANY),
        scratch_shapes=[pltpu.SemaphoreType.DMA(())])
    return pl.pallas_call(
        _dispatch_body,
        grid_spec=grid_spec,
        out_shape=jax.ShapeDtypeStruct((n_rows, dh), F32),
        input_output_aliases={3: 0},
        compiler_params=_cparams(("arbitrary",), has_side_effects=True),
        name="moe_dispatch",
    )(pos0, pos1, hp, jnp.zeros((n_rows, dh), F32))


def _moe_body(blk_e_ref, nblk_ref, xb_ref, wg_ref, wu_ref, wd_ref, y_ref):
    del blk_e_ref
    i = pl.program_id(0)

    @pl.when(i < nblk_ref[0])
    def _():
        xb = xb_ref[...].astype(BF16)
        g = _dot(xb, wg_ref[0])
        u = _dot(xb, wu_ref[0])
        hm = (g * _sigmoid(g) * u).astype(BF16)
        y_ref[...] = _dot(hm, wd_ref[0])

    @pl.when(i >= nblk_ref[0])
    def _():
        y_ref[...] = jnp.zeros_like(y_ref)


def moe_experts(xb, blk_expert, n_used, w_gate, w_up, w_down):
    rb = MOE_ROWS
    n_rows, dh = xb.shape
    d = dh
    de = w_gate.shape[-1]
    grid_spec = pltpu.PrefetchScalarGridSpec(
        num_scalar_prefetch=2,
        grid=(n_rows // rb,),
        in_specs=[pl.BlockSpec((rb, dh), lambda i, be, nu: (i, 0)),
                  pl.BlockSpec((1, d, de), lambda i, be, nu: (be[i], 0, 0)),
                  pl.BlockSpec((1, d, de), lambda i, be, nu: (be[i], 0, 0)),
                  pl.BlockSpec((1, de, d), lambda i, be, nu: (be[i], 0, 0))],
        out_specs=pl.BlockSpec((rb, dh), lambda i, be, nu: (i, 0)))
    return pl.pallas_call(
        _moe_body,
        grid_spec=grid_spec,
        out_shape=jax.ShapeDtypeStruct((n_rows, dh), F32),
        compiler_params=_cparams(("arbitrary",)),
        name="moe_experts",
    )(blk_expert, n_used, xb, w_gate, w_up, w_down)


def _combine_body(pos0_ref, pos1_ref, y_hbm, x2_ref, gt_ref, nw_ref, o_ref, buf, sem):
    tf = COMBINE_ROWS
    i = pl.program_id(0)
    n_steps = pl.num_programs(0)
    slot = lax.rem(i, 2)

    def issue(step, sl):
        def body(r, carry):
            t = step * tf + r
            pltpu.make_async_copy(y_hbm.at[pl.ds(pos0_ref[t], 1)], buf.at[sl, 0, pl.ds(r, 1)],
                                  sem.at[sl]).start()
            pltpu.make_async_copy(y_hbm.at[pl.ds(pos1_ref[t], 1)], buf.at[sl, 1, pl.ds(r, 1)],
                                  sem.at[sl]).start()
            return carry
        lax.fori_loop(0, tf, body, 0, unroll=8)

    @pl.when(i == 0)
    def _():
        issue(0, 0)

    @pl.when(i + 1 < n_steps)
    def _():
        issue(i + 1, 1 - slot)

    for kk in range(TOPK_IN_GROUP):
        pltpu.make_async_copy(y_hbm.at[pl.ds(0, tf)], buf.at[slot, kk], sem.at[slot]).wait()

    gt = gt_ref[...]
    xo = x2_ref[...] + gt[:, 0:1] * buf[slot, 0] + gt[:, 1:2] * buf[slot, 1]
    ms = jnp.mean(xo * xo, axis=-1, keepdims=True)
    o_ref[...] = xo * lax.rsqrt(ms + RMS_EPS) * nw_ref[...]


def moe_combine(yb, pos0, pos1, x2, gates, nw):
    n, d = x2.shape
    tf = COMBINE_ROWS
    grid_spec = pltpu.PrefetchScalarGridSpec(
        num_scalar_prefetch=2,
        grid=(n // tf,),
        in_specs=[pl.BlockSpec(memory_space=pl.ANY),
                  pl.BlockSpec((tf, d), lambda i, *_: (i, 0)),
                  pl.BlockSpec((tf, LANES), lambda i, *_: (i, 0)),
                  pl.BlockSpec((1, d), lambda i, *_: (0, 0))],
        out_specs=pl.BlockSpec((tf, d), lambda i, *_: (i, 0)),
        scratch_shapes=[pltpu.VMEM((2, 2, tf, d), F32), pltpu.SemaphoreType.DMA((2,))])
    return pl.pallas_call(
        _combine_body,
        grid_spec=grid_spec,
        out_shape=jax.ShapeDtypeStruct((n, d), F32),
        compiler_params=_cparams(("arbitrary",)),
        name="moe_combine",
    )(pos0, pos1, yb, x2, gates, nw.reshape(1, d))


def _dispatch_plan(expert_id):
    n_tok, k = expert_id.shape
    rb = MOE_ROWS
    n_assign = n_tok * k
    e_flat = expert_id.reshape(-1)
    onehot = (e_flat[:, None] == jnp.arange(N_EXPERTS, dtype=I32)[None, :]).astype(I32)
    csum = jnp.cumsum(onehot, axis=0)
    counts = csum[-1]
    padded = (counts + rb - 1) // rb * rb
    pad_end = jnp.cumsum(padded)
    pad_start = pad_end - padded
    pos = jnp.sum(onehot * (pad_start[None, :] + csum - 1), axis=1).astype(I32).reshape(n_tok, k)
    n_rb = -(-n_assign // rb) + N_EXPERTS
    blk_start = jnp.arange(n_rb, dtype=I32) * rb
    blk_expert = jnp.minimum(jnp.sum((pad_end[None, :] <= blk_start[:, None]).astype(I32), axis=1),
                             N_EXPERTS - 1).astype(I32)
    n_used = (pad_end[-1] // rb).astype(I32).reshape(1)
    return pos, blk_expert, n_used, n_rb * rb


def kernel(x, norm_mix_w, w_in, gdn_conv_w, gdn_A_log, gdn_dt_bias, gdn_out_norm_w, moba_out_norm_w, w_out, norm_ffn_w, w_router_group, b_router_group, w_router_expert, b_router_expert, w_expert_gate, w_expert_up, w_expert_down, norm_final_w):
    batch, seq, d = x.shape
    n = batch * seq
    assert w_in.shape[0] == 1, "the final norm is fused into the last layer's combine; one layer supported"
    l = 0
    xf = x.reshape(n, d).astype(F32)
    gw = 4 * GDN_WIDTH
    w_l = w_in[l]
    w_main = jnp.concatenate([w_l[:, :gw], w_l[:, gw + 2 * GDN_HEADS:]], axis=1).astype(BF16)
    w_ba = jnp.pad(w_l[:, gw:gw + 2 * GDN_HEADS], ((0, 0), (0, LANES - 2 * GDN_HEADS))).astype(BF16)
    proj = norm_matmul(xf, norm_mix_w[l], w_main, BF16, 512, w_main.shape[1] // 4)
    ba = norm_matmul(xf, norm_mix_w[l], w_ba, F32, 512, LANES)
    og = gdn_heads(proj, ba, gdn_conv_w[l].astype(F32), gdn_A_log[l], gdn_dt_bias[l],
                   gdn_out_norm_w[l].astype(F32), batch, seq)
    om = moba_heads(proj, moba_out_norm_w[l].astype(F32), batch, seq, gw // HEAD_DIM)

    w_o = w_out[l].astype(BF16)
    n_r = N_GROUPS + N_EXPERTS
    w_router = jnp.pad(jnp.concatenate([w_router_group[l], w_router_expert[l]], axis=1).astype(F32),
                       ((0, 0), (0, LANES - n_r)))
    wr_hi = w_router.astype(BF16)
    wr_lo = (w_router - wr_hi.astype(F32)).astype(BF16)
    r_bias = jnp.pad(jnp.concatenate([b_router_group[l], b_router_expert[l]]).astype(F32),
                     (0, LANES - n_r)).reshape(1, LANES)
    x2, hp, gates, ids = out_proj(xf, og, om, w_o[:GDN_WIDTH], w_o[GDN_WIDTH:], norm_ffn_w[l].astype(F32),
                                  wr_hi, wr_lo, r_bias, 256)

    pos, blk_expert, n_used, n_rows = _dispatch_plan(ids[:, :TOPK_IN_GROUP])
    pos0, pos1 = pos[:, 0], pos[:, 1]
    xb = moe_dispatch(hp, pos0, pos1, n_rows)
    yb = moe_experts(xb, blk_expert, n_used, w_expert_gate[l].astype(BF16),
                     w_expert_up[l].astype(BF16), w_expert_down[l].astype(BF16))
    out = moe_combine(yb, pos0, pos1, x2, gates, norm_final_w.astype(F32))
    return out.reshape(batch, seq, d).astype(x.dtype)
```

```python
import functools
import itertools

import jax
import jax.numpy as jnp
from jax import lax
from jax.experimental import pallas as pl
from jax.experimental.pallas import tpu as pltpu

F32 = jnp.float32
BF16 = jnp.bfloat16
U32 = jnp.uint32
I32 = jnp.int32

HEAD_DIM = 128
GDN_HEADS = 8
MOBA_HEADS = 8
GDN_WIDTH = GDN_HEADS * HEAD_DIM
MOBA_WIDTH = MOBA_HEADS * HEAD_DIM
CONV_WIDTH = 4
GDN_CHUNK = 64
MOBA_BLOCK = 256
MOBA_TOPK = 3
N_GROUPS = 4
EXPERTS_PER_GROUP = 8
N_EXPERTS = N_GROUPS * EXPERTS_PER_GROUP
TOPK_IN_GROUP = 2
RMS_EPS = 1e-6
NEG_INF = -1e30
MOBA_MASK = -(2.0 ** 99)
LOG2E = 1.4426950408889634
LANES = 128
VMEM_LIMIT = 56 * 1024 * 1024

GDN_TILE = 256
GDN_HEADS_PER_STEP = 8
MOE_ROWS = 128
DISPATCH_ROWS = 256
COMBINE_ROWS = 128
HI = lax.Precision.HIGHEST


def _cparams(sem, **kw):
    return pltpu.CompilerParams(dimension_semantics=sem, vmem_limit_bytes=VMEM_LIMIT, **kw)


def _dot(a, b):
    return jnp.dot(a, b, preferred_element_type=F32)


def _dot_nt(a, b, precision=None):
    return lax.dot_general(a, b, (((1,), (1,)), ((), ())), preferred_element_type=F32,
                           precision=precision)


def _sigmoid(x):
    return 1.0 / (1.0 + jnp.exp(-x))


def _norm_matmul_body(x_ref, nw_ref, w_ref, o_ref):
    x = x_ref[...]
    ms = jnp.mean(x * x, axis=-1, keepdims=True)
    h = (x * lax.rsqrt(ms + RMS_EPS) * nw_ref[...]).astype(BF16)
    o_ref[...] = _dot(h, w_ref[...]).astype(o_ref.dtype)


def norm_matmul(x, nw, w, out_dtype, tm, tn):
    n, d = x.shape
    width = w.shape[1]
    return pl.pallas_call(
        _norm_matmul_body,
        grid=(width // tn, n // tm),
        in_specs=[pl.BlockSpec((tm, d), lambda j, i: (i, 0)),
                  pl.BlockSpec((1, d), lambda j, i: (0, 0)),
                  pl.BlockSpec((d, tn), lambda j, i: (0, j))],
        out_specs=pl.BlockSpec((tm, tn), lambda j, i: (i, j)),
        out_shape=jax.ShapeDtypeStruct((n, width), out_dtype),
        compiler_params=_cparams(("arbitrary", "arbitrary")),
        name="norm_in_proj",
    )(x, nw.reshape(1, d), w)


def _gdn_body(alog_ref, dtb_ref, q_ref, k_ref, v_ref, z_ref, ba_ref, cwq_ref, cwk_ref, cwv_ref,
              onw_ref, o_ref, *scratch):
    tt = GDN_TILE
    c = GDN_CHUNK
    t = pl.program_id(2)

    @pl.when(t == 0)
    def _():
        for hh in range(GDN_HEADS_PER_STEP):
            s_ref, prev_ref, vn_ref, _ = scratch[4 * hh:4 * hh + 4]
            s_ref[...] = jnp.zeros_like(s_ref)
            prev_ref[...] = jnp.zeros_like(prev_ref)
            vn_ref[...] = jnp.zeros_like(vn_ref)

    row_8 = lax.broadcasted_iota(jnp.int32, (8, LANES), 0)
    lane_t = lax.broadcasted_iota(jnp.int32, (tt, LANES), 1)
    row = lax.broadcasted_iota(jnp.int32, (tt, tt), 0)
    col = lax.broadcasted_iota(jnp.int32, (tt, tt), 1)
    same = (row // c) == (col // c)
    causal = jnp.logical_and(same, col <= row)
    strict = jnp.logical_and(same, col < row)
    causal16 = jnp.where(causal, 1.0, 0.0).astype(BF16)
    same16 = jnp.where(same, 1.0, 0.0).astype(BF16)
    eye = jnp.where(row == col, 1.0, 0.0)
    colk = lax.broadcasted_iota(jnp.int32, (HEAD_DIM, tt), 1)
    ba = ba_ref[...]

    heads = [_gdn_one_head(hh, pl.program_id(1) * GDN_HEADS_PER_STEP + hh, alog_ref, dtb_ref, q_ref, k_ref,
                           v_ref, z_ref, ba, cwq_ref, cwk_ref, cwv_ref, onw_ref, o_ref,
                           *scratch[4 * hh:4 * hh + 4],
                           row_8, lane_t, causal, strict, causal16, same16, eye, colk)
             for hh in range(GDN_HEADS_PER_STEP)]
    for _ in itertools.zip_longest(*heads):
        pass


def _gdn_one_head(hh, h, alog_ref, dtb_ref, q_ref, k_ref, v_ref, z_ref, ba, cwq_ref, cwk_ref, cwv_ref,
                  onw_ref, o_ref, s_ref, prev_ref, vn_ref, oacc_ref,
                  row_8, lane_t, causal, strict, causal16, same16, eye, colk):
    tt = GDN_TILE
    c = GDN_CHUNK
    lanes = slice(hh * HEAD_DIM, (hh + 1) * HEAD_DIM)

    def conv_silu(x_ref, cw_ref, idx):
        x = x_ref[:, lanes].astype(F32)
        p = prev_ref[idx]
        w = cw_ref[:, lanes]
        acc = x * w[CONV_WIDTH - 1:CONV_WIDTH, :]
        for s in range(1, CONV_WIDTH):
            xs = pltpu.roll(x, s, axis=0)
            head = jnp.where(row_8 < s, pltpu.roll(p, s, axis=0), xs[:8])
            xs = jnp.concatenate([head, xs[8:]], axis=0)
            acc = acc + xs * w[CONV_WIDTH - 1 - s:CONV_WIDTH - s, :]
        prev_ref[idx] = x[tt - 8:tt, :]
        return acc * _sigmoid(acc)

    q = conv_silu(q_ref, cwq_ref, 0)
    k = conv_silu(k_ref, cwk_ref, 1)
    v = conv_silu(v_ref, cwv_ref, 2)
    q = q * lax.rsqrt(jnp.sum(q * q, axis=-1, keepdims=True) + 1e-6) * (HEAD_DIM ** -0.5)
    k = k * lax.rsqrt(jnp.sum(k * k, axis=-1, keepdims=True) + 1e-6)
    yield

    b_col = jnp.sum(jnp.where(lane_t == h, ba, 0.0), axis=-1, keepdims=True)
    a_col = jnp.sum(jnp.where(lane_t == h + GDN_HEADS, ba, 0.0), axis=-1, keepdims=True)
    beta = _sigmoid(b_col)
    xa = a_col + dtb_ref[h]
    softplus = jnp.maximum(xa, 0.0) + jnp.log(1.0 + jnp.exp(-jnp.abs(xa)))
    g = -jnp.exp(jnp.full((1, 1), alog_ref[h], F32)) * softplus

    g_hi = g.astype(BF16).astype(F32)
    g_mid = (g - g_hi).astype(BF16).astype(F32)
    g_lo = g - g_hi - g_mid
    g3 = jnp.where(lane_t == 0, g_hi, jnp.where(lane_t == 1, g_mid, jnp.where(lane_t == 2, g_lo, 0.0)))
    g3 = g3.astype(BF16)

    def three(r):
        return r[:, 0:1] + r[:, 1:2] + r[:, 2:3]

    gc_col = three(_dot(causal16, g3))
    gc_b = jnp.broadcast_to(gc_col, (tt, LANES))
    glast_b = jnp.broadcast_to(three(_dot(same16, g3)), (tt, LANES))
    gc_row = jnp.transpose(gc_b)[0:1, :]
    decay = jnp.exp(jnp.where(causal, gc_col - gc_row, NEG_INF))
    yield

    kb = k * beta
    k16 = k.astype(BF16)
    kk = _dot_nt(kb.astype(BF16), k16)
    lmat = jnp.where(strict, kk * decay, 0.0)
    yield
    attn = _dot_nt(q.astype(BF16), k16) * decay

    tinv = eye - lmat
    m = lmat
    for _ in range(5):
        m16 = m.astype(BF16)
        m = _dot(m16, m16)
        yield
        tinv = tinv + _dot(tinv.astype(BF16), m.astype(BF16))
        yield

    egc = jnp.exp(gc_b)
    rhs = jnp.concatenate([v * beta, kb * egc], axis=-1)
    sol = _dot(tinv.astype(BF16), rhs.astype(BF16))
    yield
    u = sol[:, :HEAD_DIM]
    w16 = sol[:, HEAD_DIM:].astype(BF16)
    qd16 = (q * egc).astype(BF16)
    kd = k * jnp.exp(glast_b - gc_b)
    kdt = jnp.transpose(kd)
    gtot_b = jnp.exp(glast_b)
    attn16 = attn.astype(BF16)

    s = s_ref[...]
    for n in range(tt // c):
        sl = slice(n * c, (n + 1) * c)
        s16 = s.astype(BF16)
        v_new = u[sl] - _dot(w16[sl], s16)
        vn_ref[sl, :] = v_new
        yield
        vn16 = vn_ref[...].astype(BF16)
        oacc_ref[sl, :] = _dot(qd16[sl], s16) + _dot(attn16[sl], vn16)
        kdt_n = jnp.where((colk // c) == n, kdt, 0.0).astype(BF16)
        s = s * gtot_b[n * c:n * c + 1, :] + _dot(kdt_n, vn16)
        yield
    s_ref[...] = s

    o = oacc_ref[...]
    o = o * lax.rsqrt(jnp.mean(o * o, axis=-1, keepdims=True) + RMS_EPS) * onw_ref[...]
    z = z_ref[:, lanes].astype(F32)
    o_ref[:, lanes] = (o * (z * _sigmoid(z))).astype(o_ref.dtype)


def gdn_heads(proj, ba, conv_w, a_log, dt_bias, out_norm_w, batch, seq):
    n = batch * seq
    tt = GDN_TILE
    nt = seq // tt
    hps = GDN_HEADS_PER_STEP
    ng = GDN_HEADS // hps
    wide = hps * HEAD_DIM

    def col_spec(section):
        return pl.BlockSpec((tt, wide), lambda b, h, t, *_: (b * nt + t, section * ng + h))

    def cw_spec(section):
        return pl.BlockSpec((CONV_WIDTH, wide), lambda b, h, t, *_: (0, section * ng + h))

    grid_spec = pltpu.PrefetchScalarGridSpec(
        num_scalar_prefetch=2,
        grid=(batch, ng, nt),
        in_specs=[col_spec(0), col_spec(1), col_spec(2), col_spec(3),
                  pl.BlockSpec((tt, LANES), lambda b, h, t, *_: (b * nt + t, 0)),
                  cw_spec(0), cw_spec(1), cw_spec(2),
                  pl.BlockSpec((1, HEAD_DIM), lambda b, h, t, *_: (0, 0))],
        out_specs=pl.BlockSpec((tt, wide), lambda b, h, t, *_: (b * nt + t, h)),
        scratch_shapes=[pltpu.VMEM((HEAD_DIM, HEAD_DIM), F32),
                        pltpu.VMEM((3, 8, HEAD_DIM), F32),
                        pltpu.VMEM((tt, HEAD_DIM), F32),
                        pltpu.VMEM((tt, HEAD_DIM), F32)] * hps)
    return pl.pallas_call(
        _gdn_body,
        grid_spec=grid_spec,
        out_shape=jax.ShapeDtypeStruct((n, GDN_WIDTH), BF16),
        compiler_params=_cparams(("arbitrary", "arbitrary", "arbitrary")),
        name="gdn_heads",
    )(a_log.astype(F32), dt_bias.astype(F32), proj, proj, proj, proj, ba,
      conv_w, conv_w, conv_w, out_norm_w.reshape(1, HEAD_DIM))


def _moba_body(slope_ref, q_ref, k_ref, v_ref, onw_ref, o_ref, kmean_ref, kx_ref, vt_ref, *, n_blocks):
    blk = MOBA_BLOCK
    tq = 2 * blk
    nb = n_blocks
    nbp = -(-nb // 8) * 8
    h = pl.program_id(1)
    cq = pl.program_id(2)
    c0 = 2 * cq
    slope2 = slope_ref[h] * LOG2E
    lane = lax.broadcasted_iota(I32, (blk, LANES), 1)
    is_bias_lane = jnp.logical_and(lane >= nb, lane < nb + 3)

    @pl.when(cq == 0)
    def _():
        kmean_ref[...] = jnp.zeros_like(kmean_ref)
        t_in = lax.broadcasted_iota(I32, (blk, LANES), 0)
        for j in range(nb):
            kj = k_ref[j * blk:(j + 1) * blk, :]
            kmean_ref[j:j + 1, :] = jnp.mean(kj.astype(F32), axis=0, keepdims=True)
            bias = slope2 * (t_in + j * blk).astype(F32)
            b_hi = bias.astype(BF16)
            r1 = bias - b_hi.astype(F32)
            b_mid = r1.astype(BF16)
            b_lo = (r1 - b_mid.astype(F32)).astype(BF16)
            ext = jnp.where(lane == j, 1.0, 0.0).astype(BF16)
            ext = jnp.where(lane == nb, b_hi, ext)
            ext = jnp.where(lane == nb + 1, b_mid, ext)
            ext = jnp.where(lane == nb + 2, b_lo, ext)
            kx_ref[j * blk:(j + 1) * blk, :] = jnp.concatenate([kj, ext], axis=1)
            vj = v_ref[j * blk:(j + 1) * blk, :].astype(F32)
            vt_ref[:, j * blk:(j + 1) * blk] = jnp.transpose(vj).astype(BF16)

    q_t = jnp.transpose(q_ref[...].astype(F32))
    q_t16 = q_t.astype(BF16)
    row = lax.broadcasted_iota(I32, (blk, blk), 0)
    col = lax.broadcasted_iota(I32, (blk, blk), 1)
    rid = lax.broadcasted_iota(I32, (LANES, tq), 0)

    def keys(j, nblk=1):
        return kx_ref[pl.ds(pl.multiple_of(j * blk, blk), nblk * blk), :]

    def values_t(j, nblk=1):
        return vt_ref[:, pl.ds(pl.multiple_of(j * blk, blk), nblk * blk)]

    def start(s, vals_t):
        m = jnp.max(s, axis=0, keepdims=True)
        p = jnp.exp2(s - m)
        return m, jnp.sum(p, axis=0, keepdims=True), _dot(vals_t, p.astype(BF16))

    def update(carry, s, vals_t):
        m_i, l_i, acc = carry
        m_new = jnp.maximum(m_i, jnp.max(s, axis=0, keepdims=True))
        alpha = jnp.exp2(m_i - m_new)
        p = jnp.exp2(s - m_new)
        return (m_new, l_i * alpha + jnp.sum(p, axis=0, keepdims=True),
                acc * alpha + _dot(vals_t, p.astype(BF16)))

    ones_ext_t = jnp.where(jnp.logical_and(rid >= nb, rid < nb + 3), 1.0, 0.0).astype(BF16)
    qx0_t = jnp.concatenate([q_t16, ones_ext_t], axis=0)
    state = []
    for hh in range(2):
        s_own = _dot(keys(c0 + hh), qx0_t[:, hh * blk:(hh + 1) * blk])
        s_own = jnp.where(row <= col, s_own, MOBA_MASK)
        state.append(start(s_own, values_t(c0 + hh)))

    gate_t = jnp.dot(kmean_ref[...], q_t, preferred_element_type=F32, precision=HI)[:nbp]
    bid = lax.broadcasted_iota(I32, (nbp, tq), 0)
    c_row = c0 + (lax.broadcasted_iota(I32, (nbp, tq), 1) >= blk).astype(I32)
    gm = jnp.where(bid < c_row, gate_t, NEG_INF)
    sel = jnp.zeros((nbp, tq), F32)
    for s in range(MOBA_TOPK):
        mx = jnp.max(gm, axis=0, keepdims=True)
        idx = jnp.min(jnp.where(gm == mx, bid, LANES), axis=0, keepdims=True)
        pick = bid == idx
        sel = jnp.where(jnp.logical_and(pick, s < c_row), 1.0, sel)
        gm = jnp.where(pick, -3e38, gm)
    mask_t = jnp.where(sel > 0.5, 0.0, MOBA_MASK)
    ext_t = jnp.concatenate([mask_t, jnp.zeros((LANES - nbp, tq), F32)], axis=0)
    ext_t = jnp.where(rid < nb, ext_t, jnp.where(rid < nb + 3, 1.0, 0.0))
    qx_t = jnp.concatenate([q_t16, ext_t.astype(BF16)], axis=0)

    first = state[0]
    second = update(state[1], _dot(keys(c0), qx_t[:, blk:]), values_t(c0))
    carry0 = tuple(jnp.concatenate([a, b], axis=1) for a, b in zip(first, second))

    def body(p, carry):
        return update(carry, _dot(keys(2 * p, 2), qx_t), values_t(2 * p, 2))

    _, l_f, acc_f = lax.fori_loop(0, cq, body, carry0)
    o_t = acc_f / l_f
    o_t = o_t * lax.rsqrt(jnp.mean(o_t * o_t, axis=0, keepdims=True) + RMS_EPS)
    o_ref[...] = (jnp.transpose(o_t) * onw_ref[...]).astype(o_ref.dtype)


def moba_heads(proj, out_norm_w, batch, seq, col_off):
    n = batch * seq
    blk = MOBA_BLOCK
    nb = seq // blk
    nq = nb // 2
    hh = MOBA_HEADS
    slopes =jnp.exp2(-8.0 * jnp.arange(1, hh + 1, dtype=F32) / hh)
    grid_spec = pltpu.PrefetchScalarGridSpec(
        num_scalar_prefetch=1,
        grid=(batch, hh, nq),
        in_specs=[pl.BlockSpec((2 * blk, HEAD_DIM), lambda b, h, c, *_: (b * nq + c, col_off + h)),
                  pl.BlockSpec((seq, HEAD_DIM), lambda b, h, c, *_: (b, col_off + hh + h)),
                  pl.BlockSpec((seq, HEAD_DIM), lambda b, h, c, *_: (b, col_off + 2 * hh + h)),
                  pl.BlockSpec((1, HEAD_DIM), lambda b, h, c, *_: (0, 0))],
        out_specs=pl.BlockSpec((2 * blk, HEAD_DIM), lambda b, h, c, *_: (b * nq + c, h)),
        scratch_shapes=[pltpu.VMEM((LANES, HEAD_DIM), F32),
                        pltpu.VMEM((nb * blk, HEAD_DIM + LANES), BF16),
                        pltpu.VMEM((HEAD_DIM, nb * blk), BF16)])
    assert nb % 2 == 0 and nb + 3 <= LANES
    return pl.pallas_call(
        functools.partial(_moba_body, n_blocks=nb),
        grid_spec=grid_spec,
        out_shape=jax.ShapeDtypeStruct((n, MOBA_WIDTH), BF16),
        compiler_params=_cparams(("arbitrary", "arbitrary", "arbitrary")),
        name="moba_heads",
    )(slopes, proj, proj, proj, out_norm_w.reshape(1, HEAD_DIM))


def _out_proj_body(x_ref, og_ref, om_ref, wg_ref, wm_ref, nw_ref, wrh_ref, wrl_ref, rb_ref,
                   x2_ref, hp_ref, gt_ref, id_ref):
    tm = x_ref.shape[0]
    x2 = x_ref[...] + _dot(og_ref[...], wg_ref[...]) + _dot(om_ref[...], wm_ref[...])
    x2_ref[...] = x2
    ms = jnp.mean(x2 * x2, axis=-1, keepdims=True)
    h2 = x2 * lax.rsqrt(ms + RMS_EPS) * nw_ref[...]
    h_hi = h2.astype(BF16)
    hp_ref[...] = h2
    h_lo = (h2 - h_hi.astype(F32)).astype(BF16)
    lg = _dot(h_hi, wrh_ref[...]) + _dot(h_hi, wrl_ref[...]) + _dot(h_lo, wrh_ref[...]) + rb_ref[...]

    lane = lax.broadcasted_iota(I32, (tm, LANES), 1)
    is_g = lane < N_GROUPS
    mg = jnp.max(jnp.where(is_g, lg, NEG_INF), axis=-1, keepdims=True)
    g_idx = jnp.min(jnp.where(jnp.logical_and(is_g, lg == mg), lane, LANES), axis=-1, keepdims=True)
    sum_g = jnp.sum(jnp.where(is_g, jnp.exp(lg - mg), 0.0), axis=-1, keepdims=True)
    p_top_g = 1.0 / sum_g
    lo = N_GROUPS + g_idx * EXPERTS_PER_GROUP
    in_grp = jnp.logical_and(lane >= lo, lane < lo + EXPERTS_PER_GROUP)
    m1 = jnp.max(jnp.where(in_grp, lg, NEG_INF), axis=-1, keepdims=True)
    i1 = jnp.min(jnp.where(jnp.logical_and(in_grp, lg == m1), lane, LANES), axis=-1, keepdims=True)
    rest = jnp.logical_and(in_grp, lane != i1)
    m2 = jnp.max(jnp.where(rest, lg, NEG_INF), axis=-1, keepdims=True)
    i2 = jnp.min(jnp.where(jnp.logical_and(rest, lg == m2), lane, LANES), axis=-1, keepdims=True)
    e2 = jnp.exp(m2 - m1)
    gate1 = p_top_g / (1.0 + e2)
    gate2 = p_top_g * e2 / (1.0 + e2)
    gt_ref[...] = jnp.where(lane == 0, gate1, jnp.where(lane == 1, gate2, 0.0))
    id_ref[...] = jnp.where(lane == 0, i1 - N_GROUPS, jnp.where(lane == 1, i2 - N_GROUPS, 0))


def out_proj(x, og, om, w_g, w_m, nw, wr_hi, wr_lo, r_bias, tm):
    n, d = x.shape
    const = lambda i: (0, 0)
    rows = lambda i: (i, 0)
    return pl.pallas_call(
        _out_proj_body,
        grid=(n // tm,),
        in_specs=[pl.BlockSpec((tm, d), rows),
                  pl.BlockSpec((tm, GDN_WIDTH), rows),
                  pl.BlockSpec((tm, MOBA_WIDTH), rows),
                  pl.BlockSpec((GDN_WIDTH, d), const),
                  pl.BlockSpec((MOBA_WIDTH, d), const),
                  pl.BlockSpec((1, d), const),
                  pl.BlockSpec((d, LANES), const),
                  pl.BlockSpec((d, LANES), const),
                  pl.BlockSpec((1, LANES), const)],
        out_specs=[pl.BlockSpec((tm, d), rows),
                   pl.BlockSpec((tm, d), rows),
                   pl.BlockSpec((tm, LANES), rows),
                   pl.BlockSpec((tm, LANES), rows)],
        out_shape=[jax.ShapeDtypeStruct((n, d), F32),
                   jax.ShapeDtypeStruct((n, d), F32),
                   jax.ShapeDtypeStruct((n, LANES), F32),
                   jax.ShapeDtypeStruct((n, LANES), I32)],
        compiler_params=_cparams(("arbitrary",)),
        name="out_proj_router",
    )(x, og, om, w_g, w_m, nw.reshape(1, d), wr_hi, wr_lo, r_bias)


def _dispatch_body(pos0_ref, pos1_ref, hp_ref, xb_in_hbm, xb_hbm, sem):
    del xb_in_hbm
    ts = DISPATCH_ROWS
    base = pl.program_id(0) * ts

    def body(r, carry):
        src = hp_ref.at[pl.ds(r, 1)]
        pltpu.make_async_copy(src, xb_hbm.at[pl.ds(pos0_ref[base + r], 1)], sem).start()
        pltpu.make_async_copy(src, xb_hbm.at[pl.ds(pos1_ref[base + r], 1)], sem).start()
        return carry

    lax.fori_loop(0, ts, body, 0, unroll=8)
    for _ in range(TOPK_IN_GROUP):
        pltpu.make_async_copy(hp_ref, xb_hbm.at[pl.ds(0, ts)], sem).wait()


def moe_dispatch(hp, pos0, pos1, n_rows):
    n, dh = hp.shape
    ts = DISPATCH_ROWS
    grid_spec = pltpu.PrefetchScalarGridSpec(
        num_scalar_prefetch=2,
        grid=(n // ts,),
        in_specs=[pl.BlockSpec((ts, dh), lambda i, *_: (i, 0)),
                  pl.BlockSpec(memory_space=pl.ANY)],
        out_specs=pl.BlockSpec(memory_space=pl.ANY),
        scratch_shapes=[pltpu.SemaphoreType.DMA(())])
    return pl.pallas_call(
        _dispatch_body,
        grid_spec=grid_spec,
        out_shape=jax.ShapeDtypeStruct((n_rows, dh), F32),
        input_output_aliases={3: 0},
        compiler_params=_cparams(("arbitrary",), has_side_effects=True),
        name="moe_dispatch",
    )(pos0, pos1, hp, jnp.zeros((n_rows, dh), F32))


def _moe_body(blk_e_ref, nblk_ref, xb_ref, wg_ref, wu_ref, wd_ref, y_ref):
    del blk_e_ref
    i = pl.program_id(0)

    @pl.when(i < nblk_ref[0])
    def _():
        xb = xb_ref[...].astype(BF16)
        g = _dot(xb, wg_ref[0])
        u = _dot(xb, wu_ref[0])
        hm = (g * _sigmoid(g) * u).astype(BF16)
        y_ref[...] = _dot(hm, wd_ref[0])

    @pl.when(i >= nblk_ref[0])
    def _():
        y_ref[...] = jnp.zeros_like(y_ref)


def moe_experts(xb, blk_expert, n_used, w_gate, w_up, w_down):
    rb = MOE_ROWS
    n_rows, dh = xb.shape
    d = dh
    de = w_gate.shape[-1]
    grid_spec = pltpu.PrefetchScalarGridSpec(
        num_scalar_prefetch=2,
        grid=(n_rows // rb,),
        in_specs=[pl.BlockSpec((rb, dh), lambda i, be, nu: (i, 0)),
                  pl.BlockSpec((1, d, de), lambda i, be, nu: (be[i], 0, 0)),
                  pl.BlockSpec((1, d, de), lambda i, be, nu: (be[i], 0, 0)),
                  pl.BlockSpec((1, de, d), lambda i, be, nu: (be[i], 0, 0))],
        out_specs=pl.BlockSpec((rb, dh), lambda i, be, nu: (i, 0)))
    return pl.pallas_call(
        _moe_body,
        grid_spec=grid_spec,
        out_shape=jax.ShapeDtypeStruct((n_rows, dh), F32),
        compiler_params=_cparams(("arbitrary",)),
        name="moe_experts",
    )(blk_expert, n_used, xb, w_gate, w_up, w_down)


def _combine_body(pos0_ref, pos1_ref, y_hbm, x2_ref, gt_ref, nw_ref, o_ref, buf, sem):
    tf = COMBINE_ROWS
    i = pl.program_id(0)
    n_steps = pl.num_programs(0)
    slot = lax.rem(i, 2)

    def issue(step, sl):
        def body(r, carry):
            t = step * tf + r
            pltpu.make_async_copy(y_hbm.at[pl.ds(pos0_ref[t], 1)], buf.at[sl, 0, pl.ds(r, 1)],
                                  sem.at[sl]).start()
            pltpu.make_async_copy(y_hbm.at[pl.ds(pos1_ref[t], 1)], buf.at[sl, 1, pl.ds(r, 1)],
                                  sem.at[sl]).start()
            return carry
        lax.fori_loop(0, tf, body, 0, unroll=8)

    @pl.when(i == 0)
    def _():
        issue(0, 0)

    @pl.when(i + 1 < n_steps)
    def _():
        issue(i + 1, 1 - slot)

    for kk in range(TOPK_IN_GROUP):
        pltpu.make_async_copy(y_hbm.at[pl.ds(0, tf)], buf.at[slot, kk], sem.at[slot]).wait()

    gt = gt_ref[...]
    xo = x2_ref[...] + gt[:, 0:1] * buf[slot, 0] + gt[:, 1:2] * buf[slot, 1]
    ms = jnp.mean(xo * xo, axis=-1, keepdims=True)
    o_ref[...] = xo * lax.rsqrt(ms + RMS_EPS) * nw_ref[...]


def moe_combine(yb, pos0, pos1, x2, gates, nw):
    n, d = x2.shape
    tf = COMBINE_ROWS
    grid_spec = pltpu.PrefetchScalarGridSpec(
        num_scalar_prefetch=2,
        grid=(n // tf,),
        in_specs=[pl.BlockSpec(memory_space=pl.ANY),
                  pl.BlockSpec((tf, d), lambda i, *_: (i, 0)),
                  pl.BlockSpec((tf, LANES), lambda i, *_: (i, 0)),
                  pl.BlockSpec((1, d), lambda i, *_: (0, 0))],
        out_specs=pl.BlockSpec((tf, d), lambda i, *_: (i, 0)),
        scratch_shapes=[pltpu.VMEM((2, 2, tf, d), F32), pltpu.SemaphoreType.DMA((2,))])
    return pl.pallas_call(
        _combine_body,
        grid_spec=grid_spec,
        out_shape=jax.ShapeDtypeStruct((n, d), F32),
        compiler_params=_cparams(("arbitrary",)),
        name="moe_combine",
    )(pos0, pos1, yb, x2, gates, nw.reshape(1, d))


def _dispatch_plan(expert_id):
    n_tok, k = expert_id.shape
    rb = MOE_ROWS
    n_assign = n_tok * k
    e_flat = expert_id.reshape(-1)
    onehot = (e_flat[:, None] == jnp.arange(N_EXPERTS, dtype=I32)[None, :]).astype(I32)
    csum = jnp.cumsum(onehot, axis=0)
    counts = csum[-1]
    padded = (counts + rb - 1) // rb * rb
    pad_end = jnp.cumsum(padded)
    pad_start = pad_end - padded
    pos = jnp.sum(onehot * (pad_start[None, :] + csum - 1), axis=1).astype(I32).reshape(n_tok, k)
    n_rb = -(-n_assign // rb) + N_EXPERTS
    blk_start = jnp.arange(n_rb, dtype=I32) * rb
    blk_expert = jnp.minimum(jnp.sum((pad_end[None, :] <= blk_start[:, None]).astype(I32), axis=1),
                             N_EXPERTS - 1).astype(I32)
    n_used = (pad_end[-1] // rb).astype(I32).reshape(1)
    return pos, blk_expert, n_used, n_rb * rb


def kernel(x, norm_mix_w, w_in, gdn_conv_w, gdn_A_log, gdn_dt_bias, gdn_out_norm_w, moba_out_norm_w, w_out, norm_ffn_w, w_router_group, b_router_group, w_router_expert, b_router_expert, w_expert_gate, w_expert_up, w_expert_down, norm_final_w):
    batch, seq, d = x.shape
    n = batch * seq
    assert w_in.shape[0] == 1, "the final norm is fused into the last layer's combine; one layer supported"
    l = 0
    xf = x.reshape(n, d).astype(F32)
    gw = 4 * GDN_WIDTH
    w_l = w_in[l]
    mq0 = gw + 2 * GDN_HEADS
    w_main = jnp.concatenate([w_l[:, :gw], w_l[:, mq0:mq0 + MOBA_WIDTH] * (HEAD_DIM ** -0.5 * LOG2E),
                              w_l[:, mq0 + MOBA_WIDTH:]], axis=1).astype(BF16)
    w_ba = jnp.pad(w_l[:, gw:gw + 2 * GDN_HEADS], ((0, 0), (0, LANES - 2 * GDN_HEADS))).astype(BF16)
    proj = norm_matmul(xf, norm_mix_w[l], w_main, BF16, 512, w_main.shape[1] // 4)
    ba = norm_matmul(xf, norm_mix_w[l], w_ba, F32, 512, LANES)
    og = gdn_heads(proj, ba, gdn_conv_w[l].astype(F32), gdn_A_log[l], gdn_dt_bias[l],
                   gdn_out_norm_w[l].astype(F32), batch, seq)
    om = moba_heads(proj, moba_out_norm_w[l].astype(F32), batch, seq, gw // HEAD_DIM)

    w_o = w_out[l].astype(BF16)
    n_r = N_GROUPS + N_EXPERTS
    w_router = jnp.pad(jnp.concatenate([w_router_group[l], w_router_expert[l]], axis=1).astype(F32),
                       ((0, 0), (0, LANES - n_r)))
    wr_hi = w_router.astype(BF16)
    wr_lo = (w_router - wr_hi.astype(F32)).astype(BF16)
    r_bias = jnp.pad(jnp.concatenate([b_router_group[l], b_router_expert[l]]).astype(F32),
                     (0, LANES - n_r)).reshape(1, LANES)
    x2, hp, gates, ids = out_proj(xf, og, om, w_o[:GDN_WIDTH], w_o[GDN_WIDTH:], norm_ffn_w[l].astype(F32),
                                  wr_hi, wr_lo, r_bias, 256)

    pos, blk_expert, n_used, n_rows = _dispatch_plan(ids[:, :TOPK_IN_GROUP])
    pos0, pos1 = pos[:, 0], pos[:, 1]
    xb = moe_dispatch(hp, pos0, pos1, n_rows)
    yb = moe_experts(xb, blk_expert, n_used, w_expert_gate[l].astype(BF16),
                     w_expert_up[l].astype(BF16), w_expert_down[l].astype(BF16))
    out = moe_combine(yb, pos0, pos1, x2, gates, norm_final_w.astype(F32))
    return out.reshape(batch, seq, d).astype(x.dtype)
```

```python
import functools
import itertools

import jax
import jax.numpy as jnp
from jax import lax
from jax.experimental import pallas as pl
from jax.experimental.pallas import tpu as pltpu

F32 = jnp.float32
BF16 = jnp.bfloat16
U32 = jnp.uint32
I32 = jnp.int32

HEAD_DIM = 128
GDN_HEADS = 8
MOBA_HEADS = 8
GDN_WIDTH = GDN_HEADS * HEAD_DIM
MOBA_WIDTH = MOBA_HEADS * HEAD_DIM
CONV_WIDTH = 4
GDN_CHUNK = 64
MOBA_BLOCK = 256
MOBA_TOPK = 3
N_GROUPS = 4
EXPERTS_PER_GROUP = 8
N_EXPERTS = N_GROUPS * EXPERTS_PER_GROUP
TOPK_IN_GROUP = 2
RMS_EPS = 1e-6
NEG_INF = -1e30
MOBA_MASK = -(2.0 ** 99)
LOG2E = 1.4426950408889634
LANES = 128
SUBLANES = 8
VMEM_LIMIT = 56 * 1024 * 1024

GDN_TILE = 256
GDN_HEADS_PER_STEP = 8
MOBA_HEADS_PER_STEP = 4
MOE_ROWS = 256
DISPATCH_ROWS = 256
ZERO_ROWS = 128
COMBINE_ROWS = 128
HI = lax.Precision.HIGHEST


def _cparams(sem, **kw):
    return pltpu.CompilerParams(dimension_semantics=sem, vmem_limit_bytes=VMEM_LIMIT, **kw)


def _dot(a, b):
    return jnp.dot(a, b, preferred_element_type=F32)


def _dot_nt(a, b, precision=None):
    return lax.dot_general(a, b, (((1,), (1,)), ((), ())), preferred_element_type=F32,
                           precision=precision)


def _sigmoid(x):
    return 1.0 / (1.0 + jnp.exp(-x))


def _norm_matmul_body(x_ref, nw_ref, w_ref, o_ref):
    x = x_ref[...]
    ms = jnp.mean(x * x, axis=-1, keepdims=True)
    h = (x * lax.rsqrt(ms + RMS_EPS) * nw_ref[...]).astype(BF16)
    o_ref[...] = _dot(h, w_ref[...]).astype(o_ref.dtype)


def norm_matmul(x, nw, w, out_dtype, tm, tn):
    n, d = x.shape
    width = w.shape[1]
    return pl.pallas_call(
        _norm_matmul_body,
        grid=(width // tn, n // tm),
        in_specs=[pl.BlockSpec((tm, d), lambda j, i: (i, 0)),
                  pl.BlockSpec((1, d), lambda j, i: (0, 0)),
                  pl.BlockSpec((d, tn), lambda j, i: (0, j))],
        out_specs=pl.BlockSpec((tm, tn), lambda j, i: (i, j)),
        out_shape=jax.ShapeDtypeStruct((n, width), out_dtype),
        compiler_params=_cparams(("arbitrary", "arbitrary")),
        name="norm_in_proj",
    )(x, nw.reshape(1, d), w)


def _gdn_body(alog_ref, dtb_ref, q_ref, k_ref, v_ref, z_ref, ba_ref, cwq_ref, cwk_ref, cwv_ref,
              onw_ref, o_ref, *scratch):
    tt = GDN_TILE
    c = GDN_CHUNK
    t = pl.program_id(2)

    @pl.when(t == 0)
    def _():
        for hh in range(GDN_HEADS_PER_STEP):
            s_ref, prev_ref, vn_ref, _ = scratch[4 * hh:4 * hh + 4]
            s_ref[...] = jnp.zeros_like(s_ref)
            prev_ref[...] = jnp.zeros_like(prev_ref)
            vn_ref[...] = jnp.zeros_like(vn_ref)

    row_8 = lax.broadcasted_iota(jnp.int32, (8, LANES), 0)
    lane_t = lax.broadcasted_iota(jnp.int32, (tt, LANES), 1)
    row = lax.broadcasted_iota(jnp.int32, (tt, tt), 0)
    col = lax.broadcasted_iota(jnp.int32, (tt, tt), 1)
    same = (row // c) == (col // c)
    causal = jnp.logical_and(same, col <= row)
    strict = jnp.logical_and(same, col < row)
    causal16 = jnp.where(causal, 1.0, 0.0).astype(BF16)
    same16 = jnp.where(same, 1.0, 0.0).astype(BF16)
    eye = jnp.where(row == col, 1.0, 0.0)
    colk = lax.broadcasted_iota(jnp.int32, (HEAD_DIM, tt), 1)
    ba = ba_ref[...]

    heads = [_gdn_one_head(hh, pl.program_id(1) * GDN_HEADS_PER_STEP + hh, alog_ref, dtb_ref, q_ref, k_ref,
                           v_ref, z_ref, ba, cwq_ref, cwk_ref, cwv_ref, onw_ref, o_ref,
                           *scratch[4 * hh:4 * hh + 4],
                           row_8, lane_t, causal, strict, causal16, same16, eye, colk)
             for hh in range(GDN_HEADS_PER_STEP)]
    for _ in itertools.zip_longest(*heads):
        pass


def _gdn_one_head(hh, h, alog_ref, dtb_ref, q_ref, k_ref, v_ref, z_ref, ba, cwq_ref, cwk_ref, cwv_ref,
                  onw_ref, o_ref, s_ref, prev_ref, vn_ref, oacc_ref,
                  row_8, lane_t, causal, strict, causal16, same16, eye, colk):
    tt = GDN_TILE
    c = GDN_CHUNK
    lanes = slice(hh * HEAD_DIM, (hh + 1) * HEAD_DIM)

    def conv_silu(x_ref, cw_ref, idx):
        x = x_ref[:, lanes].astype(F32)
        p = prev_ref[idx]
        w = cw_ref[:, lanes]
        acc = x * w[CONV_WIDTH - 1:CONV_WIDTH, :]
        for s in range(1, CONV_WIDTH):
            xs = pltpu.roll(x, s, axis=0)
            head = jnp.where(row_8 < s, pltpu.roll(p, s, axis=0), xs[:8])
            xs = jnp.concatenate([head, xs[8:]], axis=0)
            acc = acc + xs * w[CONV_WIDTH - 1 - s:CONV_WIDTH - s, :]
        prev_ref[idx] = x[tt - 8:tt, :]
        return acc * _sigmoid(acc)

    q = conv_silu(q_ref, cwq_ref, 0)
    k = conv_silu(k_ref, cwk_ref, 1)
    v = conv_silu(v_ref, cwv_ref, 2)
    q = q * lax.rsqrt(jnp.sum(q * q, axis=-1, keepdims=True) + 1e-6) * (HEAD_DIM ** -0.5)
    k = k * lax.rsqrt(jnp.sum(k * k, axis=-1, keepdims=True) + 1e-6)
    yield

    b_col = jnp.sum(jnp.where(lane_t == h, ba, 0.0), axis=-1, keepdims=True)
    a_col = jnp.sum(jnp.where(lane_t == h + GDN_HEADS, ba, 0.0), axis=-1, keepdims=True)
    beta = _sigmoid(b_col)
    xa = a_col + dtb_ref[h]
    softplus = jnp.maximum(xa, 0.0) + jnp.log(1.0 + jnp.exp(-jnp.abs(xa)))
    g = -jnp.exp(jnp.full((1, 1), alog_ref[h], F32)) * softplus

    g_hi = g.astype(BF16).astype(F32)
    g_mid = (g - g_hi).astype(BF16).astype(F32)
    g_lo = g - g_hi - g_mid
    g3 = jnp.where(lane_t == 0, g_hi, jnp.where(lane_t == 1, g_mid, jnp.where(lane_t == 2, g_lo, 0.0)))
    g3 = g3.astype(BF16)

    def three(r):
        return r[:, 0:1] + r[:, 1:2] + r[:, 2:3]

    gc_col = three(_dot(causal16, g3))
    gc_b = jnp.broadcast_to(gc_col, (tt, LANES))
    glast_b = jnp.broadcast_to(three(_dot(same16, g3)), (tt, LANES))
    gc_row = jnp.transpose(gc_b)[0:1, :]
    decay = jnp.exp(jnp.where(causal, gc_col - gc_row, NEG_INF))
    yield

    kb = k * beta
    k16 = k.astype(BF16)
    kk = _dot_nt(kb.astype(BF16), k16)
    lmat = jnp.where(strict, kk * decay, 0.0)
    yield
    attn = _dot_nt(q.astype(BF16), k16) * decay

    tinv = eye - lmat
    m = lmat
    for _ in range(5):
        m16 = m.astype(BF16)
        m = _dot(m16, m16)
        yield
        tinv = tinv + _dot(tinv.astype(BF16), m.astype(BF16))
        yield

    egc = jnp.exp(gc_b)
    rhs = jnp.concatenate([v * beta, kb * egc], axis=-1)
    sol = _dot(tinv.astype(BF16), rhs.astype(BF16))
    yield
    u = sol[:, :HEAD_DIM]
    w16 = sol[:, HEAD_DIM:].astype(BF16)
    qd16 = (q * egc).astype(BF16)
    kd = k * jnp.exp(glast_b - gc_b)
    kdt = jnp.transpose(kd)
    gtot_b = jnp.exp(glast_b)
    attn16 = attn.astype(BF16)

    s = s_ref[...]
    for n in range(tt // c):
        sl = slice(n * c, (n + 1) * c)
        s16 = s.astype(BF16)
        v_new = u[sl] - _dot(w16[sl], s16)
        vn_ref[sl, :] = v_new
        yield
        vn16 = vn_ref[...].astype(BF16)
        oacc_ref[sl, :] = _dot(qd16[sl], s16) + _dot(attn16[sl], vn16)
        kdt_n = jnp.where((colk // c) == n, kdt, 0.0).astype(BF16)
        s = s * gtot_b[n * c:n * c + 1, :] + _dot(kdt_n, vn16)
        yield
    s_ref[...] = s

    o = oacc_ref[...]
    o = o * lax.rsqrt(jnp.mean(o * o, axis=-1, keepdims=True) + RMS_EPS) * onw_ref[...]
    z = z_ref[:, lanes].astype(F32)
    o_ref[:, lanes] = (o * (z * _sigmoid(z))).astype(o_ref.dtype)


def gdn_heads(proj, ba, conv_w, a_log, dt_bias, out_norm_w, batch, seq):
    n = batch * seq
    tt = GDN_TILE
    nt = seq // tt
    hps = GDN_HEADS_PER_STEP
    ng = GDN_HEADS // hps
    wide = hps * HEAD_DIM

    def col_spec(section):
        return pl.BlockSpec((tt, wide), lambda b, h, t, *_: (b * nt + t, section * ng + h))

    def cw_spec(section):
        return pl.BlockSpec((CONV_WIDTH, wide), lambda b, h, t, *_: (0, section * ng + h))

    grid_spec = pltpu.PrefetchScalarGridSpec(
        num_scalar_prefetch=2,
        grid=(batch, ng, nt),
        in_specs=[col_spec(0), col_spec(1), col_spec(2), col_spec(3),
                  pl.BlockSpec((tt, LANES), lambda b, h, t, *_: (b * nt + t, 0)),
                  cw_spec(0), cw_spec(1), cw_spec(2),
                  pl.BlockSpec((1, HEAD_DIM), lambda b, h, t, *_: (0, 0))],
        out_specs=pl.BlockSpec((tt, wide), lambda b, h, t, *_: (b * nt + t, h)),
        scratch_shapes=[pltpu.VMEM((HEAD_DIM, HEAD_DIM), F32),
                        pltpu.VMEM((3, 8, HEAD_DIM), F32),
                        pltpu.VMEM((tt, HEAD_DIM), F32),
                        pltpu.VMEM((tt, HEAD_DIM), F32)] * hps)
    return pl.pallas_call(
        _gdn_body,
        grid_spec=grid_spec,
        out_shape=jax.ShapeDtypeStruct((n, GDN_WIDTH), BF16),
        compiler_params=_cparams(("arbitrary", "arbitrary", "arbitrary")),
        name="gdn_heads",
    )(a_log.astype(F32), dt_bias.astype(F32), proj, proj, proj, proj, ba,
      conv_w, conv_w, conv_w, out_norm_w.reshape(1, HEAD_DIM))


def _moba_body(slope_ref, q_ref, k_ref, v_ref, onw_ref, o_ref, *scratch, n_blocks):
    blk = MOBA_BLOCK
    tq = 2 * blk
    nb = n_blocks
    nbp = -(-nb // 8) * 8
    hps = MOBA_HEADS_PER_STEP
    cq = pl.program_id(2)
    c0 = 2 * cq
    lane = lax.broadcasted_iota(I32, (blk, LANES), 1)
    row = lax.broadcasted_iota(I32, (blk, blk), 0)
    col = lax.broadcasted_iota(I32, (blk, blk), 1)
    rid = lax.broadcasted_iota(I32, (LANES, tq), 0)
    bid = lax.broadcasted_iota(I32, (nbp, tq), 0)
    c_row = c0 + (lax.broadcasted_iota(I32, (nbp, tq), 1) >= blk).astype(I32)
    ones_ext_t = jnp.where(jnp.logical_and(rid >= nb, rid < nb + 3), 1.0, 0.0).astype(BF16)

    def head_lanes(hh):
        return slice(hh * HEAD_DIM, (hh + 1) * HEAD_DIM)

    def keys(hh, j, nblk=1):
        return scratch[3 * hh + 1][pl.ds(pl.multiple_of(j * blk, blk), nblk * blk), :]

    def values_t(hh, j, nblk=1):
        return scratch[3 * hh + 2][:, pl.ds(pl.multiple_of(j * blk, blk), nblk * blk)]

    @pl.when(cq == 0)
    def _():
        t_in = lax.broadcasted_iota(I32, (blk, LANES), 0)
        for hh in range(hps):
            kmean_ref, kx_ref, vt_ref = scratch[3 * hh:3 * hh + 3]
            slope2 = slope_ref[pl.program_id(1) * hps + hh] * LOG2E
            kmean_ref[...] = jnp.zeros_like(kmean_ref)
            for j in range(nb):
                kj = k_ref[j * blk:(j + 1) * blk, head_lanes(hh)]
                kmean_ref[j:j + 1, :] = jnp.mean(kj.astype(F32), axis=0, keepdims=True)
                bias = slope2 * (t_in + j * blk).astype(F32)
                b_hi = bias.astype(BF16)
                r1 = bias - b_hi.astype(F32)
                b_mid = r1.astype(BF16)
                b_lo = (r1 - b_mid.astype(F32)).astype(BF16)
                ext = jnp.where(lane == j, 1.0, 0.0).astype(BF16)
                ext = jnp.where(lane == nb, b_hi, ext)
                ext = jnp.where(lane == nb + 1, b_mid, ext)
                ext = jnp.where(lane == nb + 2, b_lo, ext)
                kx_ref[j * blk:(j + 1) * blk, :] = jnp.concatenate([kj, ext], axis=1)
                vj = v_ref[j * blk:(j + 1) * blk, head_lanes(hh)].astype(F32)
                vt_ref[:, j * blk:(j + 1) * blk] = jnp.transpose(vj).astype(BF16)

    def start(s, vals_t):
        m = jnp.max(s, axis=0, keepdims=True)
        p = jnp.exp2(s - m)
        return m, jnp.sum(p, axis=0, keepdims=True), _dot(vals_t, p.astype(BF16))

    def update(carry, s, vals_t):
        m_i, l_i, acc = carry
        m_new = jnp.maximum(m_i, jnp.max(s, axis=0, keepdims=True))
        alpha = jnp.exp2(m_i - m_new)
        p = jnp.exp2(s - m_new)
        return (m_new, l_i * alpha + jnp.sum(p, axis=0, keepdims=True),
                acc * alpha + _dot(vals_t, p.astype(BF16)))

    qx = [None] * hps
    carry0 = [None] * hps

    def prologue(hh):
        q_t = jnp.transpose(q_ref[:, head_lanes(hh)].astype(F32))
        q_t16 = q_t.astype(BF16)
        yield
        qx0_t = jnp.concatenate([q_t16, ones_ext_t], axis=0)
        state = []
        for half in range(2):
            s_own = _dot(keys(hh, c0 + half), qx0_t[:, half * blk:(half + 1) * blk])
            s_own = jnp.where(row <= col, s_own, MOBA_MASK)
            yield
            state.append(start(s_own, values_t(hh, c0 + half)))
            yield
        gate_t = jnp.dot(scratch[3 * hh][...], q_t, preferred_element_type=F32, precision=HI)[:nbp]
        gm = jnp.where(bid < c_row, gate_t, NEG_INF)
        sel = jnp.zeros((nbp, tq), F32)
        for s in range(MOBA_TOPK):
            mx = jnp.max(gm, axis=0, keepdims=True)
            idx = jnp.min(jnp.where(gm == mx, bid, LANES), axis=0, keepdims=True)
            pick = bid == idx
            sel = jnp.where(jnp.logical_and(pick, s < c_row), 1.0, sel)
            gm = jnp.where(pick, -3e38, gm)
        mask_t = jnp.where(sel > 0.5, 0.0, MOBA_MASK)
        ext_t = jnp.concatenate([mask_t, jnp.zeros((LANES - nbp, tq), F32)], axis=0)
        ext_t = jnp.where(rid < nb, ext_t, jnp.where(rid < nb + 3, 1.0, 0.0))
        qx[hh] = jnp.concatenate([q_t16, ext_t.astype(BF16)], axis=0)
        yield
        s_c0 = _dot(keys(hh, c0), qx[hh][:, blk:])
        yield
        second = update(state[1], s_c0, values_t(hh, c0))
        carry0[hh] = tuple(jnp.concatenate([a, b], axis=1) for a, b in zip(state[0], second))

    for _ in itertools.zip_longest(*[prologue(hh) for hh in range(hps)]):
        pass

    def body(p, carries):
        scores = [_dot(keys(hh, 2 * p, 2), qx[hh]) for hh in range(hps)]
        return tuple(update(carries[hh], scores[hh], values_t(hh, 2 * p, 2)) for hh in range(hps))

    final = lax.fori_loop(0, cq, body, tuple(carry0))
    for hh in range(hps):
        _, l_f, acc_f = final[hh]
        o_t = acc_f / l_f
        o_t = o_t * lax.rsqrt(jnp.mean(o_t * o_t, axis=0, keepdims=True) + RMS_EPS)
        o_ref[:, head_lanes(hh)] = (jnp.transpose(o_t) * onw_ref[...]).astype(o_ref.dtype)


def moba_heads(proj, out_norm_w, batch, seq, col_off):
    n = batch * seq
    blk = MOBA_BLOCK
    nb = seq // blk
    nq = nb // 2
    hps = MOBA_HEADS_PER_STEP
    ng = MOBA_HEADS // hps
    wide = hps * HEAD_DIM
    sec0 = col_off // hps
    slopes = jnp.exp2(-8.0 * jnp.arange(1, MOBA_HEADS + 1, dtype=F32) / MOBA_HEADS)
    grid_spec = pltpu.PrefetchScalarGridSpec(
        num_scalar_prefetch=1,
        grid=(batch, ng, nq),
        in_specs=[pl.BlockSpec((2 * blk, wide), lambda b, h, c, *_: (b * nq + c, sec0 + h)),
                  pl.BlockSpec((seq, wide), lambda b, h, c, *_: (b, sec0 + ng + h)),
                  pl.BlockSpec((seq, wide), lambda b, h, c, *_: (b, sec0 + 2 * ng + h)),
                  pl.BlockSpec((1, HEAD_DIM), lambda b, h, c, *_: (0, 0))],
        out_specs=pl.BlockSpec((2 * blk, wide), lambda b, h, c, *_: (b * nq + c, h)),
        scratch_shapes=[pltpu.VMEM((LANES, HEAD_DIM), F32),
                        pltpu.VMEM((nb * blk, HEAD_DIM + LANES), BF16),
                        pltpu.VMEM((HEAD_DIM, nb * blk), BF16)] * hps)
    assert nb % 2 == 0 and nb + 3 <= LANES and col_off % hps == 0
    return pl.pallas_call(
        functools.partial(_moba_body, n_blocks=nb),
        grid_spec=grid_spec,
        out_shape=jax.ShapeDtypeStruct((n, MOBA_WIDTH), BF16),
        compiler_params=_cparams(("arbitrary", "arbitrary", "arbitrary")),
        name="moba_heads",
    )(slopes, proj, proj, proj, out_norm_w.reshape(1, HEAD_DIM))


def _out_proj_body(x_ref, og_ref, om_ref, wg_ref, wm_ref, nw_ref, wrh_ref, wrl_ref, rb_ref,
                   x2_ref, hp_ref, gt_ref, id_ref):
    tm = x_ref.shape[0]
    x2 = x_ref[...] + _dot(og_ref[...], wg_ref[...]) + _dot(om_ref[...], wm_ref[...])
    x2_ref[...] = x2
    ms = jnp.mean(x2 * x2, axis=-1, keepdims=True)
    h2 = x2 * lax.rsqrt(ms + RMS_EPS) * nw_ref[...]
    h_hi = h2.astype(BF16)
    hp_ref[...] = h2
    h_lo = (h2 - h_hi.astype(F32)).astype(BF16)
    lg = _dot(h_hi, wrh_ref[...]) + _dot(h_hi, wrl_ref[...]) + _dot(h_lo, wrh_ref[...]) + rb_ref[...]

    lane = lax.broadcasted_iota(I32, (tm, LANES), 1)
    is_g = lane < N_GROUPS
    mg = jnp.max(jnp.where(is_g, lg, NEG_INF), axis=-1, keepdims=True)
    g_idx = jnp.min(jnp.where(jnp.logical_and(is_g, lg == mg), lane, LANES), axis=-1, keepdims=True)
    sum_g = jnp.sum(jnp.where(is_g, jnp.exp(lg - mg), 0.0), axis=-1, keepdims=True)
    p_top_g = 1.0 / sum_g
    lo = N_GROUPS + g_idx * EXPERTS_PER_GROUP
    in_grp = jnp.logical_and(lane >= lo, lane < lo + EXPERTS_PER_GROUP)
    m1 = jnp.max(jnp.where(in_grp, lg, NEG_INF), axis=-1, keepdims=True)
    i1 = jnp.min(jnp.where(jnp.logical_and(in_grp, lg == m1), lane, LANES), axis=-1, keepdims=True)
    rest = jnp.logical_and(in_grp, lane != i1)
    m2 = jnp.max(jnp.where(rest, lg, NEG_INF), axis=-1, keepdims=True)
    i2 = jnp.min(jnp.where(jnp.logical_and(rest, lg == m2), lane, LANES), axis=-1, keepdims=True)
    e2 = jnp.exp(m2 - m1)
    gate1 = p_top_g / (1.0 + e2)
    gate2 = p_top_g * e2 / (1.0 + e2)
    gt_ref[...] = jnp.where(lane == 0, gate1, jnp.where(lane == 1, gate2, 0.0))
    id_ref[...] = jnp.where(lane == 0, i1 - N_GROUPS, jnp.where(lane == 1, i2 - N_GROUPS, 0))


def out_proj(x, og, om, w_g, w_m, nw, wr_hi, wr_lo, r_bias, tm):
    n, d = x.shape
    const = lambda i: (0, 0)
    rows = lambda i: (i, 0)
    return pl.pallas_call(
        _out_proj_body,
        grid=(n // tm,),
        in_specs=[pl.BlockSpec((tm, d), rows),
                  pl.BlockSpec((tm, GDN_WIDTH), rows),
                  pl.BlockSpec((tm, MOBA_WIDTH), rows),
                  pl.BlockSpec((GDN_WIDTH, d), const),
                  pl.BlockSpec((MOBA_WIDTH, d), const),
                  pl.BlockSpec((1, d), const),
                  pl.BlockSpec((d, LANES), const),
                  pl.BlockSpec((d, LANES), const),
                  pl.BlockSpec((1, LANES), const)],
        out_specs=[pl.BlockSpec((tm, d), rows),
                   pl.BlockSpec((tm, d), rows),
                   pl.BlockSpec((tm, LANES), rows),
                   pl.BlockSpec((tm, LANES), rows)],
        out_shape=[jax.ShapeDtypeStruct((n, d), F32),
                   jax.ShapeDtypeStruct((n, d), F32),
                   jax.ShapeDtypeStruct((n, LANES), F32),
                   jax.ShapeDtypeStruct((n, LANES), I32)],
        compiler_params=_cparams(("arbitrary",)),
        name="out_proj_router",
    )(x, og, om, w_g, w_m, nw.reshape(1, d), wr_hi, wr_lo, r_bias)


def _zero_rows(start_not_wait, zero_buf, xb_hbm, sem, off, n):
    z = ZERO_ROWS

    def go(src, dst):
        cp = pltpu.make_async_copy(src, dst, sem)
        if start_not_wait:
            cp.start()
        else:
            cp.wait()

    head = jnp.bitwise_and(-off, SUBLANES - 1)
    for r in range(SUBLANES - 1):
        @pl.when(r < head)
        def _(r=r):
            go(zero_buf.at[pl.ds(0, 1)], xb_hbm.at[pl.ds(off + r, 1)])

    off = off + head
    n = n - head
    n_full = n // z

    def chunk(c, carry):
        go(zero_buf, xb_hbm.at[pl.ds(pl.multiple_of(off + c * z, SUBLANES), z)])
        return carry

    lax.fori_loop(0, n_full, chunk, 0)
    off = off + n_full * z
    rem = n - n_full * z
    bit = z // 2
    while bit >= SUBLANES:
        @pl.when((rem & bit) != 0)
        def _(off=off, bit=bit):
            go(zero_buf.at[pl.ds(0, bit)], xb_hbm.at[pl.ds(pl.multiple_of(off, SUBLANES), bit)])
        off = off + (rem & bit)
        bit //= 2


def _dispatch_body(pos0_ref, pos1_ref, padfrom_ref, padn_ref, hp_ref, xb_hbm, zero_buf, sem, zsem):
    ts = DISPATCH_ROWS
    base = pl.program_id(0) * ts

    @pl.when(pl.program_id(0) == 0)
    def _():
        zero_buf[...] = jnp.zeros_like(zero_buf)
        for start_not_wait in (True, False):
            def per_range(e, carry, start_not_wait=start_not_wait):
                _zero_rows(start_not_wait, zero_buf, xb_hbm, zsem, padfrom_ref[e], padn_ref[e])
                return carry
            lax.fori_loop(0, N_EXPERTS + 1, per_range, 0)

    def body(r, carry):
        src = hp_ref.at[pl.ds(r, 1)]
        pltpu.make_async_copy(src, xb_hbm.at[pl.ds(pos0_ref[base + r], 1)], sem).start()
        pltpu.make_async_copy(src, xb_hbm.at[pl.ds(pos1_ref[base + r], 1)], sem).start(priority=1)
        return carry

    lax.fori_loop(0, ts, body, 0, unroll=8)
    for _ in range(TOPK_IN_GROUP):
        pltpu.make_async_copy(hp_ref, xb_hbm.at[pl.ds(0, ts)], sem).wait()


def moe_dispatch(hp, pos0, pos1, pad_from, pad_n, n_rows):
    n, dh = hp.shape
    ts = DISPATCH_ROWS
    grid_spec = pltpu.PrefetchScalarGridSpec(
        num_scalar_prefetch=4,
        grid=(n // ts,),
        in_specs=[pl.BlockSpec((ts, dh), lambda i, *_: (i, 0))],
        out_specs=pl.BlockSpec(memory_space=pl.ANY),
        scratch_shapes=[pltpu.VMEM((ZERO_ROWS, dh), F32),
                        pltpu.SemaphoreType.DMA(()), pltpu.SemaphoreType.DMA(())])
    return pl.pallas_call(
        _dispatch_body,
        grid_spec=grid_spec,
        out_shape=jax.ShapeDtypeStruct((n_rows, dh), F32),
        compiler_params=_cparams(("arbitrary",)),
        name="moe_dispatch",
    )(pos0, pos1, pad_from, pad_n, hp)


def _moe_body(blk_e_ref, nblk_ref, xb_ref, wg_ref, wu_ref, wd_ref, y_ref):
    del blk_e_ref
    i = pl.program_id(0)

    @pl.when(i < nblk_ref[0])
    def _():
        xb = xb_ref[...].astype(BF16)
        g = _dot(xb, wg_ref[0])
        u = _dot(xb, wu_ref[0])
        hm = (g * _sigmoid(g) * u).astype(BF16)
        y_ref[...] = _dot(hm, wd_ref[0])

    @pl.when(i >= nblk_ref[0])
    def _():
        y_ref[...] = jnp.zeros_like(y_ref)


def moe_experts(xb, blk_expert, n_used, w_gate, w_up, w_down):
    rb = MOE_ROWS
    n_rows, dh = xb.shape
    d = dh
    de = w_gate.shape[-1]
    grid_spec = pltpu.PrefetchScalarGridSpec(
        num_scalar_prefetch=2,
        grid=(n_rows // rb,),
        in_specs=[pl.BlockSpec((rb, dh), lambda i, be, nu: (i, 0)),
                  pl.BlockSpec((1, d, de), lambda i, be, nu: (be[i], 0, 0)),
                  pl.BlockSpec((1, d, de), lambda i, be, nu: (be[i], 0, 0)),
                  pl.BlockSpec((1, de, d), lambda i, be, nu: (be[i], 0, 0))],
        out_specs=pl.BlockSpec((rb, dh), lambda i, be, nu: (i, 0)))
    return pl.pallas_call(
        _moe_body,
        grid_spec=grid_spec,
        out_shape=jax.ShapeDtypeStruct((n_rows, dh), F32),
        compiler_params=_cparams(("arbitrary",)),
        name="moe_experts",
    )(blk_expert, n_used, xb, w_gate, w_up, w_down)


def _combine_body(pos0_ref, pos1_ref, y_hbm, x2_ref, gt_ref, nw_ref, o_ref, buf, sem):
    tf = COMBINE_ROWS
    i = pl.program_id(0)
    n_steps = pl.num_programs(0)
    slot = lax.rem(i, 2)

    def issue(step, sl):
        def body(r, carry):
            t = step * tf + r
            pltpu.make_async_copy(y_hbm.at[pl.ds(pos0_ref[t], 1)], buf.at[sl, 0, pl.ds(r, 1)],
                                  sem.at[sl]).start()
            pltpu.make_async_copy(y_hbm.at[pl.ds(pos1_ref[t], 1)], buf.at[sl, 1, pl.ds(r, 1)],
                                  sem.at[sl]).start(priority=1)
            return carry
        lax.fori_loop(0, tf, body, 0, unroll=8)

    @pl.when(i == 0)
    def _():
        issue(0, 0)

    @pl.when(i + 1 < n_steps)
    def _():
        issue(i + 1, 1 - slot)

    for kk in range(TOPK_IN_GROUP):
        pltpu.make_async_copy(y_hbm.at[pl.ds(0, tf)], buf.at[slot, kk], sem.at[slot]).wait()

    gt = gt_ref[...]
    xo = x2_ref[...] + gt[:, 0:1] * buf[slot, 0] + gt[:, 1:2] * buf[slot, 1]
    ms = jnp.mean(xo * xo, axis=-1, keepdims=True)
    o_ref[...] = xo * lax.rsqrt(ms + RMS_EPS) * nw_ref[...]


def moe_combine(yb, pos0, pos1, x2, gates, nw):
    n, d = x2.shape
    tf = COMBINE_ROWS
    grid_spec = pltpu.PrefetchScalarGridSpec(
        num_scalar_prefetch=2,
        grid=(n // tf,),
        in_specs=[pl.BlockSpec(memory_space=pl.ANY),
                  pl.BlockSpec((tf, d), lambda i, *_: (i, 0)),
                  pl.BlockSpec((tf, LANES), lambda i, *_: (i, 0)),
                  pl.BlockSpec((1, d), lambda i, *_: (0, 0))],
        out_specs=pl.BlockSpec((tf, d), lambda i, *_: (i, 0)),
        scratch_shapes=[pltpu.VMEM((2, 2, tf, d), F32), pltpu.SemaphoreType.DMA((2,))])
    return pl.pallas_call(
        _combine_body,
        grid_spec=grid_spec,
        out_shape=jax.ShapeDtypeStruct((n, d), F32),
        compiler_params=_cparams(("arbitrary",)),
        name="moe_combine",
    )(pos0, pos1, yb, x2, gates, nw.reshape(1, d))


def _dispatch_plan(expert_id):
    n_tok, k = expert_id.shape
    rb = MOE_ROWS
    n_assign = n_tok * k
    e_flat = expert_id.reshape(-1)
    onehot = (e_flat[:, None] == jnp.arange(N_EXPERTS, dtype=I32)[None, :]).astype(I32)
    csum = jnp.cumsum(onehot, axis=0)
    counts = csum[-1]
    padded = (counts + rb - 1) // rb * rb
    pad_end = jnp.cumsum(padded)
    pad_start = pad_end - padded
    pos = jnp.sum(onehot * (pad_start[None, :] + csum - 1), axis=1).astype(I32).reshape(n_tok, k)
    n_rb = -(-n_assign // rb) + N_EXPERTS
    blk_start = jnp.arange(n_rb, dtype=I32) * rb
    blk_expert = jnp.minimum(jnp.sum((pad_end[None, :] <= blk_start[:, None]).astype(I32), axis=1),
                             N_EXPERTS - 1).astype(I32)
    n_used = (pad_end[-1] // rb).astype(I32).reshape(1)
    pad_from = jnp.concatenate([pad_start + counts, pad_end[-1:]]).astype(I32)
    pad_n = jnp.concatenate([padded - counts, n_rb * rb - pad_end[-1:]]).astype(I32)
    return pos, blk_expert, n_used, pad_from, pad_n, n_rb * rb


def kernel(x, norm_mix_w, w_in, gdn_conv_w, gdn_A_log, gdn_dt_bias, gdn_out_norm_w, moba_out_norm_w, w_out, norm_ffn_w, w_router_group, b_router_group, w_router_expert, b_router_expert, w_expert_gate, w_expert_up, w_expert_down, norm_final_w):
    batch, seq, d = x.shape
    n = batch * seq
    assert w_in.shape[0] == 1, "the final norm is fused into the last layer's combine; one layer supported"
    l = 0
    xf = x.reshape(n, d).astype(F32)
    gw = 4 * GDN_WIDTH
    w_l = w_in[l]
    mq0 = gw + 2 * GDN_HEADS
    w_main = jnp.concatenate([w_l[:, :gw].astype(BF16),
                              (w_l[:, mq0:mq0 + MOBA_WIDTH] * (HEAD_DIM ** -0.5 * LOG2E)).astype(BF16),
                              w_l[:, mq0 + MOBA_WIDTH:].astype(BF16)], axis=1)
    w_ba = jnp.pad(w_l[:, gw:gw + 2 * GDN_HEADS], ((0, 0), (0, LANES - 2 * GDN_HEADS))).astype(BF16)
    proj = norm_matmul(xf, norm_mix_w[l], w_main, BF16, 512, w_main.shape[1] // 4)
    ba = norm_matmul(xf, norm_mix_w[l], w_ba, F32, 512, LANES)
    og = gdn_heads(proj, ba, gdn_conv_w[l].astype(F32), gdn_A_log[l], gdn_dt_bias[l],
                   gdn_out_norm_w[l].astype(F32), batch, seq)
    om = moba_heads(proj, moba_out_norm_w[l].astype(F32), batch, seq, gw // HEAD_DIM)

    w_o = w_out[l].astype(BF16)
    n_r = N_GROUPS + N_EXPERTS
    w_router = jnp.pad(jnp.concatenate([w_router_group[l], w_router_expert[l]], axis=1).astype(F32),
                       ((0, 0), (0, LANES - n_r)))
    wr_hi = w_router.astype(BF16)
    wr_lo = (w_router - wr_hi.astype(F32)).astype(BF16)
    r_bias = jnp.pad(jnp.concatenate([b_router_group[l], b_router_expert[l]]).astype(F32),
                     (0, LANES - n_r)).reshape(1, LANES)
    x2, hp, gates, ids = out_proj(xf, og, om, w_o[:GDN_WIDTH], w_o[GDN_WIDTH:], norm_ffn_w[l].astype(F32),
                                  wr_hi, wr_lo, r_bias, 256)

    pos, blk_expert, n_used, pad_from, pad_n, n_rows = _dispatch_plan(ids[:, :TOPK_IN_GROUP])
    pos0, pos1 = pos[:, 0], pos[:, 1]
    xb = moe_dispatch(hp, pos0, pos1, pad_from, pad_n, n_rows)
    yb = moe_experts(xb, blk_expert, n_used, w_expert_gate[l].astype(BF16),
                     w_expert_up[l].astype(BF16), w_expert_down[l].astype(BF16))
    out = moe_combine(yb, pos0, pos1, x2, gates, norm_final_w.astype(F32))
    return out.reshape(batch, seq, d).astype(x.dtype)
```

```python
import functools
import itertools

import jax
import jax.numpy as jnp
from jax import lax
from jax.experimental import pallas as pl
from jax.experimental.pallas import tpu as pltpu

F32 = jnp.float32
BF16 = jnp.bfloat16
U32 = jnp.uint32
I32 = jnp.int32

HEAD_DIM = 128
GDN_HEADS = 8
MOBA_HEADS = 8
GDN_WIDTH = GDN_HEADS * HEAD_DIM
MOBA_WIDTH = MOBA_HEADS * HEAD_DIM
CONV_WIDTH = 4
GDN_CHUNK = 64
MOBA_BLOCK = 256
MOBA_TOPK = 3
N_GROUPS = 4
EXPERTS_PER_GROUP = 8
N_EXPERTS = N_GROUPS * EXPERTS_PER_GROUP
TOPK_IN_GROUP = 2
RMS_EPS = 1e-6
NEG_INF = -1e30
MOBA_MASK = -(2.0 ** 99)
LOG2E = 1.4426950408889634
LANES = 128
SUBLANES = 8
VMEM_LIMIT = 56 * 1024 * 1024

GDN_TILE = 256
GDN_HEADS_PER_STEP = 8
MOBA_HEADS_PER_STEP = 4
MOE_ROWS = 256
DISPATCH_ROWS = 256
ZERO_ROWS = 128
COMBINE_ROWS = 128
HI = lax.Precision.HIGHEST


def _cparams(sem, **kw):
    return pltpu.CompilerParams(dimension_semantics=sem, vmem_limit_bytes=VMEM_LIMIT, **kw)


def _dot(a, b):
    return jnp.dot(a, b, preferred_element_type=F32)


def _dot_nt(a, b, precision=None):
    return lax.dot_general(a, b, (((1,), (1,)), ((), ())), preferred_element_type=F32,
                           precision=precision)


def _sigmoid(x):
    return 1.0 / (1.0 + jnp.exp(-x))


def _norm_matmul_body(x_ref, nw_ref, w_ref, o_ref):
    x = x_ref[...]
    ms = jnp.mean(x * x, axis=-1, keepdims=True)
    h = (x * lax.rsqrt(ms + RMS_EPS) * nw_ref[...]).astype(BF16)
    o_ref[...] = _dot(h, w_ref[...]).astype(o_ref.dtype)


def norm_matmul(x, nw, w, out_dtype, tm, tn):
    n, d = x.shape
    width = w.shape[1]
    return pl.pallas_call(
        _norm_matmul_body,
        grid=(width // tn, n // tm),
        in_specs=[pl.BlockSpec((tm, d), lambda j, i: (i, 0)),
                  pl.BlockSpec((1, d), lambda j, i: (0, 0)),
                  pl.BlockSpec((d, tn), lambda j, i: (0, j))],
        out_specs=pl.BlockSpec((tm, tn), lambda j, i: (i, j)),
        out_shape=jax.ShapeDtypeStruct((n, width), out_dtype),
        compiler_params=_cparams(("arbitrary", "arbitrary")),
        name="norm_in_proj",
    )(x, nw.reshape(1, d), w)


def _gdn_body(alog_ref, dtb_ref, q_ref, k_ref, v_ref, z_ref, ba_ref, cwq_ref, cwk_ref, cwv_ref,
              onw_ref, o_ref, *scratch):
    tt = GDN_TILE
    c = GDN_CHUNK
    t = pl.program_id(2)

    @pl.when(t == 0)
    def _():
        for hh in range(GDN_HEADS_PER_STEP):
            s_ref, prev_ref, vn_ref, _ = scratch[4 * hh:4 * hh + 4]
            s_ref[...] = jnp.zeros_like(s_ref)
            prev_ref[...] = jnp.zeros_like(prev_ref)
            vn_ref[...] = jnp.zeros_like(vn_ref)

    row_8 = lax.broadcasted_iota(jnp.int32, (8, LANES), 0)
    lane_t = lax.broadcasted_iota(jnp.int32, (tt, LANES), 1)
    row = lax.broadcasted_iota(jnp.int32, (tt, tt), 0)
    col = lax.broadcasted_iota(jnp.int32, (tt, tt), 1)
    same = (row // c) == (col // c)
    causal = jnp.logical_and(same, col <= row)
    strict = jnp.logical_and(same, col < row)
    causal16 = jnp.where(causal, 1.0, 0.0).astype(BF16)
    same16 = jnp.where(same, 1.0, 0.0).astype(BF16)
    eye = jnp.where(row == col, 1.0, 0.0)
    colk = lax.broadcasted_iota(jnp.int32, (HEAD_DIM, tt), 1)
    ba = ba_ref[...]

    heads = [_gdn_one_head(hh, pl.program_id(1) * GDN_HEADS_PER_STEP + hh, alog_ref, dtb_ref, q_ref, k_ref,
                           v_ref, z_ref, ba, cwq_ref, cwk_ref, cwv_ref, onw_ref, o_ref,
                           *scratch[4 * hh:4 * hh + 4],
                           row_8, lane_t, causal, strict, causal16, same16, eye, colk)
             for hh in range(GDN_HEADS_PER_STEP)]
    for _ in itertools.zip_longest(*heads):
        pass


def _gdn_one_head(hh, h, alog_ref, dtb_ref, q_ref, k_ref, v_ref, z_ref, ba, cwq_ref, cwk_ref, cwv_ref,
                  onw_ref, o_ref, s_ref, prev_ref, vn_ref, oacc_ref,
                  row_8, lane_t, causal, strict, causal16, same16, eye, colk):
    tt = GDN_TILE
    c = GDN_CHUNK
    lanes = slice(hh * HEAD_DIM, (hh + 1) * HEAD_DIM)

    def conv_silu(x_ref, cw_ref, idx):
        x = x_ref[:, lanes].astype(F32)
        p = prev_ref[idx]
        w = cw_ref[:, lanes]
        acc = x * w[CONV_WIDTH - 1:CONV_WIDTH, :]
        for s in range(1, CONV_WIDTH):
            xs = pltpu.roll(x, s, axis=0)
            head = jnp.where(row_8 < s, pltpu.roll(p, s, axis=0), xs[:8])
            xs = jnp.concatenate([head, xs[8:]], axis=0)
            acc = acc + xs * w[CONV_WIDTH - 1 - s:CONV_WIDTH - s, :]
        prev_ref[idx] = x[tt - 8:tt, :]
        return acc * _sigmoid(acc)

    q = conv_silu(q_ref, cwq_ref, 0)
    k = conv_silu(k_ref, cwk_ref, 1)
    v = conv_silu(v_ref, cwv_ref, 2)
    q = q * lax.rsqrt(jnp.sum(q * q, axis=-1, keepdims=True) + 1e-6) * (HEAD_DIM ** -0.5)
    k = k * lax.rsqrt(jnp.sum(k * k, axis=-1, keepdims=True) + 1e-6)
    yield

    b_col = jnp.sum(jnp.where(lane_t == h, ba, 0.0), axis=-1, keepdims=True)
    a_col = jnp.sum(jnp.where(lane_t == h + GDN_HEADS, ba, 0.0), axis=-1, keepdims=True)
    beta = _sigmoid(b_col)
    xa = a_col + dtb_ref[h]
    softplus = jnp.maximum(xa, 0.0) + jnp.log(1.0 + jnp.exp(-jnp.abs(xa)))
    g = -jnp.exp(jnp.full((1, 1), alog_ref[h], F32)) * softplus

    g_hi = g.astype(BF16).astype(F32)
    g_mid = (g - g_hi).astype(BF16).astype(F32)
    g_lo = g - g_hi - g_mid
    g3 = jnp.where(lane_t == 0, g_hi, jnp.where(lane_t == 1, g_mid, jnp.where(lane_t == 2, g_lo, 0.0)))
    g3 = g3.astype(BF16)

    def three(r):
        return r[:, 0:1] + r[:, 1:2] + r[:, 2:3]

    gc_col = three(_dot(causal16, g3))
    gc_b = jnp.broadcast_to(gc_col, (tt, LANES))
    glast_b = jnp.broadcast_to(three(_dot(same16, g3)), (tt, LANES))
    gc_row = jnp.transpose(gc_b)[0:1, :]
    decay = jnp.exp(jnp.where(causal, gc_col - gc_row, NEG_INF))
    yield

    kb = k * beta
    k16 = k.astype(BF16)
    kk = _dot_nt(kb.astype(BF16), k16)
    lmat = jnp.where(strict, kk * decay, 0.0)
    yield
    attn = _dot_nt(q.astype(BF16), k16) * decay

    tinv = eye - lmat
    m16 = lmat.astype(BF16)
    for _ in range(5):
        m16 = _dot(m16, m16).astype(BF16)
        yield
        tinv = tinv + _dot(tinv.astype(BF16), m16)
        yield

    egc = jnp.exp(gc_b)
    rhs = jnp.concatenate([v * beta, kb * egc], axis=-1)
    sol = _dot(tinv.astype(BF16), rhs.astype(BF16))
    yield
    u = sol[:, :HEAD_DIM]
    w16 = sol[:, HEAD_DIM:].astype(BF16)
    qd16 = (q * egc).astype(BF16)
    kd = k * jnp.exp(glast_b - gc_b)
    kdt = jnp.transpose(kd)
    gtot_b = jnp.exp(glast_b)
    attn16 = attn.astype(BF16)

    s = s_ref[...]
    for n in range(tt // c):
        sl = slice(n * c, (n + 1) * c)
        s16 = s.astype(BF16)
        v_new = u[sl] - _dot(w16[sl], s16)
        vn_ref[sl, :] = v_new
        yield
        vn16 = vn_ref[...].astype(BF16)
        oacc_ref[sl, :] = _dot(qd16[sl], s16) + _dot(attn16[sl], vn16)
        kdt_n = jnp.where((colk // c) == n, kdt, 0.0).astype(BF16)
        s = s * gtot_b[n * c:n * c + 1, :] + _dot(kdt_n, vn16)
        yield
    s_ref[...] = s

    o = oacc_ref[...]
    o = o * lax.rsqrt(jnp.mean(o * o, axis=-1, keepdims=True) + RMS_EPS) * onw_ref[...]
    z = z_ref[:, lanes].astype(F32)
    o_ref[:, lanes] = (o * (z * _sigmoid(z))).astype(o_ref.dtype)


def gdn_heads(proj, ba, conv_w, a_log, dt_bias, out_norm_w, batch, seq):
    n = batch * seq
    tt = GDN_TILE
    nt = seq // tt
    hps = GDN_HEADS_PER_STEP
    ng = GDN_HEADS // hps
    wide = hps * HEAD_DIM

    def col_spec(section):
        return pl.BlockSpec((tt, wide), lambda b, h, t, *_: (b * nt + t, section * ng + h))

    def cw_spec(section):
        return pl.BlockSpec((CONV_WIDTH, wide), lambda b, h, t, *_: (0, section * ng + h))

    grid_spec = pltpu.PrefetchScalarGridSpec(
        num_scalar_prefetch=2,
        grid=(batch, ng, nt),
        in_specs=[col_spec(0), col_spec(1), col_spec(2), col_spec(3),
                  pl.BlockSpec((tt, LANES), lambda b, h, t, *_: (b * nt + t, 0)),
                  cw_spec(0), cw_spec(1), cw_spec(2),
                  pl.BlockSpec((1, HEAD_DIM), lambda b, h, t, *_: (0, 0))],
        out_specs=pl.BlockSpec((tt, wide), lambda b, h, t, *_: (b * nt + t, h)),
        scratch_shapes=[pltpu.VMEM((HEAD_DIM, HEAD_DIM), F32),
                        pltpu.VMEM((3, 8, HEAD_DIM), F32),
                        pltpu.VMEM((tt, HEAD_DIM), F32),
                        pltpu.VMEM((tt, HEAD_DIM), F32)] * hps)
    return pl.pallas_call(
        _gdn_body,
        grid_spec=grid_spec,
        out_shape=jax.ShapeDtypeStruct((n, GDN_WIDTH), BF16),
        compiler_params=_cparams(("arbitrary", "arbitrary", "arbitrary")),
        name="gdn_heads",
    )(a_log.astype(F32), dt_bias.astype(F32), proj, proj, proj, proj, ba,
      conv_w, conv_w, conv_w, out_norm_w.reshape(1, HEAD_DIM))


def _moba_body(slope_ref, q_ref, k_ref, v_ref, onw_ref, o_ref, *scratch, n_blocks):
    blk = MOBA_BLOCK
    tq = 2 * blk
    nb = n_blocks
    nbp = -(-nb // 8) * 8
    hps = MOBA_HEADS_PER_STEP
    cq = pl.program_id(2)
    c0 = 2 * cq
    lane = lax.broadcasted_iota(I32, (blk, LANES), 1)
    row = lax.broadcasted_iota(I32, (blk, blk), 0)
    col = lax.broadcasted_iota(I32, (blk, blk), 1)
    rid = lax.broadcasted_iota(I32, (LANES, tq), 0)
    bid = lax.broadcasted_iota(I32, (nbp, tq), 0)
    c_row = c0 + (lax.broadcasted_iota(I32, (nbp, tq), 1) >= blk).astype(I32)
    ones_ext_t = jnp.where(jnp.logical_and(rid >= nb, rid < nb + 3), 1.0, 0.0).astype(BF16)

    def head_lanes(hh):
        return slice(hh * HEAD_DIM, (hh + 1) * HEAD_DIM)

    def keys(hh, j, nblk=1):
        return scratch[3 * hh + 1][pl.ds(pl.multiple_of(j * blk, blk), nblk * blk), :]

    def values_t(hh, j, nblk=1):
        return scratch[3 * hh + 2][:, pl.ds(pl.multiple_of(j * blk, blk), nblk * blk)]

    @pl.when(cq == 0)
    def _():
        t_in = lax.broadcasted_iota(I32, (blk, LANES), 0)
        for hh in range(hps):
            kmean_ref, kx_ref, vt_ref = scratch[3 * hh:3 * hh + 3]
            slope2 = slope_ref[pl.program_id(1) * hps + hh] * LOG2E
            kmean_ref[...] = jnp.zeros_like(kmean_ref)
            for j in range(nb):
                kj = k_ref[j * blk:(j + 1) * blk, head_lanes(hh)]
                kmean_ref[j:j + 1, :] = jnp.mean(kj.astype(F32), axis=0, keepdims=True)
                bias = slope2 * (t_in + j * blk).astype(F32)
                b_hi = bias.astype(BF16)
                r1 = bias - b_hi.astype(F32)
                b_mid = r1.astype(BF16)
                b_lo = (r1 - b_mid.astype(F32)).astype(BF16)
                ext = jnp.where(lane == j, 1.0, 0.0).astype(BF16)
                ext = jnp.where(lane == nb, b_hi, ext)
                ext = jnp.where(lane == nb + 1, b_mid, ext)
                ext = jnp.where(lane == nb + 2, b_lo, ext)
                kx_ref[j * blk:(j + 1) * blk, :] = jnp.concatenate([kj, ext], axis=1)
                vj = v_ref[j * blk:(j + 1) * blk, head_lanes(hh)].astype(F32)
                vt_ref[:, j * blk:(j + 1) * blk] = jnp.transpose(vj).astype(BF16)

    def start(s, vals_t):
        m = jnp.max(s, axis=0, keepdims=True)
        p = jnp.exp2(s - m)
        return m, jnp.sum(p, axis=0, keepdims=True), _dot(vals_t, p.astype(BF16))

    def update(carry, s, vals_t):
        m_i, l_i, acc = carry
        m_new = jnp.maximum(m_i, jnp.max(s, axis=0, keepdims=True))
        alpha = jnp.exp2(m_i - m_new)
        p = jnp.exp2(s - m_new)
        return (m_new, l_i * alpha + jnp.sum(p, axis=0, keepdims=True),
                acc * alpha + _dot(vals_t, p.astype(BF16)))

    qx = [None] * hps
    carry0 = [None] * hps

    def prologue(hh):
        q_t = jnp.transpose(q_ref[:, head_lanes(hh)].astype(F32))
        q_t16 = q_t.astype(BF16)
        yield
        qx0_t = jnp.concatenate([q_t16, ones_ext_t], axis=0)
        state = []
        for half in range(2):
            s_own = _dot(keys(hh, c0 + half), qx0_t[:, half * blk:(half + 1) * blk])
            s_own = jnp.where(row <= col, s_own, MOBA_MASK)
            yield
            state.append(start(s_own, values_t(hh, c0 + half)))
            yield
        gate_t = jnp.dot(scratch[3 * hh][...], q_t, preferred_element_type=F32, precision=HI)[:nbp]
        gm = jnp.where(bid < c_row, gate_t, NEG_INF)
        sel = jnp.zeros((nbp, tq), F32)
        for s in range(MOBA_TOPK):
            mx = jnp.max(gm, axis=0, keepdims=True)
            idx = jnp.min(jnp.where(gm == mx, bid, LANES), axis=0, keepdims=True)
            pick = bid == idx
            sel = jnp.where(jnp.logical_and(pick, s < c_row), 1.0, sel)
            gm = jnp.where(pick, -3e38, gm)
        mask_t = jnp.where(sel > 0.5, 0.0, MOBA_MASK)
        ext_t = jnp.concatenate([mask_t, jnp.zeros((LANES - nbp, tq), F32)], axis=0)
        ext_t = jnp.where(rid < nb, ext_t, jnp.where(rid < nb + 3, 1.0, 0.0))
        qx[hh] = jnp.concatenate([q_t16, ext_t.astype(BF16)], axis=0)
        yield
        s_c0 = _dot(keys(hh, c0), qx[hh][:, blk:])
        yield
        second = update(state[1], s_c0, values_t(hh, c0))
        carry0[hh] = tuple(jnp.concatenate([a, b], axis=1) for a, b in zip(state[0], second))

    for _ in itertools.zip_longest(*[prologue(hh) for hh in range(hps)]):
        pass

    def body(p, carries):
        scores = [_dot(keys(hh, 2 * p, 2), qx[hh]) for hh in range(hps)]
        return tuple(update(carries[hh], scores[hh], values_t(hh, 2 * p, 2)) for hh in range(hps))

    final = lax.fori_loop(0, cq, body, tuple(carry0))
    for hh in range(hps):
        _, l_f, acc_f = final[hh]
        o_t = acc_f / l_f
        o_t = o_t * lax.rsqrt(jnp.mean(o_t * o_t, axis=0, keepdims=True) + RMS_EPS)
        o_ref[:, head_lanes(hh)] = (jnp.transpose(o_t) * onw_ref[...]).astype(o_ref.dtype)


def moba_heads(proj, out_norm_w, batch, seq, col_off):
    n = batch * seq
    blk = MOBA_BLOCK
    nb = seq // blk
    nq = nb // 2
    hps = MOBA_HEADS_PER_STEP
    ng = MOBA_HEADS // hps
    wide = hps * HEAD_DIM
    sec0 = col_off // hps
    slopes = jnp.exp2(-8.0 * jnp.arange(1, MOBA_HEADS + 1, dtype=F32) / MOBA_HEADS)
    grid_spec = pltpu.PrefetchScalarGridSpec(
        num_scalar_prefetch=1,
        grid=(batch, ng, nq),
        in_specs=[pl.BlockSpec((2 * blk, wide), lambda b, h, c, *_: (b * nq + c, sec0 + h)),
                  pl.BlockSpec((seq, wide), lambda b, h, c, *_: (b, sec0 + ng + h)),
                  pl.BlockSpec((seq, wide), lambda b, h, c, *_: (b, sec0 + 2 * ng + h)),
                  pl.BlockSpec((1, HEAD_DIM), lambda b, h, c, *_: (0, 0))],
        out_specs=pl.BlockSpec((2 * blk, wide), lambda b, h, c, *_: (b * nq + c, h)),
        scratch_shapes=[pltpu.VMEM((LANES, HEAD_DIM), F32),
                        pltpu.VMEM((nb * blk, HEAD_DIM + LANES), BF16),
                        pltpu.VMEM((HEAD_DIM, nb * blk), BF16)] * hps)
    assert nb % 2 == 0 and nb + 3 <= LANES and col_off % hps == 0
    return pl.pallas_call(
        functools.partial(_moba_body, n_blocks=nb),
        grid_spec=grid_spec,
        out_shape=jax.ShapeDtypeStruct((n, MOBA_WIDTH), BF16),
        compiler_params=_cparams(("arbitrary", "arbitrary", "arbitrary")),
        name="moba_heads",
    )(slopes, proj, proj, proj, out_norm_w.reshape(1, HEAD_DIM))


def _out_proj_body(x_ref, og_ref, om_ref, wg_ref, wm_ref, nw_ref, wrh_ref, wrl_ref, rb_ref,
                   x2_ref, hp_ref, gt_ref, id_ref):
    tm = x_ref.shape[0]
    x2 = x_ref[...] + _dot(og_ref[...], wg_ref[...]) + _dot(om_ref[...], wm_ref[...])
    x2_ref[...] = x2
    ms = jnp.mean(x2 * x2, axis=-1, keepdims=True)
    h2 = x2 * lax.rsqrt(ms + RMS_EPS) * nw_ref[...]
    h_hi = h2.astype(BF16)
    hp_ref[...] = h2
    h_lo = (h2 - h_hi.astype(F32)).astype(BF16)
    lg = _dot(h_hi, wrh_ref[...]) + _dot(h_hi, wrl_ref[...]) + _dot(h_lo, wrh_ref[...]) + rb_ref[...]

    lane = lax.broadcasted_iota(I32, (tm, LANES), 1)
    is_g = lane < N_GROUPS
    mg = jnp.max(jnp.where(is_g, lg, NEG_INF), axis=-1, keepdims=True)
    g_idx = jnp.min(jnp.where(jnp.logical_and(is_g, lg == mg), lane, LANES), axis=-1, keepdims=True)
    sum_g = jnp.sum(jnp.where(is_g, jnp.exp(lg - mg), 0.0), axis=-1, keepdims=True)
    p_top_g = 1.0 / sum_g
    lo = N_GROUPS + g_idx * EXPERTS_PER_GROUP
    in_grp = jnp.logical_and(lane >= lo, lane < lo + EXPERTS_PER_GROUP)
    m1 = jnp.max(jnp.where(in_grp, lg, NEG_INF), axis=-1, keepdims=True)
    i1 = jnp.min(jnp.where(jnp.logical_and(in_grp, lg == m1), lane, LANES), axis=-1, keepdims=True)
    rest = jnp.logical_and(in_grp, lane != i1)
    m2 = jnp.max(jnp.where(rest, lg, NEG_INF), axis=-1, keepdims=True)
    i2 = jnp.min(jnp.where(jnp.logical_and(rest, lg == m2), lane, LANES), axis=-1, keepdims=True)
    e2 = jnp.exp(m2 - m1)
    gate1 = p_top_g / (1.0 + e2)
    gate2 = p_top_g * e2 / (1.0 + e2)
    gt_ref[...] = jnp.where(lane == 0, gate1, jnp.where(lane == 1, gate2, 0.0))
    id_ref[...] = jnp.where(lane == 0, i1 - N_GROUPS, jnp.where(lane == 1, i2 - N_GROUPS, 0))


def out_proj(x, og, om, w_g, w_m, nw, wr_hi, wr_lo, r_bias, tm):
    n, d = x.shape
    const = lambda i: (0, 0)
    rows = lambda i: (i, 0)
    return pl.pallas_call(
        _out_proj_body,
        grid=(n // tm,),
        in_specs=[pl.BlockSpec((tm, d), rows),
                  pl.BlockSpec((tm, GDN_WIDTH), rows),
                  pl.BlockSpec((tm, MOBA_WIDTH), rows),
                  pl.BlockSpec((GDN_WIDTH, d), const),
                  pl.BlockSpec((MOBA_WIDTH, d), const),
                  pl.BlockSpec((1, d), const),
                  pl.BlockSpec((d, LANES), const),
                  pl.BlockSpec((d, LANES), const),
                  pl.BlockSpec((1, LANES), const)],
        out_specs=[pl.BlockSpec((tm, d), rows),
                   pl.BlockSpec((tm, d), rows),
                   pl.BlockSpec((tm, LANES), rows),
                   pl.BlockSpec((tm, LANES), rows)],
        out_shape=[jax.ShapeDtypeStruct((n, d), F32),
                   jax.ShapeDtypeStruct((n, d), F32),
                   jax.ShapeDtypeStruct((n, LANES), F32),
                   jax.ShapeDtypeStruct((n, LANES), I32)],
        compiler_params=_cparams(("arbitrary",)),
        name="out_proj_router",
    )(x, og, om, w_g, w_m, nw.reshape(1, d), wr_hi, wr_lo, r_bias)


def _zero_rows(start_not_wait, zero_buf, xb_hbm, sem, off, n):
    z = ZERO_ROWS

    def go(src, dst):
        cp = pltpu.make_async_copy(src, dst, sem)
        if start_not_wait:
            cp.start()
        else:
            cp.wait()

    head = jnp.bitwise_and(-off, SUBLANES - 1)
    for r in range(SUBLANES - 1):
        @pl.when(r < head)
        def _(r=r):
            go(zero_buf.at[pl.ds(0, 1)], xb_hbm.at[pl.ds(off + r, 1)])

    off = off + head
    n = n - head
    n_full = n // z

    def chunk(c, carry):
        go(zero_buf, xb_hbm.at[pl.ds(pl.multiple_of(off + c * z, SUBLANES), z)])
        return carry

    lax.fori_loop(0, n_full, chunk, 0)
    off = off + n_full * z
    rem = n - n_full * z
    bit = z // 2
    while bit >= SUBLANES:
        @pl.when((rem & bit) != 0)
        def _(off=off, bit=bit):
            go(zero_buf.at[pl.ds(0, bit)], xb_hbm.at[pl.ds(pl.multiple_of(off, SUBLANES), bit)])
        off = off + (rem & bit)
        bit //= 2


def _dispatch_body(pos0_ref, pos1_ref, padfrom_ref, padn_ref, hp_ref, xb_hbm, zero_buf, sem, zsem):
    ts = DISPATCH_ROWS
    base = pl.program_id(0) * ts

    @pl.when(pl.program_id(0) == 0)
    def _():
        zero_buf[...] = jnp.zeros_like(zero_buf)
        for start_not_wait in (True, False):
            def per_range(e, carry, start_not_wait=start_not_wait):
                _zero_rows(start_not_wait, zero_buf, xb_hbm, zsem, padfrom_ref[e], padn_ref[e])
                return carry
            lax.fori_loop(0, N_EXPERTS + 1, per_range, 0)

    def body(r, carry):
        src = hp_ref.at[pl.ds(r, 1)]
        pltpu.make_async_copy(src, xb_hbm.at[pl.ds(pos0_ref[base + r], 1)], sem).start()
        pltpu.make_async_copy(src, xb_hbm.at[pl.ds(pos1_ref[base + r], 1)], sem).start(priority=1)
        return carry

    lax.fori_loop(0, ts, body, 0, unroll=8)
    for _ in range(TOPK_IN_GROUP):
        pltpu.make_async_copy(hp_ref, xb_hbm.at[pl.ds(0, ts)], sem).wait()


def moe_dispatch(hp, pos0, pos1, pad_from, pad_n, n_rows):
    n, dh = hp.shape
    ts = DISPATCH_ROWS
    grid_spec = pltpu.PrefetchScalarGridSpec(
        num_scalar_prefetch=4,
        grid=(n // ts,),
        in_specs=[pl.BlockSpec((ts, dh), lambda i, *_: (i, 0))],
        out_specs=pl.BlockSpec(memory_space=pl.ANY),
        scratch_shapes=[pltpu.VMEM((ZERO_ROWS, dh), F32),
                        pltpu.SemaphoreType.DMA(()), pltpu.SemaphoreType.DMA(())])
    return pl.pallas_call(
        _dispatch_body,
        grid_spec=grid_spec,
        out_shape=jax.ShapeDtypeStruct((n_rows, dh), F32),
        compiler_params=_cparams(("arbitrary",)),
        name="moe_dispatch",
    )(pos0, pos1, pad_from, pad_n, hp)


def _moe_body(blk_e_ref, nblk_ref, first_ref, slot_ref, next_e_ref, xb_ref, wg_hbm, wu_hbm, wd_hbm, y_ref,
              wg_buf, wu_buf, wd_buf, wg16, wu16, wd16, sem):
    i = pl.program_id(0)
    e = blk_e_ref[i]
    slot = slot_ref[i]

    def weight_copies(expert, s):
        return (pltpu.make_async_copy(wg_hbm.at[expert], wg_buf.at[s], sem.at[s]),
                pltpu.make_async_copy(wu_hbm.at[expert], wu_buf.at[s], sem.at[s]),
                pltpu.make_async_copy(wd_hbm.at[expert], wd_buf.at[s], sem.at[s]))

    @pl.when(i == 0)
    def _():
        for cp in weight_copies(e, slot):
            cp.start()

    @pl.when(first_ref[i] == 1)
    def _():
        for cp in weight_copies(e, slot):
            cp.wait()

        @pl.when(next_e_ref[i] >= 0)
        def _():
            for cp in weight_copies(next_e_ref[i], 1 - slot):
                cp.start()

        wg16[...] = wg_buf[slot].astype(BF16)
        wu16[...] = wu_buf[slot].astype(BF16)
        wd16[...] = wd_buf[slot].astype(BF16)

    @pl.when(i < nblk_ref[0])
    def _():
        xb = xb_ref[...].astype(BF16)
        g = _dot(xb, wg16[...])
        u = _dot(xb, wu16[...])
        hm = (g * _sigmoid(g) * u).astype(BF16)
        y_ref[...] = _dot(hm, wd16[...])

    @pl.when(i >= nblk_ref[0])
    def _():
        y_ref[...] = jnp.zeros_like(y_ref)


def moe_experts(xb, blk_expert, n_used, first, slot, next_e, w_gate, w_up, w_down):
    rb = MOE_ROWS
    n_rows, d = xb.shape
    de = w_gate.shape[-1]
    grid_spec = pltpu.PrefetchScalarGridSpec(
        num_scalar_prefetch=5,
        grid=(n_rows // rb,),
        in_specs=[pl.BlockSpec((rb, d), lambda i, *_: (i, 0)),
                  pl.BlockSpec(memory_space=pl.ANY),
                  pl.BlockSpec(memory_space=pl.ANY),
                  pl.BlockSpec(memory_space=pl.ANY)],
        out_specs=pl.BlockSpec((rb, d), lambda i, *_: (i, 0)),
        scratch_shapes=[pltpu.VMEM((2, d, de), F32), pltpu.VMEM((2, d, de), F32), pltpu.VMEM((2, de, d), F32),
                        pltpu.VMEM((d, de), BF16), pltpu.VMEM((d, de), BF16), pltpu.VMEM((de, d), BF16),
                        pltpu.SemaphoreType.DMA((2,))])
    return pl.pallas_call(
        _moe_body,
        grid_spec=grid_spec,
        out_shape=jax.ShapeDtypeStruct((n_rows, d), F32),
        compiler_params=_cparams(("arbitrary",)),
        name="moe_experts",
    )(blk_expert, n_used, first, slot, next_e, xb, w_gate, w_up, w_down)


def _combine_body(pos0_ref, pos1_ref, y_hbm, x2_ref, gt_ref, nw_ref, o_ref, buf, sem):
    tf = COMBINE_ROWS
    i = pl.program_id(0)
    n_steps = pl.num_programs(0)
    slot = lax.rem(i, 2)

    def issue(step, sl):
        def body(r, carry):
            t = step * tf + r
            pltpu.make_async_copy(y_hbm.at[pl.ds(pos0_ref[t], 1)], buf.at[sl, 0, pl.ds(r, 1)],
                                  sem.at[sl]).start()
            pltpu.make_async_copy(y_hbm.at[pl.ds(pos1_ref[t], 1)], buf.at[sl, 1, pl.ds(r, 1)],
                                  sem.at[sl]).start(priority=1)
            return carry
        lax.fori_loop(0, tf, body, 0, unroll=8)

    @pl.when(i == 0)
    def _():
        issue(0, 0)

    @pl.when(i + 1 < n_steps)
    def _():
        issue(i + 1, 1 - slot)

    for kk in range(TOPK_IN_GROUP):
        pltpu.make_async_copy(y_hbm.at[pl.ds(0, tf)], buf.at[slot, kk], sem.at[slot]).wait()

    gt = gt_ref[...]
    xo = x2_ref[...] + gt[:, 0:1] * buf[slot, 0] + gt[:, 1:2] * buf[slot, 1]
    ms = jnp.mean(xo * xo, axis=-1, keepdims=True)
    o_ref[...] = xo * lax.rsqrt(ms + RMS_EPS) * nw_ref[...]


def moe_combine(yb, pos0, pos1, x2, gates, nw):
    n, d = x2.shape
    tf = COMBINE_ROWS
    grid_spec = pltpu.PrefetchScalarGridSpec(
        num_scalar_prefetch=2,
        grid=(n // tf,),
        in_specs=[pl.BlockSpec(memory_space=pl.ANY),
                  pl.BlockSpec((tf, d), lambda i, *_: (i, 0)),
                  pl.BlockSpec((tf, LANES), lambda i, *_: (i, 0)),
                  pl.BlockSpec((1, d), lambda i, *_: (0, 0))],
        out_specs=pl.BlockSpec((tf, d), lambda i, *_: (i, 0)),
        scratch_shapes=[pltpu.VMEM((2, 2, tf, d), F32), pltpu.SemaphoreType.DMA((2,))])
    return pl.pallas_call(
        _combine_body,
        grid_spec=grid_spec,
        out_shape=jax.ShapeDtypeStruct((n, d), F32),
        compiler_params=_cparams(("arbitrary",)),
        name="moe_combine",
    )(pos0, pos1, yb, x2, gates, nw.reshape(1, d))


def _dispatch_plan(expert_id):
    n_tok, k = expert_id.shape
    rb = MOE_ROWS
    n_assign = n_tok * k
    e_flat = expert_id.reshape(-1)
    onehot = (e_flat[:, None] == jnp.arange(N_EXPERTS, dtype=I32)[None, :]).astype(I32)
    csum = jnp.cumsum(onehot, axis=0)
    counts = csum[-1]
    padded = (counts + rb - 1) // rb * rb
    pad_end = jnp.cumsum(padded)
    pad_start = pad_end - padded
    pos = jnp.sum(onehot * (pad_start[None, :] + csum - 1), axis=1).astype(I32).reshape(n_tok, k)
    n_rb = -(-n_assign // rb) + N_EXPERTS
    blk_start = jnp.arange(n_rb, dtype=I32) * rb
    blk_expert = jnp.minimum(jnp.sum((pad_end[None, :] <= blk_start[:, None]).astype(I32), axis=1),
                             N_EXPERTS - 1).astype(I32)
    n_used = (pad_end[-1] // rb).astype(I32).reshape(1)
    pad_from = jnp.concatenate([pad_start + counts, pad_end[-1:]]).astype(I32)
    pad_n = jnp.concatenate([padded - counts, n_rb * rb - pad_end[-1:]]).astype(I32)
    blk = jnp.arange(n_rb, dtype=I32)
    prev_expert = jnp.concatenate([jnp.full((1,), -1, I32), blk_expert[:-1]])
    first = jnp.logical_and(blk < n_used[0], blk_expert != prev_expert).astype(I32)
    slot = jnp.bitwise_and(jnp.cumsum(first) - 1, 1).astype(I32)
    ids = jnp.arange(N_EXPERTS, dtype=I32)
    later = jnp.logical_and(ids[None, :] > ids[:, None], counts[None, :] > 0)
    next_nonempty = jnp.min(jnp.where(later, ids[None, :], N_EXPERTS), axis=1)
    next_nonempty = jnp.where(next_nonempty < N_EXPERTS, next_nonempty, -1).astype(I32)
    next_e = next_nonempty[blk_expert]
    return pos, blk_expert, n_used, pad_from, pad_n, first, slot, next_e, n_rb * rb


def kernel(x, norm_mix_w, w_in, gdn_conv_w, gdn_A_log, gdn_dt_bias, gdn_out_norm_w, moba_out_norm_w, w_out, norm_ffn_w, w_router_group, b_router_group, w_router_expert, b_router_expert, w_expert_gate, w_expert_up, w_expert_down, norm_final_w):
    batch, seq, d = x.shape
    n = batch * seq
    assert w_in.shape[0] == 1, "the final norm is fused into the last layer's combine; one layer supported"
    l = 0
    xf = x.reshape(n, d).astype(F32)
    gw = 4 * GDN_WIDTH
    w_l = w_in[l]
    mq0 = gw + 2 * GDN_HEADS
    col_scale = jnp.where(jnp.arange(3 * MOBA_WIDTH) < MOBA_WIDTH, HEAD_DIM ** -0.5 * LOG2E, 1.0).astype(F32)
    w_gdn = w_l[:, :gw].astype(BF16)
    w_moba = (w_l[:, mq0:] * col_scale[None, :]).astype(BF16)
    w_ba = jnp.pad(w_l[:, gw:gw + 2 * GDN_HEADS], ((0, 0), (0, LANES - 2 * GDN_HEADS))).astype(BF16)
    proj_g = norm_matmul(xf, norm_mix_w[l], w_gdn, BF16, 512, gw // 2)
    proj_m = norm_matmul(xf, norm_mix_w[l], w_moba, BF16, 512, 3 * MOBA_WIDTH // 2)
    ba = norm_matmul(xf, norm_mix_w[l], w_ba, F32, 512, LANES)
    og = gdn_heads(proj_g, ba, gdn_conv_w[l].astype(F32), gdn_A_log[l], gdn_dt_bias[l],
                   gdn_out_norm_w[l].astype(F32), batch, seq)
    om = moba_heads(proj_m, moba_out_norm_w[l].astype(F32), batch, seq, 0)

    w_o = w_out[l].astype(BF16)
    n_r = N_GROUPS + N_EXPERTS
    w_router = jnp.pad(jnp.concatenate([w_router_group[l], w_router_expert[l]], axis=1).astype(F32),
                       ((0, 0), (0, LANES - n_r)))
    wr_hi = w_router.astype(BF16)
    wr_lo = (w_router - wr_hi.astype(F32)).astype(BF16)
    r_bias = jnp.pad(jnp.concatenate([b_router_group[l], b_router_expert[l]]).astype(F32),
                     (0, LANES - n_r)).reshape(1, LANES)
    x2, hp, gates, ids = out_proj(xf, og, om, w_o[:GDN_WIDTH], w_o[GDN_WIDTH:], norm_ffn_w[l].astype(F32),
                                  wr_hi, wr_lo, r_bias, 512)

    pos, blk_expert, n_used, pad_from, pad_n, first, slot, next_e, n_rows = _dispatch_plan(
        ids[:, :TOPK_IN_GROUP])
    pos0, pos1 = pos[:, 0], pos[:, 1]
    xb = moe_dispatch(hp, pos0, pos1, pad_from, pad_n, n_rows)
    yb = moe_experts(xb, blk_expert, n_used, first, slot, next_e, w_expert_gate[l].astype(F32),
                     w_expert_up[l].astype(F32), w_expert_down[l].astype(F32))
    out = moe_combine(yb, pos0, pos1, x2, gates, norm_final_w.astype(F32))
    return out.reshape(batch, seq, d).astype(x.dtype)
```

```python
import functools
import itertools

import jax
import jax.numpy as jnp
from jax import lax
from jax.experimental import pallas as pl
from jax.experimental.pallas import tpu as pltpu

F32 = jnp.float32
BF16 = jnp.bfloat16
U32 = jnp.uint32
I32 = jnp.int32

HEAD_DIM = 128
GDN_HEADS = 8
MOBA_HEADS = 8
GDN_WIDTH = GDN_HEADS * HEAD_DIM
MOBA_WIDTH = MOBA_HEADS * HEAD_DIM
CONV_WIDTH = 4
GDN_CHUNK = 64
MOBA_BLOCK = 256
MOBA_TOPK = 3
N_GROUPS = 4
EXPERTS_PER_GROUP = 8
N_EXPERTS = N_GROUPS * EXPERTS_PER_GROUP
TOPK_IN_GROUP = 2
RMS_EPS = 1e-6
NEG_INF = -1e30
MOBA_MASK = -(2.0 ** 99)
LOG2E = 1.4426950408889634
LANES = 128
VMEM_LIMIT = 56 * 1024 * 1024

IN_PROJ_ROWS = 1024
GDN_TILE = 256
GDN_HEADS_PER_STEP = 8
MOBA_HEADS_PER_STEP = 4
MOE_ROWS = 256
COMBINE_ROWS = 128
HI = lax.Precision.HIGHEST


def _cparams(sem, **kw):
    return pltpu.CompilerParams(dimension_semantics=sem, vmem_limit_bytes=VMEM_LIMIT, **kw)


def _dot(a, b):
    return jnp.dot(a, b, preferred_element_type=F32)


def _dot_nt(a, b, precision=None):
    return lax.dot_general(a, b, (((1,), (1,)), ((), ())), preferred_element_type=F32,
                           precision=precision)


def _sigmoid(x):
    return 1.0 / (1.0 + jnp.exp(-x))


def _norm_matmul_body(x_ref, nw_ref, w_ref, *rest):
    x = x_ref[...]
    ms = jnp.mean(x * x, axis=-1, keepdims=True)
    h = (x * lax.rsqrt(ms + RMS_EPS) * nw_ref[...]).astype(BF16)
    if len(rest) == 1:
        rest[0][...] = _dot(h, w_ref[...]).astype(rest[0].dtype)
    else:
        ws_ref, o_ref, os_ref = rest
        o_ref[...] = _dot(h, w_ref[...]).astype(o_ref.dtype)
        os_ref[0] = _dot(h, ws_ref[...])


def norm_matmul(x, nw, w, out_dtype, tm, tn, w_small=None):
    n, d = x.shape
    width = w.shape[1]
    in_specs = [pl.BlockSpec((tm, d), lambda j, i: (i, 0)),
                pl.BlockSpec((1, d), lambda j, i: (0, 0)),
                pl.BlockSpec((d, tn), lambda j, i: (0, j))]
    out_specs = pl.BlockSpec((tm, tn), lambda j, i: (i, j))
    out_shape = jax.ShapeDtypeStruct((n, width), out_dtype)
    operands = (x, nw.reshape(1, d), w)
    if w_small is not None:
        ws = w_small.shape[1]
        in_specs.append(pl.BlockSpec((d, ws), lambda j, i: (0, 0)))
        out_specs = [out_specs, pl.BlockSpec((1, tm, ws), lambda j, i: (j, i, 0))]
        out_shape = [out_shape, jax.ShapeDtypeStruct((width // tn, n, ws), F32)]
        operands = operands + (w_small,)
    outs = pl.pallas_call(
        _norm_matmul_body,
        grid=(width // tn, n // tm),
        in_specs=in_specs,
        out_specs=out_specs,
        out_shape=out_shape,
        compiler_params=_cparams(("arbitrary", "arbitrary")),
        name="norm_in_proj",
    )(*operands)
    if w_small is None:
        return outs
    return outs[0], outs[1][0]


def _gdn_body(alog_ref, dtb_ref, q_ref, k_ref, v_ref, z_ref, ba_ref, cwq_ref, cwk_ref, cwv_ref,
              onw_ref, o_ref, *scratch):
    tt = GDN_TILE
    c = GDN_CHUNK
    t = pl.program_id(2)

    @pl.when(t == 0)
    def _():
        for hh in range(GDN_HEADS_PER_STEP):
            s_ref, prev_ref, vn_ref, _ = scratch[4 * hh:4 * hh + 4]
            s_ref[...] = jnp.zeros_like(s_ref)
            prev_ref[...] = jnp.zeros_like(prev_ref)
            vn_ref[...] = jnp.zeros_like(vn_ref)

    row_8 = lax.broadcasted_iota(jnp.int32, (8, LANES), 0)
    lane_t = lax.broadcasted_iota(jnp.int32, (tt, LANES), 1)
    row = lax.broadcasted_iota(jnp.int32, (tt, tt), 0)
    col = lax.broadcasted_iota(jnp.int32, (tt, tt), 1)
    same = (row // c) == (col // c)
    causal = jnp.logical_and(same, col <= row)
    strict = jnp.logical_and(same, col < row)
    causal16 = jnp.where(causal, 1.0, 0.0).astype(BF16)
    same16 = jnp.where(same, 1.0, 0.0).astype(BF16)
    eye = jnp.where(row == col, 1.0, 0.0)
    colk = lax.broadcasted_iota(jnp.int32, (HEAD_DIM, tt), 1)
    ba = ba_ref[...]

    heads = [_gdn_one_head(hh, pl.program_id(1) * GDN_HEADS_PER_STEP + hh, alog_ref, dtb_ref, q_ref, k_ref,
                           v_ref, z_ref, ba, cwq_ref, cwk_ref, cwv_ref, onw_ref, o_ref,
                           *scratch[4 * hh:4 * hh + 4],
                           row_8, lane_t, causal, strict, causal16, same16, eye, colk)
             for hh in range(GDN_HEADS_PER_STEP)]
    for _ in itertools.zip_longest(*heads):
        pass


def _gdn_one_head(hh, h, alog_ref, dtb_ref, q_ref, k_ref, v_ref, z_ref, ba, cwq_ref, cwk_ref, cwv_ref,
                  onw_ref, o_ref, s_ref, prev_ref, vn_ref, oacc_ref,
                  row_8, lane_t, causal, strict, causal16, same16, eye, colk):
    tt = GDN_TILE
    c = GDN_CHUNK
    lanes = slice(hh * HEAD_DIM, (hh + 1) * HEAD_DIM)

    def conv_silu(x_ref, cw_ref, idx):
        x = x_ref[:, lanes].astype(F32)
        p = prev_ref[idx]
        w = cw_ref[:, lanes]
        acc = x * w[CONV_WIDTH - 1:CONV_WIDTH, :]
        for s in range(1, CONV_WIDTH):
            xs = pltpu.roll(x, s, axis=0)
            head = jnp.where(row_8 < s, pltpu.roll(p, s, axis=0), xs[:8])
            xs = jnp.concatenate([head, xs[8:]], axis=0)
            acc = acc + xs * w[CONV_WIDTH - 1 - s:CONV_WIDTH - s, :]
        prev_ref[idx] = x[tt - 8:tt, :]
        return acc * _sigmoid(acc)

    q = conv_silu(q_ref, cwq_ref, 0)
    k = conv_silu(k_ref, cwk_ref, 1)
    v = conv_silu(v_ref, cwv_ref, 2)
    q = q * lax.rsqrt(jnp.sum(q * q, axis=-1, keepdims=True) + 1e-6) * (HEAD_DIM ** -0.5)
    k = k * lax.rsqrt(jnp.sum(k * k, axis=-1, keepdims=True) + 1e-6)
    yield

    b_col = jnp.sum(jnp.where(lane_t == h, ba, 0.0), axis=-1, keepdims=True)
    a_col = jnp.sum(jnp.where(lane_t == h + GDN_HEADS, ba, 0.0), axis=-1, keepdims=True)
    beta = _sigmoid(b_col)
    xa = a_col + dtb_ref[h]
    softplus = jnp.maximum(xa, 0.0) + jnp.log(1.0 + jnp.exp(-jnp.abs(xa)))
    g = -jnp.exp(jnp.full((1, 1), alog_ref[h], F32)) * softplus

    g_hi = g.astype(BF16).astype(F32)
    g_mid = (g - g_hi).astype(BF16).astype(F32)
    g_lo = g - g_hi - g_mid
    g3 = jnp.where(lane_t == 0, g_hi, jnp.where(lane_t == 1, g_mid, jnp.where(lane_t == 2, g_lo, 0.0)))
    g3 = g3.astype(BF16)

    def three(r):
        return r[:, 0:1] + r[:, 1:2] + r[:, 2:3]

    gc_col = three(_dot(causal16, g3))
    gc_b = jnp.broadcast_to(gc_col, (tt, LANES))
    glast_b = jnp.broadcast_to(three(_dot(same16, g3)), (tt, LANES))
    gc_row = jnp.transpose(gc_b)[0:1, :]
    decay = jnp.exp(jnp.where(causal, gc_col - gc_row, NEG_INF))
    yield

    kb = k * beta
    k16 = k.astype(BF16)
    kk = _dot_nt(kb.astype(BF16), k16)
    lmat = jnp.where(strict, kk * decay, 0.0)
    yield
    attn = _dot_nt(q.astype(BF16), k16) * decay

    tinv = eye - lmat
    m16 = lmat.astype(BF16)
    for _ in range(5):
        m16 = _dot(m16, m16).astype(BF16)
        yield
        tinv = tinv + _dot(tinv.astype(BF16), m16)
        yield

    egc = jnp.exp(gc_b)
    rhs = jnp.concatenate([v * beta, kb * egc], axis=-1)
    sol = _dot(tinv.astype(BF16), rhs.astype(BF16))
    yield
    u = sol[:, :HEAD_DIM]
    w16 = sol[:, HEAD_DIM:].astype(BF16)
    qd16 = (q * egc).astype(BF16)
    kd = k * jnp.exp(glast_b - gc_b)
    kdt = jnp.transpose(kd)
    gtot_b = jnp.exp(glast_b)
    attn16 = attn.astype(BF16)

    s = s_ref[...]
    for n in range(tt // c):
        sl = slice(n * c, (n + 1) * c)
        s16 = s.astype(BF16)
        v_new = u[sl] - _dot(w16[sl], s16)
        vn_ref[sl, :] = v_new
        yield
        vn16 = vn_ref[...].astype(BF16)
        oacc_ref[sl, :] = _dot(qd16[sl], s16) + _dot(attn16[sl], vn16)
        kdt_n = jnp.where((colk // c) == n, kdt, 0.0).astype(BF16)
        s = s * gtot_b[n * c:n * c + 1, :] + _dot(kdt_n, vn16)
        yield
    s_ref[...] = s

    o = oacc_ref[...]
    o = o * lax.rsqrt(jnp.mean(o * o, axis=-1, keepdims=True) + RMS_EPS) * onw_ref[...]
    z = z_ref[:, lanes].astype(F32)
    o_ref[:, lanes] = (o * (z * _sigmoid(z))).astype(o_ref.dtype)


def gdn_heads(proj, ba, conv_w, a_log, dt_bias, out_norm_w, batch, seq):
    n = batch * seq
    tt = GDN_TILE
    nt = seq // tt
    hps = GDN_HEADS_PER_STEP
    ng = GDN_HEADS // hps
    wide = hps * HEAD_DIM

    def col_spec(section):
        return pl.BlockSpec((tt, wide), lambda b, h, t, *_: (b * nt + t, section * ng + h))

    def cw_spec(section):
        return pl.BlockSpec((CONV_WIDTH, wide), lambda b, h, t, *_: (0, section * ng + h))

    grid_spec = pltpu.PrefetchScalarGridSpec(
        num_scalar_prefetch=2,
        grid=(batch, ng, nt),
        in_specs=[col_spec(0), col_spec(1), col_spec(2), col_spec(3),
                  pl.BlockSpec((tt, LANES), lambda b, h, t, *_: (b * nt + t, 0)),
                  cw_spec(0), cw_spec(1), cw_spec(2),
                  pl.BlockSpec((1, HEAD_DIM), lambda b, h, t, *_: (0, 0))],
        out_specs=pl.BlockSpec((tt, wide), lambda b, h, t, *_: (b * nt + t, h)),
        scratch_shapes=[pltpu.VMEM((HEAD_DIM, HEAD_DIM), F32),
                        pltpu.VMEM((3, 8, HEAD_DIM), F32),
                        pltpu.VMEM((tt, HEAD_DIM), F32),
                        pltpu.VMEM((tt, HEAD_DIM), F32)] * hps)
    return pl.pallas_call(
        _gdn_body,
        grid_spec=grid_spec,
        out_shape=jax.ShapeDtypeStruct((n, GDN_WIDTH), BF16),
        compiler_params=_cparams(("arbitrary", "arbitrary", "arbitrary")),
        name="gdn_heads",
    )(a_log.astype(F32), dt_bias.astype(F32), proj, proj, proj, proj, ba,
      conv_w, conv_w, conv_w, out_norm_w.reshape(1, HEAD_DIM))


def _moba_body(slope_ref, q_ref, k_ref, v_ref, onw_ref, o_ref, *scratch, n_blocks):
    blk = MOBA_BLOCK
    tq = 2 * blk
    nb = n_blocks
    nbp = -(-nb // 8) * 8
    hps = MOBA_HEADS_PER_STEP
    cq = pl.program_id(2)
    c0 = 2 * cq
    lane = lax.broadcasted_iota(I32, (blk, LANES), 1)
    row = lax.broadcasted_iota(I32, (blk, blk), 0)
    col = lax.broadcasted_iota(I32, (blk, blk), 1)
    rid = lax.broadcasted_iota(I32, (LANES, tq), 0)
    bid = lax.broadcasted_iota(I32, (nbp, tq), 0)
    c_row = c0 + (lax.broadcasted_iota(I32, (nbp, tq), 1) >= blk).astype(I32)
    ones_ext_t = jnp.where(jnp.logical_and(rid >= nb, rid < nb + 3), 1.0, 0.0).astype(BF16)

    def head_lanes(hh):
        return slice(hh * HEAD_DIM, (hh + 1) * HEAD_DIM)

    def keys(hh, j, nblk=1):
        return scratch[3 * hh + 1][pl.ds(pl.multiple_of(j * blk, blk), nblk * blk), :]

    def values_t(hh, j, nblk=1):
        return scratch[3 * hh + 2][:, pl.ds(pl.multiple_of(j * blk, blk), nblk * blk)]

    @pl.when(cq == 0)
    def _():
        t_in = lax.broadcasted_iota(I32, (blk, LANES), 0)
        for hh in range(hps):
            kmean_ref, kx_ref, vt_ref = scratch[3 * hh:3 * hh + 3]
            slope2 = slope_ref[pl.program_id(1) * hps + hh] * LOG2E
            kmean_ref[...] = jnp.zeros_like(kmean_ref)
            for j in range(nb):
                kj = k_ref[j * blk:(j + 1) * blk, head_lanes(hh)]
                kmean_ref[j:j + 1, :] = jnp.mean(kj.astype(F32), axis=0, keepdims=True)
                bias = slope2 * (t_in + j * blk).astype(F32)
                b_hi = bias.astype(BF16)
                r1 = bias - b_hi.astype(F32)
                b_mid = r1.astype(BF16)
                b_lo = (r1 - b_mid.astype(F32)).astype(BF16)
                ext = jnp.where(lane == j, 1.0, 0.0).astype(BF16)
                ext = jnp.where(lane == nb, b_hi, ext)
                ext = jnp.where(lane == nb + 1, b_mid, ext)
                ext = jnp.where(lane == nb + 2, b_lo, ext)
                kx_ref[j * blk:(j + 1) * blk, :] = jnp.concatenate([kj, ext], axis=1)
                vj = v_ref[j * blk:(j + 1) * blk, head_lanes(hh)].astype(F32)
                vt_ref[:, j * blk:(j + 1) * blk] = jnp.transpose(vj).astype(BF16)

    def start(s, vals_t):
        m = jnp.max(s, axis=0, keepdims=True)
        p = jnp.exp2(s - m)
        return m, jnp.sum(p, axis=0, keepdims=True), _dot(vals_t, p.astype(BF16))

    def update(carry, s, vals_t):
        m_i, l_i, acc = carry
        m_new = jnp.maximum(m_i, jnp.max(s, axis=0, keepdims=True))
        alpha = jnp.exp2(m_i - m_new)
        p = jnp.exp2(s - m_new)
        return (m_new, l_i * alpha + jnp.sum(p, axis=0, keepdims=True),
                acc * alpha + _dot(vals_t, p.astype(BF16)))

    qx = [None] * hps
    carry0 = [None] * hps

    def prologue(hh):
        q_t = jnp.transpose(q_ref[:, head_lanes(hh)].astype(F32))
        q_t16 = q_t.astype(BF16)
        yield
        qx0_t = jnp.concatenate([q_t16, ones_ext_t], axis=0)
        state = []
        for half in range(2):
            s_own = _dot(keys(hh, c0 + half), qx0_t[:, half * blk:(half + 1) * blk])
            s_own = jnp.where(row <= col, s_own, MOBA_MASK)
            yield
            state.append(start(s_own, values_t(hh, c0 + half)))
            yield
        gate_t = jnp.dot(scratch[3 * hh][...], q_t, preferred_element_type=F32, precision=HI)[:nbp]
        gm = jnp.where(bid < c_row, gate_t, NEG_INF)
        sel = jnp.zeros((nbp, tq), F32)
        for s in range(MOBA_TOPK):
            mx = jnp.max(gm, axis=0, keepdims=True)
            idx = jnp.min(jnp.where(gm == mx, bid, LANES), axis=0, keepdims=True)
            pick = bid == idx
            sel = jnp.where(jnp.logical_and(pick, s < c_row), 1.0, sel)
            gm = jnp.where(pick, -3e38, gm)
        mask_t = jnp.where(sel > 0.5, 0.0, MOBA_MASK)
        ext_t = jnp.concatenate([mask_t, jnp.zeros((LANES - nbp, tq), F32)], axis=0)
        ext_t = jnp.where(rid < nb, ext_t, jnp.where(rid < nb + 3, 1.0, 0.0))
        qx[hh] = jnp.concatenate([q_t16, ext_t.astype(BF16)], axis=0)
        yield
        s_c0 = _dot(keys(hh, c0), qx[hh][:, blk:])
        yield
        second = update(state[1], s_c0, values_t(hh, c0))
        carry0[hh] = tuple(jnp.concatenate([a, b], axis=1) for a, b in zip(state[0], second))

    for _ in itertools.zip_longest(*[prologue(hh) for hh in range(hps)]):
        pass

    def body(p, carries):
        scores = [_dot(keys(hh, 2 * p, 2), qx[hh]) for hh in range(hps)]
        return tuple(update(carries[hh], scores[hh], values_t(hh, 2 * p, 2)) for hh in range(hps))

    final = lax.fori_loop(0, cq, body, tuple(carry0))
    for hh in range(hps):
        _, l_f, acc_f = final[hh]
        o_t = acc_f / l_f
        o_t = o_t * lax.rsqrt(jnp.mean(o_t * o_t, axis=0, keepdims=True) + RMS_EPS)
        o_ref[:, head_lanes(hh)] = (jnp.transpose(o_t) * onw_ref[...]).astype(o_ref.dtype)


def moba_heads(proj, out_norm_w, batch, seq, col_off):
    n = batch * seq
    blk = MOBA_BLOCK
    nb = seq // blk
    nq = nb // 2
    hps = MOBA_HEADS_PER_STEP
    ng = MOBA_HEADS // hps
    wide = hps * HEAD_DIM
    sec0 = col_off // hps
    slopes = jnp.exp2(-8.0 * jnp.arange(1, MOBA_HEADS + 1, dtype=F32) / MOBA_HEADS)
    grid_spec = pltpu.PrefetchScalarGridSpec(
        num_scalar_prefetch=1,
        grid=(batch, ng, nq),
        in_specs=[pl.BlockSpec((2 * blk, wide), lambda b, h, c, *_: (b * nq + c, sec0 + h)),
                  pl.BlockSpec((seq, wide), lambda b, h, c, *_: (b, sec0 + ng + h)),
                  pl.BlockSpec((seq, wide), lambda b, h, c, *_: (b, sec0 + 2 * ng + h)),
                  pl.BlockSpec((1, HEAD_DIM), lambda b, h, c, *_: (0, 0))],
        out_specs=pl.BlockSpec((2 * blk, wide), lambda b, h, c, *_: (b * nq + c, h)),
        scratch_shapes=[pltpu.VMEM((LANES, HEAD_DIM), F32),
                        pltpu.VMEM((nb * blk, HEAD_DIM + LANES), BF16),
                        pltpu.VMEM((HEAD_DIM, nb * blk), BF16)] * hps)
    assert nb % 2 == 0 and nb + 3 <= LANES and col_off % hps == 0
    return pl.pallas_call(
        functools.partial(_moba_body, n_blocks=nb),
        grid_spec=grid_spec,
        out_shape=jax.ShapeDtypeStruct((n, MOBA_WIDTH), BF16),
        compiler_params=_cparams(("arbitrary", "arbitrary", "arbitrary")),
        name="moba_heads",
    )(slopes, proj, proj, proj, out_norm_w.reshape(1, HEAD_DIM))


def _out_proj_body(x_ref, og_ref, om_ref, wg_ref, wm_ref, nw_ref, wrh_ref, wrl_ref, rb_ref,
                   x2_ref, hp_ref, gt_ref, id_ref):
    tm = x_ref.shape[0]
    x2 = x_ref[...] + _dot(og_ref[...], wg_ref[...]) + _dot(om_ref[...], wm_ref[...])
    x2_ref[...] = x2
    ms = jnp.mean(x2 * x2, axis=-1, keepdims=True)
    h2 = x2 * lax.rsqrt(ms + RMS_EPS) * nw_ref[...]
    h_hi = h2.astype(BF16)
    hp_ref[...] = h2
    h_lo = (h2 - h_hi.astype(F32)).astype(BF16)
    lg = _dot(h_hi, wrh_ref[...]) + _dot(h_hi, wrl_ref[...]) + _dot(h_lo, wrh_ref[...]) + rb_ref[...]

    lane = lax.broadcasted_iota(I32, (tm, LANES), 1)
    is_g = lane < N_GROUPS
    mg = jnp.max(jnp.where(is_g, lg, NEG_INF), axis=-1, keepdims=True)
    g_idx = jnp.min(jnp.where(jnp.logical_and(is_g, lg == mg), lane, LANES), axis=-1, keepdims=True)
    sum_g = jnp.sum(jnp.where(is_g, jnp.exp(lg - mg), 0.0), axis=-1, keepdims=True)
    p_top_g = 1.0 / sum_g
    lo = N_GROUPS + g_idx * EXPERTS_PER_GROUP
    in_grp = jnp.logical_and(lane >= lo, lane < lo + EXPERTS_PER_GROUP)
    m1 = jnp.max(jnp.where(in_grp, lg, NEG_INF), axis=-1, keepdims=True)
    i1 = jnp.min(jnp.where(jnp.logical_and(in_grp, lg == m1), lane, LANES), axis=-1, keepdims=True)
    rest = jnp.logical_and(in_grp, lane != i1)
    m2 = jnp.max(jnp.where(rest, lg, NEG_INF), axis=-1, keepdims=True)
    i2 = jnp.min(jnp.where(jnp.logical_and(rest, lg == m2), lane, LANES), axis=-1, keepdims=True)
    e2 = jnp.exp(m2 - m1)
    gate1 = p_top_g / (1.0 + e2)
    gate2 = p_top_g * e2 / (1.0 + e2)
    gt_ref[...] = jnp.where(lane == 0, gate1, jnp.where(lane == 1, gate2, 0.0))
    id_ref[...] = jnp.where(lane == 0, i1 - N_GROUPS, jnp.where(lane == 1, i2 - N_GROUPS, 0))


def out_proj(x, og, om, w_g, w_m, nw, wr_hi, wr_lo, r_bias, tm):
    n, d = x.shape
    const = lambda i: (0, 0)
    rows = lambda i: (i, 0)
    return pl.pallas_call(
        _out_proj_body,
        grid=(n // tm,),
        in_specs=[pl.BlockSpec((tm, d), rows),
                  pl.BlockSpec((tm, GDN_WIDTH), rows),
                  pl.BlockSpec((tm, MOBA_WIDTH), rows),
                  pl.BlockSpec((GDN_WIDTH, d), const),
                  pl.BlockSpec((MOBA_WIDTH, d), const),
                  pl.BlockSpec((1, d), const),
                  pl.BlockSpec((d, LANES), const),
                  pl.BlockSpec((d, LANES), const),
                  pl.BlockSpec((1, LANES), const)],
        out_specs=[pl.BlockSpec((tm, d), rows),
                   pl.BlockSpec((tm, d), rows),
                   pl.BlockSpec((tm, LANES), rows),
                   pl.BlockSpec((tm, LANES), rows)],
        out_shape=[jax.ShapeDtypeStruct((n, d), F32),
                   jax.ShapeDtypeStruct((n, d), F32),
                   jax.ShapeDtypeStruct((n, LANES), F32),
                   jax.ShapeDtypeStruct((n, LANES), I32)],
        compiler_params=_cparams(("arbitrary",)),
        name="out_proj_router",
    )(x, og, om, w_g, w_m, nw.reshape(1, d), wr_hi, wr_lo, r_bias)


def _invert_body(pos0_ref, pos1_ref, src_ref):
    n_rows = src_ref.shape[0]
    n_tok = pos0_ref.shape[0]

    def clear(r, carry):
        src_ref[r] = 0
        return carry

    lax.fori_loop(0, n_rows, clear, 0, unroll=8)

    def fill(t, carry):
        src_ref[pos0_ref[t]] = t
        src_ref[pos1_ref[t]] = t
        return carry

    lax.fori_loop(0, n_tok, fill, 0, unroll=8)


def moe_source_rows(pos0, pos1, n_rows):
    grid_spec = pltpu.PrefetchScalarGridSpec(
        num_scalar_prefetch=2,
        grid=(1,),
        in_specs=[],
        out_specs=pl.BlockSpec(memory_space=pltpu.SMEM))
    return pl.pallas_call(
        _invert_body,
        grid_spec=grid_spec,
        out_shape=jax.ShapeDtypeStruct((n_rows,), I32),
        compiler_params=_cparams(("arbitrary",)),
        name="moe_source_rows",
    )(pos0, pos1)


def _moe_body(blk_e_ref, nblk_ref, first_ref, slot_ref, next_e_ref, src_ref, h_hbm, wg_hbm, wu_hbm, wd_hbm,
              y_ref, xbuf, wg_buf, wu_buf, wd_buf, wg16, wu16, wd16, sem, gsem):
    rb = MOE_ROWS
    i = pl.program_id(0)
    e = blk_e_ref[i]
    slot = slot_ref[i]
    n_used = nblk_ref[0]
    xslot = lax.rem(i, 2)

    def row_copy(block, r, s):
        tok = src_ref[block * rb + r]
        return pltpu.make_async_copy(h_hbm.at[pl.ds(tok, 1)], xbuf.at[s, pl.ds(r, 1)], gsem.at[s])

    def wait_rows(s):
        pltpu.make_async_copy(h_hbm.at[pl.ds(0, rb)], xbuf.at[s], gsem.at[s]).wait()

    @pl.when(i == 0)
    def _():
        def issue(r, carry):
            row_copy(0, r, 0).start()
            return carry
        lax.fori_loop(0, rb, issue, 0, unroll=8)

    def weight_copies(expert, s):
        return (pltpu.make_async_copy(wg_hbm.at[expert], wg_buf.at[s], sem.at[s]),
                pltpu.make_async_copy(wu_hbm.at[expert], wu_buf.at[s], sem.at[s]),
                pltpu.make_async_copy(wd_hbm.at[expert], wd_buf.at[s], sem.at[s]))

    @pl.when(i == 0)
    def _():
        for cp in weight_copies(e, slot):
            cp.start()

    @pl.when(first_ref[i] == 1)
    def _():
        for cp in weight_copies(e, slot):
            cp.wait()

        @pl.when(next_e_ref[i] >= 0)
        def _():
            for cp in weight_copies(next_e_ref[i], 1 - slot):
                cp.start()

        wg16[...] = wg_buf[slot].astype(BF16)
        wu16[...] = wu_buf[slot].astype(BF16)
        wd16[...] = wd_buf[slot].astype(BF16)

    def compute(xb):
        g = _dot(xb, wg16[...])
        u = _dot(xb, wu16[...])
        hm = (g * _sigmoid(g) * u).astype(BF16)
        y_ref[...] = _dot(hm, wd16[...])

    @pl.when(i < n_used - 1)
    def _():
        wait_rows(xslot)
        xb = xbuf[xslot].astype(BF16)
        for r in range(rb):
            row_copy(i + 1, r, 1 - xslot).start(priority=r % 2)
        compute(xb)

    @pl.when(i == n_used - 1)
    def _():
        wait_rows(xslot)
        compute(xbuf[xslot].astype(BF16))

    @pl.when(i >= n_used)
    def _():
        y_ref[...] = jnp.zeros_like(y_ref)


def moe_experts(h2, src_rows, blk_expert, n_used, first, slot, next_e, w_gate, w_up, w_down):
    rb = MOE_ROWS
    n_rows = src_rows.shape[0]
    d = h2.shape[1]
    de = w_gate.shape[-1]
    grid_spec = pltpu.PrefetchScalarGridSpec(
        num_scalar_prefetch=6,
        grid=(n_rows // rb,),
        in_specs=[pl.BlockSpec(memory_space=pl.ANY),
                  pl.BlockSpec(memory_space=pl.ANY),
                  pl.BlockSpec(memory_space=pl.ANY),
                  pl.BlockSpec(memory_space=pl.ANY)],
        out_specs=pl.BlockSpec((rb, d), lambda i, *_: (i, 0)),
        scratch_shapes=[pltpu.VMEM((2, rb, d), F32),
                        pltpu.VMEM((2, d, de), F32), pltpu.VMEM((2, d, de), F32), pltpu.VMEM((2, de, d), F32),
                        pltpu.VMEM((d, de), BF16), pltpu.VMEM((d, de), BF16), pltpu.VMEM((de, d), BF16),
                        pltpu.SemaphoreType.DMA((2,)), pltpu.SemaphoreType.DMA((2,))])
    return pl.pallas_call(
        _moe_body,
        grid_spec=grid_spec,
        out_shape=jax.ShapeDtypeStruct((n_rows, d), F32),
        compiler_params=_cparams(("arbitrary",)),
        name="moe_experts",
    )(blk_expert, n_used, first, slot, next_e, src_rows, h2, w_gate, w_up, w_down)


def _combine_body(pos0_ref, pos1_ref, y_hbm, x2_ref, gt_ref, nw_ref, o_ref, buf, sem):
    tf = COMBINE_ROWS
    i = pl.program_id(0)
    n_steps = pl.num_programs(0)
    slot = lax.rem(i, 2)

    def issue(step, sl):
        def body(r, carry):
            t = step * tf + r
            pltpu.make_async_copy(y_hbm.at[pl.ds(pos0_ref[t], 1)], buf.at[sl, 0, pl.ds(r, 1)],
                                  sem.at[sl]).start()
            pltpu.make_async_copy(y_hbm.at[pl.ds(pos1_ref[t], 1)], buf.at[sl, 1, pl.ds(r, 1)],
                                  sem.at[sl]).start(priority=1)
            return carry
        lax.fori_loop(0, tf, body, 0, unroll=8)

    @pl.when(i == 0)
    def _():
        issue(0, 0)

    @pl.when(i + 1 < n_steps)
    def _():
        issue(i + 1, 1 - slot)

    for kk in range(TOPK_IN_GROUP):
        pltpu.make_async_copy(y_hbm.at[pl.ds(0, tf)], buf.at[slot, kk], sem.at[slot]).wait()

    gt = gt_ref[...]
    xo = x2_ref[...] + gt[:, 0:1] * buf[slot, 0] + gt[:, 1:2] * buf[slot, 1]
    ms = jnp.mean(xo * xo, axis=-1, keepdims=True)
    o_ref[...] = xo * lax.rsqrt(ms + RMS_EPS) * nw_ref[...]


def moe_combine(yb, pos0, pos1, x2, gates, nw):
    n, d = x2.shape
    tf = COMBINE_ROWS
    grid_spec = pltpu.PrefetchScalarGridSpec(
        num_scalar_prefetch=2,
        grid=(n // tf,),
        in_specs=[pl.BlockSpec(memory_space=pl.ANY),
                  pl.BlockSpec((tf, d), lambda i, *_: (i, 0)),
                  pl.BlockSpec((tf, LANES), lambda i, *_: (i, 0)),
                  pl.BlockSpec((1, d), lambda i, *_: (0, 0))],
        out_specs=pl.BlockSpec((tf, d), lambda i, *_: (i, 0)),
        scratch_shapes=[pltpu.VMEM((2, 2, tf, d), F32), pltpu.SemaphoreType.DMA((2,))])
    return pl.pallas_call(
        _combine_body,
        grid_spec=grid_spec,
        out_shape=jax.ShapeDtypeStruct((n, d), F32),
        compiler_params=_cparams(("arbitrary",)),
        name="moe_combine",
    )(pos0, pos1, yb, x2, gates, nw.reshape(1, d))


def _dispatch_plan(expert_id):
    n_tok, k = expert_id.shape
    rb = MOE_ROWS
    n_assign = n_tok * k
    e_flat = expert_id.reshape(-1)
    onehot = (e_flat[:, None] == jnp.arange(N_EXPERTS, dtype=I32)[None, :]).astype(I32)
    csum = jnp.cumsum(onehot, axis=0)
    counts = csum[-1]
    padded = (counts + rb - 1) // rb * rb
    pad_end = jnp.cumsum(padded)
    pad_start = pad_end - padded
    pos = jnp.sum(onehot * (pad_start[None, :] + csum - 1), axis=1).astype(I32).reshape(n_tok, k)
    n_rb = -(-n_assign // rb) + N_EXPERTS
    blk_start = jnp.arange(n_rb, dtype=I32) * rb
    blk_expert = jnp.minimum(jnp.sum((pad_end[None, :] <= blk_start[:, None]).astype(I32), axis=1),
                             N_EXPERTS - 1).astype(I32)
    n_used = (pad_end[-1] // rb).astype(I32).reshape(1)
    blk = jnp.arange(n_rb, dtype=I32)
    prev_expert = jnp.concatenate([jnp.full((1,), -1, I32), blk_expert[:-1]])
    first = jnp.logical_and(blk < n_used[0], blk_expert != prev_expert).astype(I32)
    slot = jnp.bitwise_and(jnp.cumsum(first) - 1, 1).astype(I32)
    ids = jnp.arange(N_EXPERTS, dtype=I32)
    later = jnp.logical_and(ids[None, :] > ids[:, None], counts[None, :] > 0)
    next_nonempty = jnp.min(jnp.where(later, ids[None, :], N_EXPERTS), axis=1)
    next_nonempty = jnp.where(next_nonempty < N_EXPERTS, next_nonempty, -1).astype(I32)
    next_e = next_nonempty[blk_expert]
    return pos, blk_expert, n_used, first, slot, next_e, n_rb * rb


def kernel(x, norm_mix_w, w_in, gdn_conv_w, gdn_A_log, gdn_dt_bias, gdn_out_norm_w, moba_out_norm_w, w_out, norm_ffn_w, w_router_group, b_router_group, w_router_expert, b_router_expert, w_expert_gate, w_expert_up, w_expert_down, norm_final_w):
    batch, seq, d = x.shape
    n = batch * seq
    assert w_in.shape[0] == 1, "the final norm is fused into the last layer's combine; one layer supported"
    l = 0
    xf = x.reshape(n, d).astype(F32)
    gw = 4 * GDN_WIDTH
    w_l = w_in[l]
    mq0 = gw + 2 * GDN_HEADS
    col_scale = jnp.where(jnp.arange(3 * MOBA_WIDTH) < MOBA_WIDTH, HEAD_DIM ** -0.5 * LOG2E, 1.0).astype(F32)
    w_gdn = w_l[:, :gw].astype(BF16)
    w_moba = (w_l[:, mq0:] * col_scale[None, :]).astype(BF16)
    w_ba = jnp.pad(w_l[:, gw:gw + 2 * GDN_HEADS], ((0, 0), (0, LANES - 2 * GDN_HEADS))).astype(BF16)
    proj_g, ba = norm_matmul(xf, norm_mix_w[l], w_gdn, BF16, IN_PROJ_ROWS, gw // 2, w_small=w_ba)
    proj_m = norm_matmul(xf, norm_mix_w[l], w_moba, BF16, IN_PROJ_ROWS, 3 * MOBA_WIDTH // 2)
    og = gdn_heads(proj_g, ba, gdn_conv_w[l].astype(F32), gdn_A_log[l], gdn_dt_bias[l],
                   gdn_out_norm_w[l].astype(F32), batch, seq)
    om = moba_heads(proj_m, moba_out_norm_w[l].astype(F32), batch, seq, 0)

    w_o = w_out[l].astype(BF16)
    n_r = N_GROUPS + N_EXPERTS
    w_router = jnp.pad(jnp.concatenate([w_router_group[l], w_router_expert[l]], axis=1).astype(F32),
                       ((0, 0), (0, LANES - n_r)))
    wr_hi = w_router.astype(BF16)
    wr_lo = (w_router - wr_hi.astype(F32)).astype(BF16)
    r_bias = jnp.pad(jnp.concatenate([b_router_group[l], b_router_expert[l]]).astype(F32),
                     (0, LANES - n_r)).reshape(1, LANES)
    x2, hp, gates, ids = out_proj(xf, og, om, w_o[:GDN_WIDTH], w_o[GDN_WIDTH:], norm_ffn_w[l].astype(F32),
                                  wr_hi, wr_lo, r_bias, 512)

    pos, blk_expert, n_used, first, slot, next_e, n_rows = _dispatch_plan(ids[:, :TOPK_IN_GROUP])
    pos0, pos1 = pos[:, 0], pos[:, 1]
    src_rows = moe_source_rows(pos0, pos1, n_rows)
    yb = moe_experts(hp, src_rows, blk_expert, n_used, first, slot, next_e, w_expert_gate[l].astype(F32),
                     w_expert_up[l].astype(F32), w_expert_down[l].astype(F32))
    out = moe_combine(yb, pos0, pos1, x2, gates, norm_final_w.astype(F32))
    return out.reshape(batch, seq, d).astype(x.dtype)
```

```python
import functools
import itertools

import jax
import jax.numpy as jnp
from jax import lax
from jax.experimental import pallas as pl
from jax.experimental.pallas import tpu as pltpu

F32 = jnp.float32
BF16 = jnp.bfloat16
U32 = jnp.uint32
I32 = jnp.int32

HEAD_DIM = 128
GDN_HEADS = 8
MOBA_HEADS = 8
GDN_WIDTH = GDN_HEADS * HEAD_DIM
MOBA_WIDTH = MOBA_HEADS * HEAD_DIM
CONV_WIDTH = 4
GDN_CHUNK = 64
MOBA_BLOCK = 256
MOBA_TOPK = 3
N_GROUPS = 4
EXPERTS_PER_GROUP = 8
N_EXPERTS = N_GROUPS * EXPERTS_PER_GROUP
TOPK_IN_GROUP = 2
RMS_EPS = 1e-6
NEG_INF = -1e30
MOBA_MASK = -(2.0 ** 99)
LOG2E = 1.4426950408889634
LANES = 128
VMEM_LIMIT = 56 * 1024 * 1024

IN_PROJ_ROWS = 1024
GDN_TILE = 256
GDN_HEADS_PER_STEP = 8
MOBA_HEADS_PER_STEP = 4
MOE_ROW_BUFFERS = 3
MOE_ROWS = 256
COMBINE_ROWS = 128
HI = lax.Precision.HIGHEST


def _cparams(sem, **kw):
    return pltpu.CompilerParams(dimension_semantics=sem, vmem_limit_bytes=VMEM_LIMIT, **kw)


def _dot(a, b):
    return jnp.dot(a, b, preferred_element_type=F32)


def _dot_nt(a, b, precision=None):
    return lax.dot_general(a, b, (((1,), (1,)), ((), ())), preferred_element_type=F32,
                           precision=precision)


def _sigmoid(x):
    return 1.0 / (1.0 + jnp.exp(-x))


def _norm_matmul_body(x_ref, nw_ref, w_ref, *rest):
    x = x_ref[...]
    ms = jnp.mean(x * x, axis=-1, keepdims=True)
    h = (x * lax.rsqrt(ms + RMS_EPS) * nw_ref[...]).astype(BF16)
    if len(rest) == 1:
        rest[0][...] = _dot(h, w_ref[...]).astype(rest[0].dtype)
    else:
        ws_ref, o_ref, os_ref = rest
        o_ref[...] = _dot(h, w_ref[...]).astype(o_ref.dtype)
        os_ref[0] = _dot(h, ws_ref[...])


def norm_matmul(x, nw, w, out_dtype, tm, tn, w_small=None):
    n, d = x.shape
    width = w.shape[1]
    in_specs = [pl.BlockSpec((tm, d), lambda j, i: (i, 0)),
                pl.BlockSpec((1, d), lambda j, i: (0, 0)),
                pl.BlockSpec((d, tn), lambda j, i: (0, j))]
    out_specs = pl.BlockSpec((tm, tn), lambda j, i: (i, j))
    out_shape = jax.ShapeDtypeStruct((n, width), out_dtype)
    operands = (x, nw.reshape(1, d), w)
    if w_small is not None:
        ws = w_small.shape[1]
        in_specs.append(pl.BlockSpec((d, ws), lambda j, i: (0, 0)))
        out_specs = [out_specs, pl.BlockSpec((1, tm, ws), lambda j, i: (j, i, 0))]
        out_shape = [out_shape, jax.ShapeDtypeStruct((width // tn, n, ws), F32)]
        operands = operands + (w_small,)
    outs = pl.pallas_call(
        _norm_matmul_body,
        grid=(width // tn, n // tm),
        in_specs=in_specs,
        out_specs=out_specs,
        out_shape=out_shape,
        compiler_params=_cparams(("arbitrary", "arbitrary")),
        name="norm_in_proj",
    )(*operands)
    if w_small is None:
        return outs
    return outs[0], outs[1][0]


def _gdn_body(alog_ref, dtb_ref, q_ref, k_ref, v_ref, z_ref, ba_ref, cwq_ref, cwk_ref, cwv_ref,
              onw_ref, o_ref, *scratch):
    tt = GDN_TILE
    c = GDN_CHUNK
    t = pl.program_id(2)

    @pl.when(t == 0)
    def _():
        for hh in range(GDN_HEADS_PER_STEP):
            s_ref, prev_ref, vn_ref, _ = scratch[4 * hh:4 * hh + 4]
            s_ref[...] = jnp.zeros_like(s_ref)
            prev_ref[...] = jnp.zeros_like(prev_ref)
            vn_ref[...] = jnp.zeros_like(vn_ref)

    row_8 = lax.broadcasted_iota(jnp.int32, (8, LANES), 0)
    lane_t = lax.broadcasted_iota(jnp.int32, (tt, LANES), 1)
    row = lax.broadcasted_iota(jnp.int32, (tt, tt), 0)
    col = lax.broadcasted_iota(jnp.int32, (tt, tt), 1)
    same = (row // c) == (col // c)
    causal = jnp.logical_and(same, col <= row)
    strict = jnp.logical_and(same, col < row)
    causal16 = jnp.where(causal, 1.0, 0.0).astype(BF16)
    same16 = jnp.where(same, 1.0, 0.0).astype(BF16)
    eye = jnp.where(row == col, 1.0, 0.0)
    colk = lax.broadcasted_iota(jnp.int32, (HEAD_DIM, tt), 1)
    ba = ba_ref[...]

    heads = [_gdn_one_head(hh, pl.program_id(1) * GDN_HEADS_PER_STEP + hh, alog_ref, dtb_ref, q_ref, k_ref,
                           v_ref, z_ref, ba, cwq_ref, cwk_ref, cwv_ref, onw_ref, o_ref,
                           *scratch[4 * hh:4 * hh + 4],
                           row_8, lane_t, causal, strict, causal16, same16, eye, colk)
             for hh in range(GDN_HEADS_PER_STEP)]
    for _ in itertools.zip_longest(*heads):
        pass


def _gdn_one_head(hh, h, alog_ref, dtb_ref, q_ref, k_ref, v_ref, z_ref, ba, cwq_ref, cwk_ref, cwv_ref,
                  onw_ref, o_ref, s_ref, prev_ref, vn_ref, oacc_ref,
                  row_8, lane_t, causal, strict, causal16, same16, eye, colk):
    tt = GDN_TILE
    c = GDN_CHUNK
    lanes = slice(hh * HEAD_DIM, (hh + 1) * HEAD_DIM)

    def conv_silu(x_ref, cw_ref, idx):
        x = x_ref[:, lanes].astype(F32)
        p = prev_ref[idx]
        w = cw_ref[:, lanes]
        acc = x * w[CONV_WIDTH - 1:CONV_WIDTH, :]
        for s in range(1, CONV_WIDTH):
            xs = pltpu.roll(x, s, axis=0)
            head = jnp.where(row_8 < s, pltpu.roll(p, s, axis=0), xs[:8])
            xs = jnp.concatenate([head, xs[8:]], axis=0)
            acc = acc + xs * w[CONV_WIDTH - 1 - s:CONV_WIDTH - s, :]
        prev_ref[idx] = x[tt - 8:tt, :]
        return acc * _sigmoid(acc)

    q = conv_silu(q_ref, cwq_ref, 0)
    k = conv_silu(k_ref, cwk_ref, 1)
    v = conv_silu(v_ref, cwv_ref, 2)
    q = q * lax.rsqrt(jnp.sum(q * q, axis=-1, keepdims=True) + 1e-6) * (HEAD_DIM ** -0.5)
    k = k * lax.rsqrt(jnp.sum(k * k, axis=-1, keepdims=True) + 1e-6)
    yield

    b_col = jnp.sum(jnp.where(lane_t == h, ba, 0.0), axis=-1, keepdims=True)
    a_col = jnp.sum(jnp.where(lane_t == h + GDN_HEADS, ba, 0.0), axis=-1, keepdims=True)
    beta = _sigmoid(b_col)
    xa = a_col + dtb_ref[h]
    softplus = jnp.maximum(xa, 0.0) + jnp.log(1.0 + jnp.exp(-jnp.abs(xa)))
    g = -jnp.exp(jnp.full((1, 1), alog_ref[h], F32)) * softplus

    g_hi = g.astype(BF16).astype(F32)
    g_mid = (g - g_hi).astype(BF16).astype(F32)
    g_lo = g - g_hi - g_mid
    g3 = jnp.where(lane_t == 0, g_hi, jnp.where(lane_t == 1, g_mid, jnp.where(lane_t == 2, g_lo, 0.0)))
    g3 = g3.astype(BF16)

    def three(r):
        return r[:, 0:1] + r[:, 1:2] + r[:, 2:3]

    gc_col = three(_dot(causal16, g3))
    gc_b = jnp.broadcast_to(gc_col, (tt, LANES))
    glast_b = jnp.broadcast_to(three(_dot(same16, g3)), (tt, LANES))
    gc_row = jnp.transpose(gc_b)[0:1, :]
    decay = jnp.exp(jnp.where(causal, gc_col - gc_row, NEG_INF))
    yield

    kb = k * beta
    k16 = k.astype(BF16)
    kk = _dot_nt(kb.astype(BF16), k16)
    lmat = jnp.where(strict, kk * decay, 0.0)
    yield
    attn = _dot_nt(q.astype(BF16), k16) * decay

    tinv = eye - lmat
    m16 = lmat.astype(BF16)
    for _ in range(5):
        m16 = _dot(m16, m16).astype(BF16)
        yield
        tinv = tinv + _dot(tinv.astype(BF16), m16)
        yield

    egc = jnp.exp(gc_b)
    rhs = jnp.concatenate([v * beta, kb * egc], axis=-1)
    sol = _dot(tinv.astype(BF16), rhs.astype(BF16))
    yield
    u = sol[:, :HEAD_DIM]
    w16 = sol[:, HEAD_DIM:].astype(BF16)
    qd16 = (q * egc).astype(BF16)
    kd = k * jnp.exp(glast_b - gc_b)
    kdt = jnp.transpose(kd)
    gtot_b = jnp.exp(glast_b)
    attn16 = attn.astype(BF16)

    s = s_ref[...]
    for n in range(tt // c):
        sl = slice(n * c, (n + 1) * c)
        s16 = s.astype(BF16)
        v_new = u[sl] - _dot(w16[sl], s16)
        vn_ref[sl, :] = v_new
        yield
        vn16 = vn_ref[...].astype(BF16)
        oacc_ref[sl, :] = _dot(qd16[sl], s16) + _dot(attn16[sl], vn16)
        kdt_n = jnp.where((colk // c) == n, kdt, 0.0).astype(BF16)
        s = s * gtot_b[n * c:n * c + 1, :] + _dot(kdt_n, vn16)
        yield
    s_ref[...] = s

    o = oacc_ref[...]
    o = o * lax.rsqrt(jnp.mean(o * o, axis=-1, keepdims=True) + RMS_EPS) * onw_ref[...]
    z = z_ref[:, lanes].astype(F32)
    o_ref[:, lanes] = (o * (z * _sigmoid(z))).astype(o_ref.dtype)


def gdn_heads(proj, ba, conv_w, a_log, dt_bias, out_norm_w, batch, seq):
    n = batch * seq
    tt = GDN_TILE
    nt = seq // tt
    hps = GDN_HEADS_PER_STEP
    ng = GDN_HEADS // hps
    wide = hps * HEAD_DIM

    def col_spec(section):
        return pl.BlockSpec((tt, wide), lambda b, h, t, *_: (b * nt + t, section * ng + h))

    def cw_spec(section):
        return pl.BlockSpec((CONV_WIDTH, wide), lambda b, h, t, *_: (0, section * ng + h))

    grid_spec = pltpu.PrefetchScalarGridSpec(
        num_scalar_prefetch=2,
        grid=(batch, ng, nt),
        in_specs=[col_spec(0), col_spec(1), col_spec(2), col_spec(3),
                  pl.BlockSpec((tt, LANES), lambda b, h, t, *_: (b * nt + t, 0)),
                  cw_spec(0), cw_spec(1), cw_spec(2),
                  pl.BlockSpec((1, HEAD_DIM), lambda b, h, t, *_: (0, 0))],
        out_specs=pl.BlockSpec((tt, wide), lambda b, h, t, *_: (b * nt + t, h)),
        scratch_shapes=[pltpu.VMEM((HEAD_DIM, HEAD_DIM), F32),
                        pltpu.VMEM((3, 8, HEAD_DIM), F32),
                        pltpu.VMEM((tt, HEAD_DIM), F32),
                        pltpu.VMEM((tt, HEAD_DIM), F32)] * hps)
    return pl.pallas_call(
        _gdn_body,
        grid_spec=grid_spec,
        out_shape=jax.ShapeDtypeStruct((n, GDN_WIDTH), BF16),
        compiler_params=_cparams(("arbitrary", "arbitrary", "arbitrary")),
        name="gdn_heads",
    )(a_log.astype(F32), dt_bias.astype(F32), proj, proj, proj, proj, ba,
      conv_w, conv_w, conv_w, out_norm_w.reshape(1, HEAD_DIM))


def _moba_body(slope_ref, q_ref, k_ref, v_ref, onw_ref, o_ref, *scratch, n_blocks):
    blk = MOBA_BLOCK
    tq = 2 * blk
    nb = n_blocks
    nbp = -(-nb // 8) * 8
    hps = MOBA_HEADS_PER_STEP
    cq = pl.program_id(2)
    c0 = 2 * cq
    lane = lax.broadcasted_iota(I32, (blk, LANES), 1)
    row = lax.broadcasted_iota(I32, (blk, blk), 0)
    col = lax.broadcasted_iota(I32, (blk, blk), 1)
    rid = lax.broadcasted_iota(I32, (LANES, tq), 0)
    bid = lax.broadcasted_iota(I32, (nbp, tq), 0)
    c_row = c0 + (lax.broadcasted_iota(I32, (nbp, tq), 1) >= blk).astype(I32)
    ones_ext_t = jnp.where(jnp.logical_and(rid >= nb, rid < nb + 3), 1.0, 0.0).astype(BF16)

    def head_lanes(hh):
        return slice(hh * HEAD_DIM, (hh + 1) * HEAD_DIM)

    def keys(hh, j, nblk=1):
        return scratch[3 * hh + 1][pl.ds(pl.multiple_of(j * blk, blk), nblk * blk), :]

    def values_t(hh, j, nblk=1):
        return scratch[3 * hh + 2][:, pl.ds(pl.multiple_of(j * blk, blk), nblk * blk)]

    @pl.when(cq == 0)
    def _():
        t_in = lax.broadcasted_iota(I32, (blk, LANES), 0)
        for hh in range(hps):
            kmean_ref, kx_ref, vt_ref = scratch[3 * hh:3 * hh + 3]
            slope2 = slope_ref[pl.program_id(1) * hps + hh] * LOG2E
            kmean_ref[...] = jnp.zeros_like(kmean_ref)
            for j in range(nb):
                kj = k_ref[j * blk:(j + 1) * blk, head_lanes(hh)]
                kmean_ref[j:j + 1, :] = jnp.mean(kj.astype(F32), axis=0, keepdims=True)
                bias = slope2 * (t_in + j * blk).astype(F32)
                b_hi = bias.astype(BF16)
                r1 = bias - b_hi.astype(F32)
                b_mid = r1.astype(BF16)
                b_lo = (r1 - b_mid.astype(F32)).astype(BF16)
                ext = jnp.where(lane == j, 1.0, 0.0).astype(BF16)
                ext = jnp.where(lane == nb, b_hi, ext)
                ext = jnp.where(lane == nb + 1, b_mid, ext)
                ext = jnp.where(lane == nb + 2, b_lo, ext)
                kx_ref[j * blk:(j + 1) * blk, :] = jnp.concatenate([kj, ext], axis=1)
                vj = v_ref[j * blk:(j + 1) * blk, head_lanes(hh)].astype(F32)
                vt_ref[:, j * blk:(j + 1) * blk] = jnp.transpose(vj).astype(BF16)

    def start(s, vals_t):
        m = jnp.max(s, axis=0, keepdims=True)
        p = jnp.exp2(s - m)
        return m, jnp.sum(p, axis=0, keepdims=True), _dot(vals_t, p.astype(BF16))

    def update(carry, s, vals_t):
        m_i, l_i, acc = carry
        m_new = jnp.maximum(m_i, jnp.max(s, axis=0, keepdims=True))
        alpha = jnp.exp2(m_i - m_new)
        p = jnp.exp2(s - m_new)
        return (m_new, l_i * alpha + jnp.sum(p, axis=0, keepdims=True),
                acc * alpha + _dot(vals_t, p.astype(BF16)))

    qx = [None] * hps
    carry0 = [None] * hps

    def prologue(hh):
        q_t = jnp.transpose(q_ref[:, head_lanes(hh)].astype(F32))
        q_t16 = q_t.astype(BF16)
        yield
        qx0_t = jnp.concatenate([q_t16, ones_ext_t], axis=0)
        state = []
        for half in range(2):
            s_own = _dot(keys(hh, c0 + half), qx0_t[:, half * blk:(half + 1) * blk])
            s_own = jnp.where(row <= col, s_own, MOBA_MASK)
            yield
            state.append(start(s_own, values_t(hh, c0 + half)))
            yield
        gate_t = jnp.dot(scratch[3 * hh][...], q_t, preferred_element_type=F32, precision=HI)[:nbp]
        gm = jnp.where(bid < c_row, gate_t, NEG_INF)
        sel = jnp.zeros((nbp, tq), F32)
        for s in range(MOBA_TOPK):
            mx = jnp.max(gm, axis=0, keepdims=True)
            idx = jnp.min(jnp.where(gm == mx, bid, LANES), axis=0, keepdims=True)
            pick = bid == idx
            sel = jnp.where(jnp.logical_and(pick, s < c_row), 1.0, sel)
            gm = jnp.where(pick, -3e38, gm)
        mask_t = jnp.where(sel > 0.5, 0.0, MOBA_MASK)
        ext_t = jnp.concatenate([mask_t, jnp.zeros((LANES - nbp, tq), F32)], axis=0)
        ext_t = jnp.where(rid < nb, ext_t, jnp.where(rid < nb + 3, 1.0, 0.0))
        qx[hh] = jnp.concatenate([q_t16, ext_t.astype(BF16)], axis=0)
        yield
        s_c0 = _dot(keys(hh, c0), qx[hh][:, blk:])
        yield
        second = update(state[1], s_c0, values_t(hh, c0))
        carry0[hh] = tuple(jnp.concatenate([a, b], axis=1) for a, b in zip(state[0], second))

    for _ in itertools.zip_longest(*[prologue(hh) for hh in range(hps)]):
        pass

    def body(p, carries):
        scores = [_dot(keys(hh, 2 * p, 2), qx[hh]) for hh in range(hps)]
        return tuple(update(carries[hh], scores[hh], values_t(hh, 2 * p, 2)) for hh in range(hps))

    final = lax.fori_loop(0, cq, body, tuple(carry0))
    for hh in range(hps):
        _, l_f, acc_f = final[hh]
        o_t = acc_f / l_f
        o_t = o_t * lax.rsqrt(jnp.mean(o_t * o_t, axis=0, keepdims=True) + RMS_EPS)
        o_ref[:, head_lanes(hh)] = (jnp.transpose(o_t) * onw_ref[...]).astype(o_ref.dtype)


def moba_heads(proj, out_norm_w, batch, seq, col_off):
    n = batch * seq
    blk = MOBA_BLOCK
    nb = seq // blk
    nq = nb // 2
    hps = MOBA_HEADS_PER_STEP
    ng = MOBA_HEADS // hps
    wide = hps * HEAD_DIM
    sec0 = col_off // hps
    slopes = jnp.exp2(-8.0 * jnp.arange(1, MOBA_HEADS + 1, dtype=F32) / MOBA_HEADS)
    grid_spec = pltpu.PrefetchScalarGridSpec(
        num_scalar_prefetch=1,
        grid=(batch, ng, nq),
        in_specs=[pl.BlockSpec((2 * blk, wide), lambda b, h, c, *_: (b * nq + c, sec0 + h)),
                  pl.BlockSpec((seq, wide), lambda b, h, c, *_: (b, sec0 + ng + h)),
                  pl.BlockSpec((seq, wide), lambda b, h, c, *_: (b, sec0 + 2 * ng + h)),
                  pl.BlockSpec((1, HEAD_DIM), lambda b, h, c, *_: (0, 0))],
        out_specs=pl.BlockSpec((2 * blk, wide), lambda b, h, c, *_: (b * nq + c, h)),
        scratch_shapes=[pltpu.VMEM((LANES, HEAD_DIM), F32),
                        pltpu.VMEM((nb * blk, HEAD_DIM + LANES), BF16),
                        pltpu.VMEM((HEAD_DIM, nb * blk), BF16)] * hps)
    assert nb % 2 == 0 and nb + 3 <= LANES and col_off % hps == 0
    return pl.pallas_call(
        functools.partial(_moba_body, n_blocks=nb),
        grid_spec=grid_spec,
        out_shape=jax.ShapeDtypeStruct((n, MOBA_WIDTH), BF16),
        compiler_params=_cparams(("arbitrary", "arbitrary", "arbitrary")),
        name="moba_heads",
    )(slopes, proj, proj, proj, out_norm_w.reshape(1, HEAD_DIM))


def _out_proj_body(x_ref, og_ref, om_ref, wg_ref, wm_ref, nw_ref, wrh_ref, wrl_ref, rb_ref,
                   x2_ref, hp_ref, gt_ref, id_ref):
    tm = x_ref.shape[0]
    x2 = x_ref[...] + _dot(og_ref[...], wg_ref[...]) + _dot(om_ref[...], wm_ref[...])
    x2_ref[...] = x2
    ms = jnp.mean(x2 * x2, axis=-1, keepdims=True)
    h2 = x2 * lax.rsqrt(ms + RMS_EPS) * nw_ref[...]
    h_hi = h2.astype(BF16)
    hp_ref[...] = h2
    h_lo = (h2 - h_hi.astype(F32)).astype(BF16)
    lg = _dot(h_hi, wrh_ref[...]) + _dot(h_hi, wrl_ref[...]) + _dot(h_lo, wrh_ref[...]) + rb_ref[...]

    lane = lax.broadcasted_iota(I32, (tm, LANES), 1)
    is_g = lane < N_GROUPS
    mg = jnp.max(jnp.where(is_g, lg, NEG_INF), axis=-1, keepdims=True)
    g_idx = jnp.min(jnp.where(jnp.logical_and(is_g, lg == mg), lane, LANES), axis=-1, keepdims=True)
    sum_g = jnp.sum(jnp.where(is_g, jnp.exp(lg - mg), 0.0), axis=-1, keepdims=True)
    p_top_g = 1.0 / sum_g
    lo = N_GROUPS + g_idx * EXPERTS_PER_GROUP
    in_grp = jnp.logical_and(lane >= lo, lane < lo + EXPERTS_PER_GROUP)
    m1 = jnp.max(jnp.where(in_grp, lg, NEG_INF), axis=-1, keepdims=True)
    i1 = jnp.min(jnp.where(jnp.logical_and(in_grp, lg == m1), lane, LANES), axis=-1, keepdims=True)
    rest = jnp.logical_and(in_grp, lane != i1)
    m2 = jnp.max(jnp.where(rest, lg, NEG_INF), axis=-1, keepdims=True)
    i2 = jnp.min(jnp.where(jnp.logical_and(rest, lg == m2), lane, LANES), axis=-1, keepdims=True)
    e2 = jnp.exp(m2 - m1)
    gate1 = p_top_g / (1.0 + e2)
    gate2 = p_top_g * e2 / (1.0 + e2)
    gt_ref[...] = jnp.where(lane == 0, gate1, jnp.where(lane == 1, gate2, 0.0))
    id_ref[...] = jnp.where(lane == 0, i1 - N_GROUPS, jnp.where(lane == 1, i2 - N_GROUPS, 0))


def out_proj(x, og, om, w_g, w_m, nw, wr_hi, wr_lo, r_bias, tm):
    n, d = x.shape
    const = lambda i: (0, 0)
    rows = lambda i: (i, 0)
    return pl.pallas_call(
        _out_proj_body,
        grid=(n // tm,),
        in_specs=[pl.BlockSpec((tm, d), rows),
                  pl.BlockSpec((tm, GDN_WIDTH), rows),
                  pl.BlockSpec((tm, MOBA_WIDTH), rows),
                  pl.BlockSpec((GDN_WIDTH, d), const),
                  pl.BlockSpec((MOBA_WIDTH, d), const),
                  pl.BlockSpec((1, d), const),
                  pl.BlockSpec((d, LANES), const),
                  pl.BlockSpec((d, LANES), const),
                  pl.BlockSpec((1, LANES), const)],
        out_specs=[pl.BlockSpec((tm, d), rows),
                   pl.BlockSpec((tm, d), rows),
                   pl.BlockSpec((tm, LANES), rows),
                   pl.BlockSpec((tm, LANES), rows)],
        out_shape=[jax.ShapeDtypeStruct((n, d), F32),
                   jax.ShapeDtypeStruct((n, d), F32),
                   jax.ShapeDtypeStruct((n, LANES), F32),
                   jax.ShapeDtypeStruct((n, LANES), I32)],
        compiler_params=_cparams(("arbitrary",)),
        name="out_proj_router",
    )(x, og, om, w_g, w_m, nw.reshape(1, d), wr_hi, wr_lo, r_bias)


def _invert_body(pos0_ref, pos1_ref, src_ref, zeros_vmem, sem):
    n_tok = pos0_ref.shape[0]
    zeros_vmem[...] = jnp.zeros_like(zeros_vmem)
    clear = pltpu.make_async_copy(zeros_vmem, src_ref, sem)
    clear.start()
    clear.wait()

    def fill(t, carry):
        src_ref[pos0_ref[t]] = t
        src_ref[pos1_ref[t]] = t
        return carry

    lax.fori_loop(0, n_tok, fill, 0, unroll=8)


def moe_source_rows(pos0, pos1, n_rows):
    grid_spec = pltpu.PrefetchScalarGridSpec(
        num_scalar_prefetch=2,
        grid=(1,),
        in_specs=[],
        out_specs=pl.BlockSpec(memory_space=pltpu.SMEM),
        scratch_shapes=[pltpu.VMEM((n_rows,), I32), pltpu.SemaphoreType.DMA(())])
    return pl.pallas_call(
        _invert_body,
        grid_spec=grid_spec,
        out_shape=jax.ShapeDtypeStruct((n_rows,), I32),
        compiler_params=_cparams(("arbitrary",)),
        name="moe_source_rows",
    )(pos0, pos1)


def _moe_body(blk_e_ref, nblk_ref, first_ref, slot_ref, next_e_ref, src_ref, h_hbm, wg_hbm, wu_hbm, wd_hbm,
              y_ref, xbuf, wg_buf, wu_buf, wd_buf, wg16, wu16, wd16, sem, gsem):
    rb = MOE_ROWS
    i = pl.program_id(0)
    e = blk_e_ref[i]
    slot = slot_ref[i]
    n_used = nblk_ref[0]
    nbuf = MOE_ROW_BUFFERS
    ahead = nbuf - 1
    xslot = lax.rem(i, nbuf)

    def row_copy(block, r):
        s = lax.rem(block, nbuf)
        tok = src_ref[block * rb + r]
        return pltpu.make_async_copy(h_hbm.at[pl.ds(tok, 1)], xbuf.at[s, pl.ds(r, 1)], gsem.at[s])

    def wait_rows(s):
        pltpu.make_async_copy(h_hbm.at[pl.ds(0, rb)], xbuf.at[s], gsem.at[s]).wait()

    @pl.when(i == 0)
    def _():
        for b in range(ahead):
            @pl.when(b < n_used)
            def _(b=b):
                def issue(r, carry):
                    row_copy(b, r).start()
                    return carry
                lax.fori_loop(0, rb, issue, 0, unroll=8)

    def weight_copies(expert, s):
        return (pltpu.make_async_copy(wg_hbm.at[expert], wg_buf.at[s], sem.at[s]),
                pltpu.make_async_copy(wu_hbm.at[expert], wu_buf.at[s], sem.at[s]),
                pltpu.make_async_copy(wd_hbm.at[expert], wd_buf.at[s], sem.at[s]))

    @pl.when(i == 0)
    def _():
        for cp in weight_copies(e, slot):
            cp.start(priority=1)

    @pl.when(first_ref[i] == 1)
    def _():
        for cp in weight_copies(e, slot):
            cp.wait()

        @pl.when(next_e_ref[i] >= 0)
        def _():
            for cp in weight_copies(next_e_ref[i], 1 - slot):
                cp.start(priority=1)

        wg16[...] = wg_buf[slot].astype(BF16)
        wu16[...] = wu_buf[slot].astype(BF16)
        wd16[...] = wd_buf[slot].astype(BF16)

    def compute(xb):
        g = _dot(xb, wg16[...])
        u = _dot(xb, wu16[...])
        hm = (g * _sigmoid(g) * u).astype(BF16)
        y_ref[...] = _dot(hm, wd16[...])

    @pl.when(i < n_used - ahead)
    def _():
        wait_rows(xslot)
        xb = xbuf[xslot].astype(BF16)
        for r in range(rb):
            row_copy(i + ahead, r).start()
        compute(xb)

    @pl.when(jnp.logical_and(i >= n_used - ahead, i < n_used))
    def _():
        wait_rows(xslot)
        compute(xbuf[xslot].astype(BF16))

    @pl.when(i >= n_used)
    def _():
        y_ref[...] = jnp.zeros_like(y_ref)


def moe_experts(h2, src_rows, blk_expert, n_used, first, slot, next_e, w_gate, w_up, w_down):
    rb = MOE_ROWS
    n_rows = src_rows.shape[0]
    d = h2.shape[1]
    de = w_gate.shape[-1]
    grid_spec = pltpu.PrefetchScalarGridSpec(
        num_scalar_prefetch=6,
        grid=(n_rows // rb,),
        in_specs=[pl.BlockSpec(memory_space=pl.ANY),
                  pl.BlockSpec(memory_space=pl.ANY),
                  pl.BlockSpec(memory_space=pl.ANY),
                  pl.BlockSpec(memory_space=pl.ANY)],
        out_specs=pl.BlockSpec((rb, d), lambda i, *_: (i, 0)),
        scratch_shapes=[pltpu.VMEM((MOE_ROW_BUFFERS, rb, d), F32),
                        pltpu.VMEM((2, d, de), F32), pltpu.VMEM((2, d, de), F32), pltpu.VMEM((2, de, d), F32),
                        pltpu.VMEM((d, de), BF16), pltpu.VMEM((d, de), BF16), pltpu.VMEM((de, d), BF16),
                        pltpu.SemaphoreType.DMA((2,)), pltpu.SemaphoreType.DMA((MOE_ROW_BUFFERS,))])
    return pl.pallas_call(
        _moe_body,
        grid_spec=grid_spec,
        out_shape=jax.ShapeDtypeStruct((n_rows, d), F32),
        compiler_params=_cparams(("arbitrary",)),
        name="moe_experts",
    )(blk_expert, n_used, first, slot, next_e, src_rows, h2, w_gate, w_up, w_down)


def _combine_body(pos0_ref, pos1_ref, y_hbm, x2_ref, gt_ref, nw_ref, o_ref, buf, sem):
    tf = COMBINE_ROWS
    i = pl.program_id(0)
    n_steps = pl.num_programs(0)
    slot = lax.rem(i, 2)

    def issue(step, sl):
        def body(r, carry):
            t = step * tf + r
            pltpu.make_async_copy(y_hbm.at[pl.ds(pos0_ref[t], 1)], buf.at[sl, 0, pl.ds(r, 1)],
                                  sem.at[sl]).start()
            pltpu.make_async_copy(y_hbm.at[pl.ds(pos1_ref[t], 1)], buf.at[sl, 1, pl.ds(r, 1)],
                                  sem.at[sl]).start(priority=1)
            return carry
        lax.fori_loop(0, tf, body, 0, unroll=8)

    @pl.when(i == 0)
    def _():
        issue(0, 0)

    @pl.when(i + 1 < n_steps)
    def _():
        issue(i + 1, 1 - slot)

    for kk in range(TOPK_IN_GROUP):
        pltpu.make_async_copy(y_hbm.at[pl.ds(0, tf)], buf.at[slot, kk], sem.at[slot]).wait()

    gt = gt_ref[...]
    xo = x2_ref[...] + gt[:, 0:1] * buf[slot, 0] + gt[:, 1:2] * buf[slot, 1]
    ms = jnp.mean(xo * xo, axis=-1, keepdims=True)
    o_ref[...] = xo * lax.rsqrt(ms + RMS_EPS) * nw_ref[...]


def moe_combine(yb, pos0, pos1, x2, gates, nw):
    n, d = x2.shape
    tf = COMBINE_ROWS
    grid_spec = pltpu.PrefetchScalarGridSpec(
        num_scalar_prefetch=2,
        grid=(n // tf,),
        in_specs=[pl.BlockSpec(memory_space=pl.ANY),
                  pl.BlockSpec((tf, d), lambda i, *_: (i, 0)),
                  pl.BlockSpec((tf, LANES), lambda i, *_: (i, 0)),
                  pl.BlockSpec((1, d), lambda i, *_: (0, 0))],
        out_specs=pl.BlockSpec((tf, d), lambda i, *_: (i, 0)),
        scratch_shapes=[pltpu.VMEM((2, 2, tf, d), F32), pltpu.SemaphoreType.DMA((2,))])
    return pl.pallas_call(
        _combine_body,
        grid_spec=grid_spec,
        out_shape=jax.ShapeDtypeStruct((n, d), F32),
        compiler_params=_cparams(("arbitrary",)),
        name="moe_combine",
    )(pos0, pos1, yb, x2, gates, nw.reshape(1, d))


def _dispatch_plan(expert_id):
    n_tok, k = expert_id.shape
    rb = MOE_ROWS
    n_assign = n_tok * k
    e_flat = expert_id.reshape(-1)
    onehot = (e_flat[:, None] == jnp.arange(N_EXPERTS, dtype=I32)[None, :]).astype(I32)
    csum = jnp.cumsum(onehot, axis=0)
    counts = csum[-1]
    padded = (counts + rb - 1) // rb * rb
    pad_end = jnp.cumsum(padded)
    pad_start = pad_end - padded
    pos = jnp.sum(onehot * (pad_start[None, :] + csum - 1), axis=1).astype(I32).reshape(n_tok, k)
    n_rb = -(-n_assign // rb) + N_EXPERTS
    blk_start = jnp.arange(n_rb, dtype=I32) * rb
    blk_expert = jnp.minimum(jnp.sum((pad_end[None, :] <= blk_start[:, None]).astype(I32), axis=1),
                             N_EXPERTS - 1).astype(I32)
    n_used = (pad_end[-1] // rb).astype(I32).reshape(1)
    blk = jnp.arange(n_rb, dtype=I32)
    prev_expert = jnp.concatenate([jnp.full((1,), -1, I32), blk_expert[:-1]])
    first = jnp.logical_and(blk < n_used[0], blk_expert != prev_expert).astype(I32)
    slot = jnp.bitwise_and(jnp.cumsum(first) - 1, 1).astype(I32)
    ids = jnp.arange(N_EXPERTS, dtype=I32)
    later = jnp.logical_and(ids[None, :] > ids[:, None], counts[None, :] > 0)
    next_nonempty = jnp.min(jnp.where(later, ids[None, :], N_EXPERTS), axis=1)
    next_nonempty = jnp.where(next_nonempty < N_EXPERTS, next_nonempty, -1).astype(I32)
    next_e = next_nonempty[blk_expert]
    return pos, blk_expert, n_used, first, slot, next_e, n_rb * rb


def kernel(x, norm_mix_w, w_in, gdn_conv_w, gdn_A_log, gdn_dt_bias, gdn_out_norm_w, moba_out_norm_w, w_out, norm_ffn_w, w_router_group, b_router_group, w_router_expert, b_router_expert, w_expert_gate, w_expert_up, w_expert_down, norm_final_w):
    batch, seq, d = x.shape
    n = batch * seq
    assert w_in.shape[0] == 1, "the final norm is fused into the last layer's combine; one layer supported"
    l = 0
    xf = x.reshape(n, d).astype(F32)
    gw = 4 * GDN_WIDTH
    w_l = w_in[l]
    mq0 = gw + 2 * GDN_HEADS
    col_scale = jnp.where(jnp.arange(3 * MOBA_WIDTH) < MOBA_WIDTH, HEAD_DIM ** -0.5 * LOG2E, 1.0).astype(F32)
    w_gdn = w_l[:, :gw].astype(BF16)
    w_moba = (w_l[:, mq0:] * col_scale[None, :]).astype(BF16)
    w_ba = jnp.pad(w_l[:, gw:gw + 2 * GDN_HEADS], ((0, 0), (0, LANES - 2 * GDN_HEADS))).astype(BF16)
    proj_g, ba = norm_matmul(xf, norm_mix_w[l], w_gdn, BF16, IN_PROJ_ROWS, gw // 2, w_small=w_ba)
    proj_m = norm_matmul(xf, norm_mix_w[l], w_moba, BF16, IN_PROJ_ROWS, 3 * MOBA_WIDTH // 2)
    og = gdn_heads(proj_g, ba, gdn_conv_w[l].astype(F32), gdn_A_log[l], gdn_dt_bias[l],
                   gdn_out_norm_w[l].astype(F32), batch, seq)
    om = moba_heads(proj_m, moba_out_norm_w[l].astype(F32), batch, seq, 0)

    w_o = w_out[l].astype(BF16)
    n_r = N_GROUPS + N_EXPERTS
    w_router = jnp.pad(jnp.concatenate([w_router_group[l], w_router_expert[l]], axis=1).astype(F32),
                       ((0, 0), (0, LANES - n_r)))
    wr_hi = w_router.astype(BF16)
    wr_lo = (w_router - wr_hi.astype(F32)).astype(BF16)
    r_bias = jnp.pad(jnp.concatenate([b_router_group[l], b_router_expert[l]]).astype(F32),
                     (0, LANES - n_r)).reshape(1, LANES)
    x2, hp, gates, ids = out_proj(xf, og, om, w_o[:GDN_WIDTH], w_o[GDN_WIDTH:], norm_ffn_w[l].astype(F32),
                                  wr_hi, wr_lo, r_bias, 512)

    pos, blk_expert, n_used, first, slot, next_e, n_rows = _dispatch_plan(ids[:, :TOPK_IN_GROUP])
    pos0, pos1 = pos[:, 0], pos[:, 1]
    src_rows = moe_source_rows(pos0, pos1, n_rows)
    yb = moe_experts(hp, src_rows, blk_expert, n_used, first, slot, next_e, w_expert_gate[l].astype(F32),
                     w_expert_up[l].astype(F32), w_expert_down[l].astype(F32))
    out = moe_combine(yb, pos0, pos1, x2, gates, norm_final_w.astype(F32))
    return out.reshape(batch, seq, d).astype(x.dtype)
```

```python
import functools
import itertools

import jax
import jax.numpy as jnp
from jax import lax
from jax.experimental import pallas as pl
from jax.experimental.pallas import tpu as pltpu

F32 = jnp.float32
BF16 = jnp.bfloat16
U32 = jnp.uint32
I32 = jnp.int32

HEAD_DIM = 128
GDN_HEADS = 8
MOBA_HEADS = 8
GDN_WIDTH = GDN_HEADS * HEAD_DIM
MOBA_WIDTH = MOBA_HEADS * HEAD_DIM
CONV_WIDTH = 4
GDN_CHUNK = 64
MOBA_BLOCK = 256
MOBA_TOPK = 3
N_GROUPS = 4
EXPERTS_PER_GROUP = 8
N_EXPERTS = N_GROUPS * EXPERTS_PER_GROUP
TOPK_IN_GROUP = 2
RMS_EPS = 1e-6
NEG_INF = -1e30
MOBA_MASK = -(2.0 ** 99)
LOG2E = 1.4426950408889634
LANES = 128
VMEM_LIMIT = 56 * 1024 * 1024

IN_PROJ_ROWS = 1024
GDN_TILE = 256
GDN_HEADS_PER_STEP = 8
MOBA_HEADS_PER_STEP = 4
MOE_ROW_BUFFERS = 3
MOE_ROWS = 256
COMBINE_BUFFERS = 3
COMBINE_ROWS = 128
HI = lax.Precision.HIGHEST


def _cparams(sem, **kw):
    return pltpu.CompilerParams(dimension_semantics=sem, vmem_limit_bytes=VMEM_LIMIT, **kw)


def _dot(a, b):
    return jnp.dot(a, b, preferred_element_type=F32)


def _dot_nt(a, b, precision=None):
    return lax.dot_general(a, b, (((1,), (1,)), ((), ())), preferred_element_type=F32,
                           precision=precision)


def _sigmoid(x):
    return 1.0 / (1.0 + jnp.exp(-x))


def _norm_matmul_body(x_ref, nw_ref, w_ref, *rest):
    x = x_ref[...]
    ms = jnp.mean(x * x, axis=-1, keepdims=True)
    h = (x * lax.rsqrt(ms + RMS_EPS) * nw_ref[...]).astype(BF16)
    if len(rest) == 1:
        rest[0][...] = _dot_nt(h, w_ref[...]).astype(rest[0].dtype)
    else:
        ws_ref, o_ref, os_ref = rest
        o_ref[...] = _dot_nt(h, w_ref[...]).astype(o_ref.dtype)
        os_ref[0] = _dot_nt(h, ws_ref[...])


def norm_matmul(x, nw, w, out_dtype, tm, tn, w_small=None):
    n, d = x.shape
    width = w.shape[0]
    in_specs = [pl.BlockSpec((tm, d), lambda j, i: (i, 0)),
                pl.BlockSpec((1, d), lambda j, i: (0, 0)),
                pl.BlockSpec((tn, d), lambda j, i: (j, 0))]
    out_specs = pl.BlockSpec((tm, tn), lambda j, i: (i, j))
    out_shape = jax.ShapeDtypeStruct((n, width), out_dtype)
    operands = (x, nw.reshape(1, d), w)
    if w_small is not None:
        ws = w_small.shape[0]
        in_specs.append(pl.BlockSpec((ws, d), lambda j, i: (0, 0)))
        out_specs = [out_specs, pl.BlockSpec((1, tm, ws), lambda j, i: (j, i, 0))]
        out_shape = [out_shape, jax.ShapeDtypeStruct((width // tn, n, ws), F32)]
        operands = operands + (w_small,)
    outs = pl.pallas_call(
        _norm_matmul_body,
        grid=(width // tn, n // tm),
        in_specs=in_specs,
        out_specs=out_specs,
        out_shape=out_shape,
        compiler_params=_cparams(("arbitrary", "arbitrary")),
        name="norm_in_proj",
    )(*operands)
    if w_small is None:
        return outs
    return outs[0], outs[1][0]


def _gdn_body(alog_ref, dtb_ref, q_ref, k_ref, v_ref, z_ref, ba_ref, cwq_ref, cwk_ref, cwv_ref,
              onw_ref, o_ref, *scratch):
    tt = GDN_TILE
    c = GDN_CHUNK
    t = pl.program_id(2)

    @pl.when(t == 0)
    def _():
        for hh in range(GDN_HEADS_PER_STEP):
            s_ref, prev_ref, vn_ref, _ = scratch[4 * hh:4 * hh + 4]
            s_ref[...] = jnp.zeros_like(s_ref)
            prev_ref[...] = jnp.zeros_like(prev_ref)
            vn_ref[...] = jnp.zeros_like(vn_ref)

    row_8 = lax.broadcasted_iota(jnp.int32, (8, LANES), 0)
    lane_t = lax.broadcasted_iota(jnp.int32, (tt, LANES), 1)
    row = lax.broadcasted_iota(jnp.int32, (tt, tt), 0)
    col = lax.broadcasted_iota(jnp.int32, (tt, tt), 1)
    same = (row // c) == (col // c)
    causal = jnp.logical_and(same, col <= row)
    strict = jnp.logical_and(same, col < row)
    causal16 = jnp.where(causal, 1.0, 0.0).astype(BF16)
    same16 = jnp.where(same, 1.0, 0.0).astype(BF16)
    eye = jnp.where(row == col, 1.0, 0.0)
    colk = lax.broadcasted_iota(jnp.int32, (HEAD_DIM, tt), 1)
    ba = ba_ref[...]

    heads = [_gdn_one_head(hh, pl.program_id(1) * GDN_HEADS_PER_STEP + hh, alog_ref, dtb_ref, q_ref, k_ref,
                           v_ref, z_ref, ba, cwq_ref, cwk_ref, cwv_ref, onw_ref, o_ref,
                           *scratch[4 * hh:4 * hh + 4],
                           row_8, lane_t, causal, strict, causal16, same16, eye, colk)
             for hh in range(GDN_HEADS_PER_STEP)]
    for _ in itertools.zip_longest(*heads):
        pass


def _gdn_one_head(hh, h, alog_ref, dtb_ref, q_ref, k_ref, v_ref, z_ref, ba, cwq_ref, cwk_ref, cwv_ref,
                  onw_ref, o_ref, s_ref, prev_ref, vn_ref, oacc_ref,
                  row_8, lane_t, causal, strict, causal16, same16, eye, colk):
    tt = GDN_TILE
    c = GDN_CHUNK
    lanes = slice(hh * HEAD_DIM, (hh + 1) * HEAD_DIM)

    def conv_silu(x_ref, cw_ref, idx):
        x = x_ref[:, lanes].astype(F32)
        p = prev_ref[idx]
        w = cw_ref[:, lanes]
        acc = x * w[CONV_WIDTH - 1:CONV_WIDTH, :]
        for s in range(1, CONV_WIDTH):
            xs = pltpu.roll(x, s, axis=0)
            head = jnp.where(row_8 < s, pltpu.roll(p, s, axis=0), xs[:8])
            xs = jnp.concatenate([head, xs[8:]], axis=0)
            acc = acc + xs * w[CONV_WIDTH - 1 - s:CONV_WIDTH - s, :]
        prev_ref[idx] = x[tt - 8:tt, :]
        return acc * _sigmoid(acc)

    q = conv_silu(q_ref, cwq_ref, 0)
    k = conv_silu(k_ref, cwk_ref, 1)
    v = conv_silu(v_ref, cwv_ref, 2)
    q = q * lax.rsqrt(jnp.sum(q * q, axis=-1, keepdims=True) + 1e-6) * (HEAD_DIM ** -0.5)
    k = k * lax.rsqrt(jnp.sum(k * k, axis=-1, keepdims=True) + 1e-6)
    yield

    b_col = jnp.sum(jnp.where(lane_t == h, ba, 0.0), axis=-1, keepdims=True)
    a_col = jnp.sum(jnp.where(lane_t == h + GDN_HEADS, ba, 0.0), axis=-1, keepdims=True)
    beta = _sigmoid(b_col)
    xa = a_col + dtb_ref[h]
    softplus = jnp.maximum(xa, 0.0) + jnp.log(1.0 + jnp.exp(-jnp.abs(xa)))
    g = -jnp.exp(jnp.full((1, 1), alog_ref[h], F32)) * softplus

    g_hi = g.astype(BF16).astype(F32)
    g_mid = (g - g_hi).astype(BF16).astype(F32)
    g_lo = g - g_hi - g_mid
    g3 = jnp.where(lane_t == 0, g_hi, jnp.where(lane_t == 1, g_mid, jnp.where(lane_t == 2, g_lo, 0.0)))
    g3 = g3.astype(BF16)

    def three(r):
        return r[:, 0:1] + r[:, 1:2] + r[:, 2:3]

    gc_col = three(_dot(causal16, g3))
    gc_b = jnp.broadcast_to(gc_col, (tt, LANES))
    glast_b = jnp.broadcast_to(three(_dot(same16, g3)), (tt, LANES))
    gc_row = jnp.transpose(gc_b)[0:1, :]
    decay = jnp.exp(jnp.where(causal, gc_col - gc_row, NEG_INF))
    yield

    kb = k * beta
    k16 = k.astype(BF16)
    kk = _dot_nt(kb.astype(BF16), k16)
    lmat = jnp.where(strict, kk * decay, 0.0)
    yield
    attn = _dot_nt(q.astype(BF16), k16) * decay

    tinv = eye - lmat
    m16 = lmat.astype(BF16)
    for _ in range(5):
        m16 = _dot(m16, m16).astype(BF16)
        yield
        tinv = tinv + _dot(tinv.astype(BF16), m16)
        yield

    egc = jnp.exp(gc_b)
    rhs = jnp.concatenate([v * beta, kb * egc], axis=-1)
    sol = _dot(tinv.astype(BF16), rhs.astype(BF16))
    yield
    u = sol[:, :HEAD_DIM]
    w16 = sol[:, HEAD_DIM:].astype(BF16)
    qd16 = (q * egc).astype(BF16)
    kd = k * jnp.exp(glast_b - gc_b)
    kdt = jnp.transpose(kd)
    gtot_b = jnp.exp(glast_b)
    attn16 = attn.astype(BF16)

    s = s_ref[...]
    for n in range(tt // c):
        sl = slice(n * c, (n + 1) * c)
        s16 = s.astype(BF16)
        v_new = u[sl] - _dot(w16[sl], s16)
        vn_ref[sl, :] = v_new
        yield
        vn16 = vn_ref[...].astype(BF16)
        oacc_ref[sl, :] = _dot(qd16[sl], s16) + _dot(attn16[sl], vn16)
        kdt_n = jnp.where((colk // c) == n, kdt, 0.0).astype(BF16)
        s = s * gtot_b[n * c:n * c + 1, :] + _dot(kdt_n, vn16)
        yield
    s_ref[...] = s

    o = oacc_ref[...]
    o = o * lax.rsqrt(jnp.mean(o * o, axis=-1, keepdims=True) + RMS_EPS) * onw_ref[...]
    z = z_ref[:, lanes].astype(F32)
    o_ref[:, lanes] = (o * (z * _sigmoid(z))).astype(o_ref.dtype)


def gdn_heads(proj, ba, conv_w, a_log, dt_bias, out_norm_w, batch, seq):
    n = batch * seq
    tt = GDN_TILE
    nt = seq // tt
    hps = GDN_HEADS_PER_STEP
    ng = GDN_HEADS // hps
    wide = hps * HEAD_DIM

    def col_spec(section):
        return pl.BlockSpec((tt, wide), lambda b, h, t, *_: (b * nt + t, section * ng + h))

    def cw_spec(section):
        return pl.BlockSpec((CONV_WIDTH, wide), lambda b, h, t, *_: (0, section * ng + h))

    grid_spec = pltpu.PrefetchScalarGridSpec(
        num_scalar_prefetch=2,
        grid=(batch, ng, nt),
        in_specs=[col_spec(0), col_spec(1), col_spec(2), col_spec(3),
                  pl.BlockSpec((tt, LANES), lambda b, h, t, *_: (b * nt + t, 0)),
                  cw_spec(0), cw_spec(1), cw_spec(2),
                  pl.BlockSpec((1, HEAD_DIM), lambda b, h, t, *_: (0, 0))],
        out_specs=pl.BlockSpec((tt, wide), lambda b, h, t, *_: (b * nt + t, h)),
        scratch_shapes=[pltpu.VMEM((HEAD_DIM, HEAD_DIM), F32),
                        pltpu.VMEM((3, 8, HEAD_DIM), F32),
                        pltpu.VMEM((tt, HEAD_DIM), F32),
                        pltpu.VMEM((tt, HEAD_DIM), F32)] * hps)
    return pl.pallas_call(
        _gdn_body,
        grid_spec=grid_spec,
        out_shape=jax.ShapeDtypeStruct((n, GDN_WIDTH), BF16),
        compiler_params=_cparams(("arbitrary", "arbitrary", "arbitrary")),
        name="gdn_heads",
    )(a_log.astype(F32), dt_bias.astype(F32), proj, proj, proj, proj, ba,
      conv_w, conv_w, conv_w, out_norm_w.reshape(1, HEAD_DIM))


def _moba_body(slope_ref, q_ref, k_ref, v_ref, onw_ref, o_ref, *scratch, n_blocks):
    blk = MOBA_BLOCK
    tq = 2 * blk
    nb = n_blocks
    nbp = -(-nb // 8) * 8
    hps = MOBA_HEADS_PER_STEP
    cq = pl.program_id(2)
    c0 = 2 * cq
    lane = lax.broadcasted_iota(I32, (blk, LANES), 1)
    row = lax.broadcasted_iota(I32, (blk, blk), 0)
    col = lax.broadcasted_iota(I32, (blk, blk), 1)
    rid = lax.broadcasted_iota(I32, (LANES, tq), 0)
    bid = lax.broadcasted_iota(I32, (nbp, tq), 0)
    c_row = c0 + (lax.broadcasted_iota(I32, (nbp, tq), 1) >= blk).astype(I32)
    ones_ext_t = jnp.where(jnp.logical_and(rid >= nb, rid < nb + 3), 1.0, 0.0).astype(BF16)

    def head_lanes(hh):
        return slice(hh * HEAD_DIM, (hh + 1) * HEAD_DIM)

    def keys(hh, j, nblk=1):
        return scratch[3 * hh + 1][pl.ds(pl.multiple_of(j * blk, blk), nblk * blk), :]

    def values_t(hh, j, nblk=1):
        return scratch[3 * hh + 2][:, pl.ds(pl.multiple_of(j * blk, blk), nblk * blk)]

    @pl.when(cq == 0)
    def _():
        t_in = lax.broadcasted_iota(I32, (blk, LANES), 0)
        for hh in range(hps):
            kmean_ref, kx_ref, vt_ref = scratch[3 * hh:3 * hh + 3]
            slope2 = slope_ref[pl.program_id(1) * hps + hh] * LOG2E
            kmean_ref[...] = jnp.zeros_like(kmean_ref)
            for j in range(nb):
                kj = k_ref[j * blk:(j + 1) * blk, head_lanes(hh)]
                kmean_ref[j:j + 1, :] = jnp.mean(kj.astype(F32), axis=0, keepdims=True)
                bias = slope2 * (t_in + j * blk).astype(F32)
                b_hi = bias.astype(BF16)
                r1 = bias - b_hi.astype(F32)
                b_mid = r1.astype(BF16)
                b_lo = (r1 - b_mid.astype(F32)).astype(BF16)
                ext = jnp.where(lane == j, 1.0, 0.0).astype(BF16)
                ext = jnp.where(lane == nb, b_hi, ext)
                ext = jnp.where(lane == nb + 1, b_mid, ext)
                ext = jnp.where(lane == nb + 2, b_lo, ext)
                kx_ref[j * blk:(j + 1) * blk, :] = jnp.concatenate([kj, ext], axis=1)
                vj = v_ref[j * blk:(j + 1) * blk, head_lanes(hh)].astype(F32)
                vt_ref[:, j * blk:(j + 1) * blk] = jnp.transpose(vj).astype(BF16)

    def start(s, vals_t):
        m = jnp.max(s, axis=0, keepdims=True)
        p = jnp.exp2(s - m)
        return m, jnp.sum(p, axis=0, keepdims=True), _dot(vals_t, p.astype(BF16))

    def update(carry, s, vals_t):
        m_i, l_i, acc = carry
        m_new = jnp.maximum(m_i, jnp.max(s, axis=0, keepdims=True))
        alpha = jnp.exp2(m_i - m_new)
        p = jnp.exp2(s - m_new)
        return (m_new, l_i * alpha + jnp.sum(p, axis=0, keepdims=True),
                acc * alpha + _dot(vals_t, p.astype(BF16)))

    qx = [None] * hps
    carry0 = [None] * hps

    def prologue(hh):
        q_t = jnp.transpose(q_ref[:, head_lanes(hh)].astype(F32))
        q_t16 = q_t.astype(BF16)
        yield
        qx0_t = jnp.concatenate([q_t16, ones_ext_t], axis=0)
        state = []
        for half in range(2):
            s_own = _dot(keys(hh, c0 + half), qx0_t[:, half * blk:(half + 1) * blk])
            s_own = jnp.where(row <= col, s_own, MOBA_MASK)
            yield
            state.append(start(s_own, values_t(hh, c0 + half)))
            yield
        gate_t = jnp.dot(scratch[3 * hh][...], q_t, preferred_element_type=F32, precision=HI)[:nbp]
        gm = jnp.where(bid < c_row, gate_t, NEG_INF)
        sel = jnp.zeros((nbp, tq), F32)
        for s in range(MOBA_TOPK):
            mx = jnp.max(gm, axis=0, keepdims=True)
            idx = jnp.min(jnp.where(gm == mx, bid, LANES), axis=0, keepdims=True)
            pick = bid == idx
            sel = jnp.where(jnp.logical_and(pick, s < c_row), 1.0, sel)
            gm = jnp.where(pick, -3e38, gm)
        mask_t = jnp.where(sel > 0.5, 0.0, MOBA_MASK)
        ext_t = jnp.concatenate([mask_t, jnp.zeros((LANES - nbp, tq), F32)], axis=0)
        ext_t = jnp.where(rid < nb, ext_t, jnp.where(rid < nb + 3, 1.0, 0.0))
        qx[hh] = jnp.concatenate([q_t16, ext_t.astype(BF16)], axis=0)
        yield
        s_c0 = _dot(keys(hh, c0), qx[hh][:, blk:])
        yield
        second = update(state[1], s_c0, values_t(hh, c0))
        carry0[hh] = tuple(jnp.concatenate([a, b], axis=1) for a, b in zip(state[0], second))

    for _ in itertools.zip_longest(*[prologue(hh) for hh in range(hps)]):
        pass

    def body(p, carries):
        scores = [_dot(keys(hh, 2 * p, 2), qx[hh]) for hh in range(hps)]
        return tuple(update(carries[hh], scores[hh], values_t(hh, 2 * p, 2)) for hh in range(hps))

    final = lax.fori_loop(0, cq, body, tuple(carry0))
    for hh in range(hps):
        _, l_f, acc_f = final[hh]
        o_t = acc_f / l_f
        o_t = o_t * lax.rsqrt(jnp.mean(o_t * o_t, axis=0, keepdims=True) + RMS_EPS)
        o_ref[:, head_lanes(hh)] = (jnp.transpose(o_t) * onw_ref[...]).astype(o_ref.dtype)


def moba_heads(proj, out_norm_w, batch, seq, col_off):
    n = batch * seq
    blk = MOBA_BLOCK
    nb = seq // blk
    nq = nb // 2
    hps = MOBA_HEADS_PER_STEP
    ng = MOBA_HEADS // hps
    wide = hps * HEAD_DIM
    sec0 = col_off // hps
    slopes = jnp.exp2(-8.0 * jnp.arange(1, MOBA_HEADS + 1, dtype=F32) / MOBA_HEADS)
    grid_spec = pltpu.PrefetchScalarGridSpec(
        num_scalar_prefetch=1,
        grid=(batch, ng, nq),
        in_specs=[pl.BlockSpec((2 * blk, wide), lambda b, h, c, *_: (b * nq + c, sec0 + h)),
                  pl.BlockSpec((seq, wide), lambda b, h, c, *_: (b, sec0 + ng + h)),
                  pl.BlockSpec((seq, wide), lambda b, h, c, *_: (b, sec0 + 2 * ng + h)),
                  pl.BlockSpec((1, HEAD_DIM), lambda b, h, c, *_: (0, 0))],
        out_specs=pl.BlockSpec((2 * blk, wide), lambda b, h, c, *_: (b * nq + c, h)),
        scratch_shapes=[pltpu.VMEM((LANES, HEAD_DIM), F32),
                        pltpu.VMEM((nb * blk, HEAD_DIM + LANES), BF16),
                        pltpu.VMEM((HEAD_DIM, nb * blk), BF16)] * hps)
    assert nb % 2 == 0 and nb + 3 <= LANES and col_off % hps == 0
    return pl.pallas_call(
        functools.partial(_moba_body, n_blocks=nb),
        grid_spec=grid_spec,
        out_shape=jax.ShapeDtypeStruct((n, MOBA_WIDTH), BF16),
        compiler_params=_cparams(("arbitrary", "arbitrary", "arbitrary")),
        name="moba_heads",
    )(slopes, proj, proj, proj, out_norm_w.reshape(1, HEAD_DIM))


def _out_proj_body(x_ref, og_ref, om_ref, wg_ref, wm_ref, nw_ref, wrh_ref, wrl_ref, rb_ref,
                   x2_ref, hp_ref, gt_ref, id_ref):
    tm = x_ref.shape[0]
    x2 = x_ref[...] + _dot(og_ref[...], wg_ref[...]) + _dot(om_ref[...], wm_ref[...])
    x2_ref[...] = x2
    ms = jnp.mean(x2 * x2, axis=-1, keepdims=True)
    h2 = x2 * lax.rsqrt(ms + RMS_EPS) * nw_ref[...]
    h_hi = h2.astype(BF16)
    hp_ref[...] = h2
    h_lo = (h2 - h_hi.astype(F32)).astype(BF16)
    lg = _dot(h_hi, wrh_ref[...]) + _dot(h_hi, wrl_ref[...]) + _dot(h_lo, wrh_ref[...]) + rb_ref[...]

    lane = lax.broadcasted_iota(I32, (tm, LANES), 1)
    is_g = lane < N_GROUPS
    mg = jnp.max(jnp.where(is_g, lg, NEG_INF), axis=-1, keepdims=True)
    g_idx = jnp.min(jnp.where(jnp.logical_and(is_g, lg == mg), lane, LANES), axis=-1, keepdims=True)
    sum_g = jnp.sum(jnp.where(is_g, jnp.exp(lg - mg), 0.0), axis=-1, keepdims=True)
    p_top_g = 1.0 / sum_g
    lo = N_GROUPS + g_idx * EXPERTS_PER_GROUP
    in_grp = jnp.logical_and(lane >= lo, lane < lo + EXPERTS_PER_GROUP)
    m1 = jnp.max(jnp.where(in_grp, lg, NEG_INF), axis=-1, keepdims=True)
    i1 = jnp.min(jnp.where(jnp.logical_and(in_grp, lg == m1), lane, LANES), axis=-1, keepdims=True)
    rest = jnp.logical_and(in_grp, lane != i1)
    m2 = jnp.max(jnp.where(rest, lg, NEG_INF), axis=-1, keepdims=True)
    i2 = jnp.min(jnp.where(jnp.logical_and(rest, lg == m2), lane, LANES), axis=-1, keepdims=True)
    e2 = jnp.exp(m2 - m1)
    gate1 = p_top_g / (1.0 + e2)
    gate2 = p_top_g * e2 / (1.0 + e2)
    gt_ref[...] = jnp.where(lane == 0, gate1, jnp.where(lane == 1, gate2, 0.0))
    id_ref[...] = jnp.where(lane == 0, i1 - N_GROUPS, jnp.where(lane == 1, i2 - N_GROUPS, 0))


def out_proj(x, og, om, w_g, w_m, nw, wr_hi, wr_lo, r_bias, tm):
    n, d = x.shape
    const = lambda i: (0, 0)
    rows = lambda i: (i, 0)
    return pl.pallas_call(
        _out_proj_body,
        grid=(n // tm,),
        in_specs=[pl.BlockSpec((tm, d), rows),
                  pl.BlockSpec((tm, GDN_WIDTH), rows),
                  pl.BlockSpec((tm, MOBA_WIDTH), rows),
                  pl.BlockSpec((GDN_WIDTH, d), const),
                  pl.BlockSpec((MOBA_WIDTH, d), const),
                  pl.BlockSpec((1, d), const),
                  pl.BlockSpec((d, LANES), const),
                  pl.BlockSpec((d, LANES), const),
                  pl.BlockSpec((1, LANES), const)],
        out_specs=[pl.BlockSpec((tm, d), rows),
                   pl.BlockSpec((tm, d), rows),
                   pl.BlockSpec((tm, LANES), rows),
                   pl.BlockSpec((tm, LANES), rows)],
        out_shape=[jax.ShapeDtypeStruct((n, d), F32),
                   jax.ShapeDtypeStruct((n, d), F32),
                   jax.ShapeDtypeStruct((n, LANES), F32),
                   jax.ShapeDtypeStruct((n, LANES), I32)],
        compiler_params=_cparams(("arbitrary",)),
        name="out_proj_router",
    )(x, og, om, w_g, w_m, nw.reshape(1, d), wr_hi, wr_lo, r_bias)


def _invert_body(pos0_ref, pos1_ref, src_ref, zeros_vmem, sem):
    n_tok = pos0_ref.shape[0]
    zeros_vmem[...] = jnp.zeros_like(zeros_vmem)
    clear = pltpu.make_async_copy(zeros_vmem, src_ref, sem)
    clear.start()
    clear.wait()

    def fill(t, carry):
        src_ref[pos0_ref[t]] = t
        src_ref[pos1_ref[t]] = t
        return carry

    lax.fori_loop(0, n_tok, fill, 0, unroll=8)


def moe_source_rows(pos0, pos1, n_rows):
    grid_spec = pltpu.PrefetchScalarGridSpec(
        num_scalar_prefetch=2,
        grid=(1,),
        in_specs=[],
        out_specs=pl.BlockSpec(memory_space=pltpu.SMEM),
        scratch_shapes=[pltpu.VMEM((n_rows,), I32), pltpu.SemaphoreType.DMA(())])
    return pl.pallas_call(
        _invert_body,
        grid_spec=grid_spec,
        out_shape=jax.ShapeDtypeStruct((n_rows,), I32),
        compiler_params=_cparams(("arbitrary",)),
        name="moe_source_rows",
    )(pos0, pos1)


def _moe_body(blk_e_ref, nblk_ref, first_ref, slot_ref, next_e_ref, src_ref, h_hbm, wg_hbm, wu_hbm, wd_hbm,
              y_ref, xbuf, wg_buf, wu_buf, wd_buf, wg16, wu16, wd16, sem, gsem):
    rb = MOE_ROWS
    i = pl.program_id(0)
    e = blk_e_ref[i]
    slot = slot_ref[i]
    n_used = nblk_ref[0]
    nbuf = MOE_ROW_BUFFERS
    ahead = nbuf - 1
    xslot = lax.rem(i, nbuf)

    def row_copy(block, r):
        s = lax.rem(block, nbuf)
        tok = src_ref[block * rb + r]
        return pltpu.make_async_copy(h_hbm.at[pl.ds(tok, 1)], xbuf.at[s, pl.ds(r, 1)], gsem.at[s])

    def wait_rows(s):
        pltpu.make_async_copy(h_hbm.at[pl.ds(0, rb)], xbuf.at[s], gsem.at[s]).wait()

    @pl.when(i == 0)
    def _():
        for b in range(ahead):
            @pl.when(b < n_used)
            def _(b=b):
                def issue(r, carry):
                    row_copy(b, r).start()
                    return carry
                lax.fori_loop(0, rb, issue, 0, unroll=8)

    def weight_copies(expert, s):
        return (pltpu.make_async_copy(wg_hbm.at[expert], wg_buf.at[s], sem.at[s]),
                pltpu.make_async_copy(wu_hbm.at[expert], wu_buf.at[s], sem.at[s]),
                pltpu.make_async_copy(wd_hbm.at[expert], wd_buf.at[s], sem.at[s]))

    @pl.when(i == 0)
    def _():
        for cp in weight_copies(e, slot):
            cp.start(priority=1)

    @pl.when(first_ref[i] == 1)
    def _():
        for cp in weight_copies(e, slot):
            cp.wait()

        @pl.when(next_e_ref[i] >= 0)
        def _():
            for cp in weight_copies(next_e_ref[i], 1 - slot):
                cp.start(priority=1)

        wg16[...] = wg_buf[slot].astype(BF16)
        wu16[...] = wu_buf[slot].astype(BF16)
        wd16[...] = wd_buf[slot].astype(BF16)

    def compute(xb):
        g = _dot(xb, wg16[...])
        u = _dot(xb, wu16[...])
        hm = (g * _sigmoid(g) * u).astype(BF16)
        y_ref[...] = _dot(hm, wd16[...])

    @pl.when(i < n_used - ahead)
    def _():
        wait_rows(xslot)
        xb = xbuf[xslot].astype(BF16)
        for r in range(rb):
            row_copy(i + ahead, r).start()
        compute(xb)

    @pl.when(jnp.logical_and(i >= n_used - ahead, i < n_used))
    def _():
        wait_rows(xslot)
        compute(xbuf[xslot].astype(BF16))

    @pl.when(i >= n_used)
    def _():
        y_ref[...] = jnp.zeros_like(y_ref)


def moe_experts(h2, src_rows, blk_expert, n_used, first, slot, next_e, w_gate, w_up, w_down):
    rb = MOE_ROWS
    n_rows = src_rows.shape[0]
    d = h2.shape[1]
    de = w_gate.shape[-1]
    grid_spec = pltpu.PrefetchScalarGridSpec(
        num_scalar_prefetch=6,
        grid=(n_rows // rb,),
        in_specs=[pl.BlockSpec(memory_space=pl.ANY),
                  pl.BlockSpec(memory_space=pl.ANY),
                  pl.BlockSpec(memory_space=pl.ANY),
                  pl.BlockSpec(memory_space=pl.ANY)],
        out_specs=pl.BlockSpec((rb, d), lambda i, *_: (i, 0)),
        scratch_shapes=[pltpu.VMEM((MOE_ROW_BUFFERS, rb, d), F32),
                        pltpu.VMEM((2, d, de), F32), pltpu.VMEM((2, d, de), F32), pltpu.VMEM((2, de, d), F32),
                        pltpu.VMEM((d, de), BF16), pltpu.VMEM((d, de), BF16), pltpu.VMEM((de, d), BF16),
                        pltpu.SemaphoreType.DMA((2,)), pltpu.SemaphoreType.DMA((MOE_ROW_BUFFERS,))])
    return pl.pallas_call(
        _moe_body,
        grid_spec=grid_spec,
        out_shape=jax.ShapeDtypeStruct((n_rows, d), F32),
        compiler_params=_cparams(("arbitrary",)),
        name="moe_experts",
    )(blk_expert, n_used, first, slot, next_e, src_rows, h2, w_gate, w_up, w_down)


def _combine_body(pos0_ref, pos1_ref, y_hbm, x2_ref, gt_ref, nw_ref, o_ref, buf, sem):
    tf = COMBINE_ROWS
    nbuf = COMBINE_BUFFERS
    ahead = nbuf - 1
    i = pl.program_id(0)
    n_steps = pl.num_programs(0)
    slot = lax.rem(i, nbuf)

    def start_row(step, r):
        sl = lax.rem(step, nbuf)
        t = step * tf + r
        pltpu.make_async_copy(y_hbm.at[pl.ds(pos0_ref[t], 1)], buf.at[sl, 0, pl.ds(r, 1)],
                              sem.at[sl]).start()
        pltpu.make_async_copy(y_hbm.at[pl.ds(pos1_ref[t], 1)], buf.at[sl, 1, pl.ds(r, 1)],
                              sem.at[sl]).start(priority=1)

    @pl.when(i == 0)
    def _():
        for b in range(ahead):
            def body(r, carry, b=b):
                start_row(b, r)
                return carry
            lax.fori_loop(0, tf, body, 0, unroll=8)

    def wait_rows():
        for kk in range(TOPK_IN_GROUP):
            pltpu.make_async_copy(y_hbm.at[pl.ds(0, tf)], buf.at[slot, kk], sem.at[slot]).wait()

    def finish(y0, y1):
        gt = gt_ref[...]
        xo = x2_ref[...] + gt[:, 0:1] * y0 + gt[:, 1:2] * y1
        ms = jnp.mean(xo * xo, axis=-1, keepdims=True)
        o_ref[...] = xo * lax.rsqrt(ms + RMS_EPS) * nw_ref[...]

    @pl.when(i + ahead < n_steps)
    def _():
        wait_rows()
        y0 = buf[slot, 0]
        y1 = buf[slot, 1]
        for r in range(tf):
            start_row(i + ahead, r)
        finish(y0, y1)

    @pl.when(i + ahead >= n_steps)
    def _():
        wait_rows()
        finish(buf[slot, 0], buf[slot, 1])


def moe_combine(yb, pos0, pos1, x2, gates, nw):
    n, d = x2.shape
    tf = COMBINE_ROWS
    grid_spec = pltpu.PrefetchScalarGridSpec(
        num_scalar_prefetch=2,
        grid=(n // tf,),
        in_specs=[pl.BlockSpec(memory_space=pl.ANY),
                  pl.BlockSpec((tf, d), lambda i, *_: (i, 0)),
                  pl.BlockSpec((tf, LANES), lambda i, *_: (i, 0)),
                  pl.BlockSpec((1, d), lambda i, *_: (0, 0))],
        out_specs=pl.BlockSpec((tf, d), lambda i, *_: (i, 0)),
        scratch_shapes=[pltpu.VMEM((COMBINE_BUFFERS, 2, tf, d), F32), pltpu.SemaphoreType.DMA((COMBINE_BUFFERS,))])
    return pl.pallas_call(
        _combine_body,
        grid_spec=grid_spec,
        out_shape=jax.ShapeDtypeStruct((n, d), F32),
        compiler_params=_cparams(("arbitrary",)),
        name="moe_combine",
    )(pos0, pos1, yb, x2, gates, nw.reshape(1, d))


def _dispatch_plan(expert_id):
    n_tok, k = expert_id.shape
    rb = MOE_ROWS
    n_assign = n_tok * k
    e_flat = expert_id.reshape(-1)
    onehot = (e_flat[:, None] == jnp.arange(N_EXPERTS, dtype=I32)[None, :]).astype(I32)
    csum = jnp.cumsum(onehot, axis=0)
    counts = csum[-1]
    padded = (counts + rb - 1) // rb * rb
    pad_end = jnp.cumsum(padded)
    pad_start = pad_end - padded
    pos = jnp.sum(onehot * (pad_start[None, :] + csum - 1), axis=1).astype(I32).reshape(n_tok, k)
    n_rb = -(-n_assign // rb) + N_EXPERTS
    blk_start = jnp.arange(n_rb, dtype=I32) * rb
    blk_expert = jnp.minimum(jnp.sum((pad_end[None, :] <= blk_start[:, None]).astype(I32), axis=1),
                             N_EXPERTS - 1).astype(I32)
    n_used = (pad_end[-1] // rb).astype(I32).reshape(1)
    blk = jnp.arange(n_rb, dtype=I32)
    prev_expert = jnp.concatenate([jnp.full((1,), -1, I32), blk_expert[:-1]])
    first = jnp.logical_and(blk < n_used[0], blk_expert != prev_expert).astype(I32)
    slot = jnp.bitwise_and(jnp.cumsum(first) - 1, 1).astype(I32)
    ids = jnp.arange(N_EXPERTS, dtype=I32)
    later = jnp.logical_and(ids[None, :] > ids[:, None], counts[None, :] > 0)
    next_nonempty = jnp.min(jnp.where(later, ids[None, :], N_EXPERTS), axis=1)
    next_nonempty = jnp.where(next_nonempty < N_EXPERTS, next_nonempty, -1).astype(I32)
    next_e = next_nonempty[blk_expert]
    return pos, blk_expert, n_used, first, slot, next_e, n_rb * rb


def kernel(x, norm_mix_w, w_in, gdn_conv_w, gdn_A_log, gdn_dt_bias, gdn_out_norm_w, moba_out_norm_w, w_out, norm_ffn_w, w_router_group, b_router_group, w_router_expert, b_router_expert, w_expert_gate, w_expert_up, w_expert_down, norm_final_w):
    batch, seq, d = x.shape
    n = batch * seq
    assert w_in.shape[0] == 1, "the final norm is fused into the last layer's combine; one layer supported"
    l = 0
    xf = x.reshape(n, d).astype(F32)
    gw = 4 * GDN_WIDTH
    w_l = w_in[l]
    mq0 = gw + 2 * GDN_HEADS
    col_scale = jnp.where(jnp.arange(3 * MOBA_WIDTH) < MOBA_WIDTH, HEAD_DIM ** -0.5 * LOG2E, 1.0).astype(F32)
    w_t = jnp.swapaxes(w_l, 0, 1)
    w_gdn = w_t[:gw].astype(BF16)
    w_moba = (w_t[mq0:] * col_scale[:, None]).astype(BF16)
    w_ba = jnp.pad(w_t[gw:gw + 2 * GDN_HEADS], ((0, LANES - 2 * GDN_HEADS), (0, 0))).astype(BF16)
    proj_g, ba = norm_matmul(xf, norm_mix_w[l], w_gdn, BF16, IN_PROJ_ROWS, gw // 2, w_small=w_ba)
    proj_m = norm_matmul(xf, norm_mix_w[l], w_moba, BF16, IN_PROJ_ROWS, 3 * MOBA_WIDTH // 2)
    og = gdn_heads(proj_g, ba, gdn_conv_w[l].astype(F32), gdn_A_log[l], gdn_dt_bias[l],
                   gdn_out_norm_w[l].astype(F32), batch, seq)
    om = moba_heads(proj_m, moba_out_norm_w[l].astype(F32), batch, seq, 0)

    w_o = w_out[l].astype(BF16)
    n_r = N_GROUPS + N_EXPERTS
    w_router = jnp.pad(jnp.concatenate([w_router_group[l], w_router_expert[l]], axis=1).astype(F32),
                       ((0, 0), (0, LANES - n_r)))
    wr_hi = w_router.astype(BF16)
    wr_lo = (w_router - wr_hi.astype(F32)).astype(BF16)
    r_bias = jnp.pad(jnp.concatenate([b_router_group[l], b_router_expert[l]]).astype(F32),
                     (0, LANES - n_r)).reshape(1, LANES)
    x2, hp, gates, ids = out_proj(xf, og, om, w_o[:GDN_WIDTH], w_o[GDN_WIDTH:], norm_ffn_w[l].astype(F32),
                                  wr_hi, wr_lo, r_bias, 512)

    pos, blk_expert, n_used, first, slot, next_e, n_rows = _dispatch_plan(ids[:, :TOPK_IN_GROUP])
    pos0, pos1 = pos[:, 0], pos[:, 1]
    src_rows = moe_source_rows(pos0, pos1, n_rows)
    yb = moe_experts(hp, src_rows, blk_expert, n_used, first, slot, next_e, w_expert_gate[l].astype(F32),
                     w_expert_up[l].astype(F32), w_expert_down[l].astype(F32))
    out = moe_combine(yb, pos0, pos1, x2, gates, norm_final_w.astype(F32))
    return out.reshape(batch, seq, d).astype(x.dtype)
```

```python
import functools
import itertools

import jax
import jax.numpy as jnp
from jax import lax
from jax.experimental import pallas as pl
from jax.experimental.pallas import tpu as pltpu

F32 = jnp.float32
BF16 = jnp.bfloat16
U32 = jnp.uint32
I32 = jnp.int32

HEAD_DIM = 128
GDN_HEADS = 8
MOBA_HEADS = 8
GDN_WIDTH = GDN_HEADS * HEAD_DIM
MOBA_WIDTH = MOBA_HEADS * HEAD_DIM
CONV_WIDTH = 4
GDN_CHUNK = 64
MOBA_BLOCK = 256
MOBA_TOPK = 3
N_GROUPS = 4
EXPERTS_PER_GROUP = 8
N_EXPERTS = N_GROUPS * EXPERTS_PER_GROUP
TOPK_IN_GROUP = 2
RMS_EPS = 1e-6
NEG_INF = -1e30
MOBA_MASK = -(2.0 ** 99)
LOG2E = 1.4426950408889634
LANES = 128
VMEM_LIMIT = 56 * 1024 * 1024

IN_PROJ_ROWS = 1024
GDN_TILE = 256
GDN_HEADS_PER_STEP = 8
MOBA_HEADS_PER_STEP = 4
MOE_ROW_BUFFERS = 3
MOE_ROWS = 256
COMBINE_BUFFERS = 3
COMBINE_ROWS = 128
HI = lax.Precision.HIGHEST


def _cparams(sem, **kw):
    return pltpu.CompilerParams(dimension_semantics=sem, vmem_limit_bytes=VMEM_LIMIT, **kw)


def _dot(a, b):
    return jnp.dot(a, b, preferred_element_type=F32)


def _dot_nt(a, b, precision=None):
    return lax.dot_general(a, b, (((1,), (1,)), ((), ())), preferred_element_type=F32,
                           precision=precision)


def _sigmoid(x):
    return 1.0 / (1.0 + jnp.exp(-x))


def _norm_matmul_body(x_ref, nw_ref, w_ref, *rest):
    x = x_ref[...]
    ms = jnp.mean(x * x, axis=-1, keepdims=True)
    h = (x * lax.rsqrt(ms + RMS_EPS) * nw_ref[...]).astype(BF16)
    if len(rest) == 1:
        rest[0][...] = _dot_nt(h, w_ref[...]).astype(rest[0].dtype)
    else:
        ws_ref, o_ref, os_ref = rest
        o_ref[...] = _dot_nt(h, w_ref[...]).astype(o_ref.dtype)
        os_ref[0] = _dot_nt(h, ws_ref[...])


def norm_matmul(x, nw, w, out_dtype, tm, tn, w_small=None):
    n, d = x.shape
    width = w.shape[0]
    in_specs = [pl.BlockSpec((tm, d), lambda j, i: (i, 0)),
                pl.BlockSpec((1, d), lambda j, i: (0, 0)),
                pl.BlockSpec((tn, d), lambda j, i: (j, 0))]
    out_specs = pl.BlockSpec((tm, tn), lambda j, i: (i, j))
    out_shape = jax.ShapeDtypeStruct((n, width), out_dtype)
    operands = (x, nw.reshape(1, d), w)
    if w_small is not None:
        ws = w_small.shape[0]
        in_specs.append(pl.BlockSpec((ws, d), lambda j, i: (0, 0)))
        out_specs = [out_specs, pl.BlockSpec((1, tm, ws), lambda j, i: (j, i, 0))]
        out_shape = [out_shape, jax.ShapeDtypeStruct((width // tn, n, ws), F32)]
        operands = operands + (w_small,)
    outs = pl.pallas_call(
        _norm_matmul_body,
        grid=(width // tn, n // tm),
        in_specs=in_specs,
        out_specs=out_specs,
        out_shape=out_shape,
        compiler_params=_cparams(("arbitrary", "arbitrary")),
        name="norm_in_proj",
    )(*operands)
    if w_small is None:
        return outs
    return outs[0], outs[1][0]


def _gdn_body(alog_ref, dtb_ref, q_ref, k_ref, v_ref, z_ref, ba_ref, cwq_ref, cwk_ref, cwv_ref,
              onw_ref, o_ref, *scratch):
    tt = GDN_TILE
    c = GDN_CHUNK
    t = pl.program_id(2)

    @pl.when(t == 0)
    def _():
        for hh in range(GDN_HEADS_PER_STEP):
            s_ref, prev_ref, vn_ref, _ = scratch[4 * hh:4 * hh + 4]
            s_ref[...] = jnp.zeros_like(s_ref)
            prev_ref[...] = jnp.zeros_like(prev_ref)
            vn_ref[...] = jnp.zeros_like(vn_ref)

    row_8 = lax.broadcasted_iota(jnp.int32, (8, LANES), 0)
    lane_t = lax.broadcasted_iota(jnp.int32, (tt, LANES), 1)
    row = lax.broadcasted_iota(jnp.int32, (tt, tt), 0)
    col = lax.broadcasted_iota(jnp.int32, (tt, tt), 1)
    same = (row // c) == (col // c)
    causal = jnp.logical_and(same, col <= row)
    strict = jnp.logical_and(same, col < row)
    causal16 = jnp.where(causal, 1.0, 0.0).astype(BF16)
    same16 = jnp.where(same, 1.0, 0.0).astype(BF16)
    eye = jnp.where(row == col, 1.0, 0.0)
    colk = lax.broadcasted_iota(jnp.int32, (HEAD_DIM, tt), 1)
    ba = ba_ref[...]

    heads = [_gdn_one_head(hh, pl.program_id(1) * GDN_HEADS_PER_STEP + hh, alog_ref, dtb_ref, q_ref, k_ref,
                           v_ref, z_ref, ba, cwq_ref, cwk_ref, cwv_ref, onw_ref, o_ref,
                           *scratch[4 * hh:4 * hh + 4],
                           row_8, lane_t, causal, strict, causal16, same16, eye, colk)
             for hh in range(GDN_HEADS_PER_STEP)]
    for _ in itertools.zip_longest(*heads):
        pass


def _gdn_one_head(hh, h, alog_ref, dtb_ref, q_ref, k_ref, v_ref, z_ref, ba, cwq_ref, cwk_ref, cwv_ref,
                  onw_ref, o_ref, s_ref, prev_ref, vn_ref, oacc_ref,
                  row_8, lane_t, causal, strict, causal16, same16, eye, colk):
    tt = GDN_TILE
    c = GDN_CHUNK
    lanes = slice(hh * HEAD_DIM, (hh + 1) * HEAD_DIM)

    def conv_silu(x_ref, cw_ref, idx):
        x = x_ref[:, lanes].astype(F32)
        p = prev_ref[idx]
        w = cw_ref[:, lanes]
        acc = x * w[CONV_WIDTH - 1:CONV_WIDTH, :]
        for s in range(1, CONV_WIDTH):
            xs = pltpu.roll(x, s, axis=0)
            head = jnp.where(row_8 < s, pltpu.roll(p, s, axis=0), xs[:8])
            xs = jnp.concatenate([head, xs[8:]], axis=0)
            acc = acc + xs * w[CONV_WIDTH - 1 - s:CONV_WIDTH - s, :]
        prev_ref[idx] = x[tt - 8:tt, :]
        return acc * _sigmoid(acc)

    q = conv_silu(q_ref, cwq_ref, 0)
    k = conv_silu(k_ref, cwk_ref, 1)
    v = conv_silu(v_ref, cwv_ref, 2)
    q = q * lax.rsqrt(jnp.sum(q * q, axis=-1, keepdims=True) + 1e-6) * (HEAD_DIM ** -0.5)
    k = k * lax.rsqrt(jnp.sum(k * k, axis=-1, keepdims=True) + 1e-6)
    yield

    b_col = jnp.sum(jnp.where(lane_t == h, ba, 0.0), axis=-1, keepdims=True)
    a_col = jnp.sum(jnp.where(lane_t == h + GDN_HEADS, ba, 0.0), axis=-1, keepdims=True)
    beta = _sigmoid(b_col)
    xa = a_col + dtb_ref[h]
    softplus = jnp.maximum(xa, 0.0) + jnp.log(1.0 + jnp.exp(-jnp.abs(xa)))
    g = -jnp.exp(jnp.full((1, 1), alog_ref[h], F32)) * softplus

    g_hi = g.astype(BF16).astype(F32)
    g_mid = (g - g_hi).astype(BF16).astype(F32)
    g_lo = g - g_hi - g_mid
    g3 = jnp.where(lane_t == 0, g_hi, jnp.where(lane_t == 1, g_mid, jnp.where(lane_t == 2, g_lo, 0.0)))
    g3 = g3.astype(BF16)

    def three(r):
        return r[:, 0:1] + r[:, 1:2] + r[:, 2:3]

    gc_col = three(_dot(causal16, g3))
    gc_b = jnp.broadcast_to(gc_col, (tt, LANES))
    glast_b = jnp.broadcast_to(three(_dot(same16, g3)), (tt, LANES))
    gc_row = jnp.transpose(gc_b)[0:1, :]
    decay = jnp.exp(jnp.where(causal, gc_col - gc_row, NEG_INF))
    yield

    kb = k * beta
    k16 = k.astype(BF16)
    kk = _dot_nt(kb.astype(BF16), k16)
    lmat = jnp.where(strict, kk * decay, 0.0)
    yield
    attn = _dot_nt(q.astype(BF16), k16) * decay

    tinv = eye - lmat
    m16 = lmat.astype(BF16)
    for _ in range(5):
        m16 = _dot(m16, m16).astype(BF16)
        yield
        tinv = tinv + _dot(tinv.astype(BF16), m16)
        yield

    egc = jnp.exp(gc_b)
    rhs = jnp.concatenate([v * beta, kb * egc], axis=-1)
    sol = _dot(tinv.astype(BF16), rhs.astype(BF16))
    yield
    u = sol[:, :HEAD_DIM]
    w16 = sol[:, HEAD_DIM:].astype(BF16)
    qd16 = (q * egc).astype(BF16)
    kd = k * jnp.exp(glast_b - gc_b)
    kdt = jnp.transpose(kd)
    gtot_b = jnp.exp(glast_b)
    attn16 = attn.astype(BF16)

    s = s_ref[...]
    for n in range(tt // c):
        sl = slice(n * c, (n + 1) * c)
        s16 = s.astype(BF16)
        v_new = u[sl] - _dot(w16[sl], s16)
        vn_ref[sl, :] = v_new
        yield
        vn16 = vn_ref[...].astype(BF16)
        oacc_ref[sl, :] = _dot(qd16[sl], s16) + _dot(attn16[sl], vn16)
        kdt_n = jnp.where((colk // c) == n, kdt, 0.0).astype(BF16)
        s = s * gtot_b[n * c:n * c + 1, :] + _dot(kdt_n, vn16)
        yield
    s_ref[...] = s

    o = oacc_ref[...]
    o = o * lax.rsqrt(jnp.mean(o * o, axis=-1, keepdims=True) + RMS_EPS) * onw_ref[...]
    z = z_ref[:, lanes].astype(F32)
    o_ref[:, lanes] = (o * (z * _sigmoid(z))).astype(o_ref.dtype)


def gdn_heads(proj, ba, conv_w, a_log, dt_bias, out_norm_w, batch, seq):
    n = batch * seq
    tt = GDN_TILE
    nt = seq // tt
    hps = GDN_HEADS_PER_STEP
    ng = GDN_HEADS // hps
    wide = hps * HEAD_DIM

    def col_spec(section):
        return pl.BlockSpec((tt, wide), lambda b, h, t, *_: (b * nt + t, section * ng + h))

    def cw_spec(section):
        return pl.BlockSpec((CONV_WIDTH, wide), lambda b, h, t, *_: (0, section * ng + h))

    grid_spec = pltpu.PrefetchScalarGridSpec(
        num_scalar_prefetch=2,
        grid=(batch, ng, nt),
        in_specs=[col_spec(0), col_spec(1), col_spec(2), col_spec(3),
                  pl.BlockSpec((tt, LANES), lambda b, h, t, *_: (b * nt + t, 0)),
                  cw_spec(0), cw_spec(1), cw_spec(2),
                  pl.BlockSpec((1, HEAD_DIM), lambda b, h, t, *_: (0, 0))],
        out_specs=pl.BlockSpec((tt, wide), lambda b, h, t, *_: (b * nt + t, h)),
        scratch_shapes=[pltpu.VMEM((HEAD_DIM, HEAD_DIM), F32),
                        pltpu.VMEM((3, 8, HEAD_DIM), F32),
                        pltpu.VMEM((tt, HEAD_DIM), F32),
                        pltpu.VMEM((tt, HEAD_DIM), F32)] * hps)
    return pl.pallas_call(
        _gdn_body,
        grid_spec=grid_spec,
        out_shape=jax.ShapeDtypeStruct((n, GDN_WIDTH), BF16),
        compiler_params=_cparams(("arbitrary", "arbitrary", "arbitrary")),
        name="gdn_heads",
    )(a_log.astype(F32), dt_bias.astype(F32), proj, proj, proj, proj, ba,
      conv_w, conv_w, conv_w, out_norm_w.reshape(1, HEAD_DIM))


def _moba_body(slope_ref, q_ref, k_ref, v_ref, onw_ref, o_ref, *scratch, n_blocks):
    blk = MOBA_BLOCK
    tq = 2 * blk
    nb = n_blocks
    nbp = -(-nb // 8) * 8
    hps = MOBA_HEADS_PER_STEP
    cq = pl.program_id(2)
    c0 = 2 * cq
    lane = lax.broadcasted_iota(I32, (blk, LANES), 1)
    row = lax.broadcasted_iota(I32, (blk, blk), 0)
    col = lax.broadcasted_iota(I32, (blk, blk), 1)
    rid = lax.broadcasted_iota(I32, (LANES, tq), 0)
    bid = lax.broadcasted_iota(I32, (nbp, tq), 0)
    c_row = c0 + (lax.broadcasted_iota(I32, (nbp, tq), 1) >= blk).astype(I32)
    ones_ext_t = jnp.where(jnp.logical_and(rid >= nb, rid < nb + 3), 1.0, 0.0).astype(BF16)

    def head_lanes(hh):
        return slice(hh * HEAD_DIM, (hh + 1) * HEAD_DIM)

    def keys(hh, j, nblk=1):
        return scratch[3 * hh + 1][pl.ds(pl.multiple_of(j * blk, blk), nblk * blk), :]

    def values_t(hh, j, nblk=1):
        return scratch[3 * hh + 2][:, pl.ds(pl.multiple_of(j * blk, blk), nblk * blk)]

    @pl.when(cq == 0)
    def _():
        t_in = lax.broadcasted_iota(I32, (blk, LANES), 0)
        for hh in range(hps):
            kmean_ref, kx_ref, vt_ref = scratch[3 * hh:3 * hh + 3]
            slope2 = slope_ref[pl.program_id(1) * hps + hh] * LOG2E
            kmean_ref[...] = jnp.zeros_like(kmean_ref)
            for j in range(nb):
                kj = k_ref[j * blk:(j + 1) * blk, head_lanes(hh)]
                kmean_ref[j:j + 1, :] = jnp.mean(kj.astype(F32), axis=0, keepdims=True)
                bias = slope2 * (t_in + j * blk).astype(F32)
                b_hi = bias.astype(BF16)
                r1 = bias - b_hi.astype(F32)
                b_mid = r1.astype(BF16)
                b_lo = (r1 - b_mid.astype(F32)).astype(BF16)
                ext = jnp.where(lane == j, 1.0, 0.0).astype(BF16)
                ext = jnp.where(lane == nb, b_hi, ext)
                ext = jnp.where(lane == nb + 1, b_mid, ext)
                ext = jnp.where(lane == nb + 2, b_lo, ext)
                kx_ref[j * blk:(j + 1) * blk, :] = jnp.concatenate([kj, ext], axis=1)
                vj = v_ref[j * blk:(j + 1) * blk, head_lanes(hh)].astype(F32)
                vt_ref[:, j * blk:(j + 1) * blk] = jnp.transpose(vj).astype(BF16)

    def start(s, vals_t):
        m = jnp.max(s, axis=0, keepdims=True)
        p = jnp.exp2(s - m)
        return m, jnp.sum(p, axis=0, keepdims=True), _dot(vals_t, p.astype(BF16))

    def update(carry, s, vals_t):
        m_i, l_i, acc = carry
        m_new = jnp.maximum(m_i, jnp.max(s, axis=0, keepdims=True))
        alpha = jnp.exp2(m_i - m_new)
        p = jnp.exp2(s - m_new)
        return (m_new, l_i * alpha + jnp.sum(p, axis=0, keepdims=True),
                acc * alpha + _dot(vals_t, p.astype(BF16)))

    qx = [None] * hps
    carry0 = [None] * hps

    def prologue(hh):
        q_t = jnp.transpose(q_ref[:, head_lanes(hh)].astype(F32))
        q_t16 = q_t.astype(BF16)
        yield
        qx0_t = jnp.concatenate([q_t16, ones_ext_t], axis=0)
        state = []
        for half in range(2):
            s_own = _dot(keys(hh, c0 + half), qx0_t[:, half * blk:(half + 1) * blk])
            s_own = jnp.where(row <= col, s_own, MOBA_MASK)
            yield
            state.append(start(s_own, values_t(hh, c0 + half)))
            yield
        gate_t = jnp.dot(scratch[3 * hh][...], q_t, preferred_element_type=F32, precision=HI)[:nbp]
        gm = jnp.where(bid < c_row, gate_t, NEG_INF)
        sel = jnp.zeros((nbp, tq), F32)
        for s in range(MOBA_TOPK):
            mx = jnp.max(gm, axis=0, keepdims=True)
            idx = jnp.min(jnp.where(gm == mx, bid, LANES), axis=0, keepdims=True)
            pick = bid == idx
            sel = jnp.where(jnp.logical_and(pick, s < c_row), 1.0, sel)
            gm = jnp.where(pick, -3e38, gm)
        mask_t = jnp.where(sel > 0.5, 0.0, MOBA_MASK)
        ext_t = jnp.concatenate([mask_t, jnp.zeros((LANES - nbp, tq), F32)], axis=0)
        ext_t = jnp.where(rid < nb, ext_t, jnp.where(rid < nb + 3, 1.0, 0.0))
        qx[hh] = jnp.concatenate([q_t16, ext_t.astype(BF16)], axis=0)
        yield
        s_c0 = _dot(keys(hh, c0), qx[hh][:, blk:])
        yield
        second = update(state[1], s_c0, values_t(hh, c0))
        carry0[hh] = tuple(jnp.concatenate([a, b], axis=1) for a, b in zip(state[0], second))

    for _ in itertools.zip_longest(*[prologue(hh) for hh in range(hps)]):
        pass

    def body(p, carries):
        scores = [_dot(keys(hh, 2 * p, 2), qx[hh]) for hh in range(hps)]
        return tuple(update(carries[hh], scores[hh], values_t(hh, 2 * p, 2)) for hh in range(hps))

    final = lax.fori_loop(0, cq, body, tuple(carry0))
    for hh in range(hps):
        _, l_f, acc_f = final[hh]
        o_t = acc_f / l_f
        o_t = o_t * lax.rsqrt(jnp.mean(o_t * o_t, axis=0, keepdims=True) + RMS_EPS)
        o_ref[:, head_lanes(hh)] = (jnp.transpose(o_t) * onw_ref[...]).astype(o_ref.dtype)


def moba_heads(proj, out_norm_w, batch, seq, col_off):
    n = batch * seq
    blk = MOBA_BLOCK
    nb = seq // blk
    nq = nb // 2
    hps = MOBA_HEADS_PER_STEP
    ng = MOBA_HEADS // hps
    wide = hps * HEAD_DIM
    sec0 = col_off // hps
    slopes = jnp.exp2(-8.0 * jnp.arange(1, MOBA_HEADS + 1, dtype=F32) / MOBA_HEADS)
    grid_spec = pltpu.PrefetchScalarGridSpec(
        num_scalar_prefetch=1,
        grid=(batch, ng, nq),
        in_specs=[pl.BlockSpec((2 * blk, wide), lambda b, h, c, *_: (b * nq + c, sec0 + h)),
                  pl.BlockSpec((seq, wide), lambda b, h, c, *_: (b, sec0 + ng + h)),
                  pl.BlockSpec((seq, wide), lambda b, h, c, *_: (b, sec0 + 2 * ng + h)),
                  pl.BlockSpec((1, HEAD_DIM), lambda b, h, c, *_: (0, 0))],
        out_specs=pl.BlockSpec((2 * blk, wide), lambda b, h, c, *_: (b * nq + c, h)),
        scratch_shapes=[pltpu.VMEM((LANES, HEAD_DIM), F32),
                        pltpu.VMEM((nb * blk, HEAD_DIM + LANES), BF16),
                        pltpu.VMEM((HEAD_DIM, nb * blk), BF16)] * hps)
    assert nb % 2 == 0 and nb + 3 <= LANES and col_off % hps == 0
    return pl.pallas_call(
        functools.partial(_moba_body, n_blocks=nb),
        grid_spec=grid_spec,
        out_shape=jax.ShapeDtypeStruct((n, MOBA_WIDTH), BF16),
        compiler_params=_cparams(("arbitrary", "arbitrary", "arbitrary")),
        name="moba_heads",
    )(slopes, proj, proj, proj, out_norm_w.reshape(1, HEAD_DIM))


def _out_proj_body(x_ref, og_ref, om_ref, wg_ref, wm_ref, nw_ref, wrh_ref, wrl_ref, rb_ref,
                   x2_ref, hp_ref, gt_ref, id_ref):
    tm = x_ref.shape[0]
    x2 = x_ref[...] + _dot(og_ref[...], wg_ref[...]) + _dot(om_ref[...], wm_ref[...])
    x2_ref[...] = x2
    ms = jnp.mean(x2 * x2, axis=-1, keepdims=True)
    h2 = x2 * lax.rsqrt(ms + RMS_EPS) * nw_ref[...]
    h_hi = h2.astype(BF16)
    hp_ref[...] = h2
    h_lo = (h2 - h_hi.astype(F32)).astype(BF16)
    lg = _dot(h_hi, wrh_ref[...]) + _dot(h_hi, wrl_ref[...]) + _dot(h_lo, wrh_ref[...]) + rb_ref[...]

    lane = lax.broadcasted_iota(I32, (tm, LANES), 1)
    is_g = lane < N_GROUPS
    mg = jnp.max(jnp.where(is_g, lg, NEG_INF), axis=-1, keepdims=True)
    g_idx = jnp.min(jnp.where(jnp.logical_and(is_g, lg == mg), lane, LANES), axis=-1, keepdims=True)
    sum_g = jnp.sum(jnp.where(is_g, jnp.exp(lg - mg), 0.0), axis=-1, keepdims=True)
    p_top_g = 1.0 / sum_g
    lo = N_GROUPS + g_idx * EXPERTS_PER_GROUP
    in_grp = jnp.logical_and(lane >= lo, lane < lo + EXPERTS_PER_GROUP)
    m1 = jnp.max(jnp.where(in_grp, lg, NEG_INF), axis=-1, keepdims=True)
    i1 = jnp.min(jnp.where(jnp.logical_and(in_grp, lg == m1), lane, LANES), axis=-1, keepdims=True)
    rest = jnp.logical_and(in_grp, lane != i1)
    m2 = jnp.max(jnp.where(rest, lg, NEG_INF), axis=-1, keepdims=True)
    i2 = jnp.min(jnp.where(jnp.logical_and(rest, lg == m2), lane, LANES), axis=-1, keepdims=True)
    e2 = jnp.exp(m2 - m1)
    gate1 = p_top_g / (1.0 + e2)
    gate2 = p_top_g * e2 / (1.0 + e2)
    gt_ref[...] = jnp.where(lane == 0, gate1, jnp.where(lane == 1, gate2, 0.0))
    id_ref[...] = jnp.where(lane == 0, i1 - N_GROUPS, jnp.where(lane == 1, i2 - N_GROUPS, 0))


def out_proj(x, og, om, w_g, w_m, nw, wr_hi, wr_lo, r_bias, tm):
    n, d = x.shape
    const = lambda i: (0, 0)
    rows = lambda i: (i, 0)
    return pl.pallas_call(
        _out_proj_body,
        grid=(n // tm,),
        in_specs=[pl.BlockSpec((tm, d), rows),
                  pl.BlockSpec((tm, GDN_WIDTH), rows),
                  pl.BlockSpec((tm, MOBA_WIDTH), rows),
                  pl.BlockSpec((GDN_WIDTH, d), const),
                  pl.BlockSpec((MOBA_WIDTH, d), const),
                  pl.BlockSpec((1, d), const),
                  pl.BlockSpec((d, LANES), const),
                  pl.BlockSpec((d, LANES), const),
                  pl.BlockSpec((1, LANES), const)],
        out_specs=[pl.BlockSpec((tm, d), rows),
                   pl.BlockSpec((tm, d), rows),
                   pl.BlockSpec((tm, LANES), rows),
                   pl.BlockSpec((tm, LANES), rows)],
        out_shape=[jax.ShapeDtypeStruct((n, d), F32),
                   jax.ShapeDtypeStruct((n, d), F32),
                   jax.ShapeDtypeStruct((n, LANES), F32),
                   jax.ShapeDtypeStruct((n, LANES), I32)],
        compiler_params=_cparams(("arbitrary",)),
        name="out_proj_router",
    )(x, og, om, w_g, w_m, nw.reshape(1, d), wr_hi, wr_lo, r_bias)


def _invert_body(pos0_ref, pos1_ref, src_ref, zeros_vmem, sem):
    n_tok = pos0_ref.shape[0]
    zeros_vmem[...] = jnp.zeros_like(zeros_vmem)
    clear = pltpu.make_async_copy(zeros_vmem, src_ref, sem)
    clear.start()
    clear.wait()

    def fill(t, carry):
        src_ref[pos0_ref[t]] = t
        src_ref[pos1_ref[t]] = t
        return carry

    lax.fori_loop(0, n_tok, fill, 0, unroll=8)


def moe_source_rows(pos0, pos1, n_rows):
    grid_spec = pltpu.PrefetchScalarGridSpec(
        num_scalar_prefetch=2,
        grid=(1,),
        in_specs=[],
        out_specs=pl.BlockSpec(memory_space=pltpu.SMEM),
        scratch_shapes=[pltpu.VMEM((n_rows,), I32), pltpu.SemaphoreType.DMA(())])
    return pl.pallas_call(
        _invert_body,
        grid_spec=grid_spec,
        out_shape=jax.ShapeDtypeStruct((n_rows,), I32),
        compiler_params=_cparams(("arbitrary",)),
        name="moe_source_rows",
    )(pos0, pos1)


def _moe_body(blk_e_ref, nblk_ref, first_ref, slot_ref, next_e_ref, src_ref, h_hbm, wg_hbm, wu_hbm, wd_hbm,
              y_ref, xbuf, wg_buf, wu_buf, wd_buf, wg16, wu16, wd16, sem, gsem):
    rb = MOE_ROWS
    i = pl.program_id(0)
    e = blk_e_ref[i]
    slot = slot_ref[i]
    n_used = nblk_ref[0]
    nbuf = MOE_ROW_BUFFERS
    ahead = nbuf - 1
    xslot = lax.rem(i, nbuf)

    def row_copy(block, r):
        s = lax.rem(block, nbuf)
        tok = src_ref[block * rb + r]
        return pltpu.make_async_copy(h_hbm.at[pl.ds(tok, 1)], xbuf.at[s, pl.ds(r, 1)], gsem.at[s])

    def wait_rows(s):
        pltpu.make_async_copy(h_hbm.at[pl.ds(0, rb)], xbuf.at[s], gsem.at[s]).wait()

    @pl.when(i == 0)
    def _():
        for b in range(ahead):
            @pl.when(b < n_used)
            def _(b=b):
                def issue(r, carry):
                    row_copy(b, r).start()
                    return carry
                lax.fori_loop(0, rb, issue, 0, unroll=8)

    def weight_copies(expert, s):
        return (pltpu.make_async_copy(wg_hbm.at[expert], wg_buf.at[s], sem.at[s]),
                pltpu.make_async_copy(wu_hbm.at[expert], wu_buf.at[s], sem.at[s]),
                pltpu.make_async_copy(wd_hbm.at[expert], wd_buf.at[s], sem.at[s]))

    @pl.when(i == 0)
    def _():
        for cp in weight_copies(e, slot):
            cp.start(priority=1)

    @pl.when(first_ref[i] == 1)
    def _():
        for cp in weight_copies(e, slot):
            cp.wait()

        @pl.when(next_e_ref[i] >= 0)
        def _():
            for cp in weight_copies(next_e_ref[i], 1 - slot):
                cp.start(priority=1)

        wg16[...] = wg_buf[slot].astype(BF16)
        wu16[...] = wu_buf[slot].astype(BF16)
        wd16[...] = wd_buf[slot].astype(BF16)

    def compute(xb):
        g = _dot(xb, wg16[...])
        u = _dot(xb, wu16[...])
        hm = (g * _sigmoid(g) * u).astype(BF16)
        y_ref[...] = _dot(hm, wd16[...])

    @pl.when(i < n_used - ahead)
    def _():
        wait_rows(xslot)
        xb = xbuf[xslot].astype(BF16)
        for r in range(rb):
            row_copy(i + ahead, r).start(priority=r % 2)
        compute(xb)

    @pl.when(jnp.logical_and(i >= n_used - ahead, i < n_used))
    def _():
        wait_rows(xslot)
        compute(xbuf[xslot].astype(BF16))

    @pl.when(i >= n_used)
    def _():
        y_ref[...] = jnp.zeros_like(y_ref)


def moe_experts(h2, src_rows, blk_expert, n_used, first, slot, next_e, w_gate, w_up, w_down):
    rb = MOE_ROWS
    n_rows = src_rows.shape[0]
    d = h2.shape[1]
    de = w_gate.shape[-1]
    grid_spec = pltpu.PrefetchScalarGridSpec(
        num_scalar_prefetch=6,
        grid=(n_rows // rb,),
        in_specs=[pl.BlockSpec(memory_space=pl.ANY),
                  pl.BlockSpec(memory_space=pl.ANY),
                  pl.BlockSpec(memory_space=pl.ANY),
                  pl.BlockSpec(memory_space=pl.ANY)],
        out_specs=pl.BlockSpec((rb, d), lambda i, *_: (i, 0)),
        scratch_shapes=[pltpu.VMEM((MOE_ROW_BUFFERS, rb, d), F32),
                        pltpu.VMEM((2, d, de), F32), pltpu.VMEM((2, d, de), F32), pltpu.VMEM((2, de, d), F32),
                        pltpu.VMEM((d, de), BF16), pltpu.VMEM((d, de), BF16), pltpu.VMEM((de, d), BF16),
                        pltpu.SemaphoreType.DMA((2,)), pltpu.SemaphoreType.DMA((MOE_ROW_BUFFERS,))])
    return pl.pallas_call(
        _moe_body,
        grid_spec=grid_spec,
        out_shape=jax.ShapeDtypeStruct((n_rows, d), F32),
        compiler_params=_cparams(("arbitrary",)),
        name="moe_experts",
    )(blk_expert, n_used, first, slot, next_e, src_rows, h2, w_gate, w_up, w_down)


def _combine_body(pos0_ref, pos1_ref, y_hbm, x2_ref, gt_ref, nw_ref, o_ref, buf, sem):
    tf = COMBINE_ROWS
    nbuf = COMBINE_BUFFERS
    ahead = nbuf - 1
    i = pl.program_id(0)
    n_steps = pl.num_programs(0)
    slot = lax.rem(i, nbuf)

    def start_row(step, r):
        sl = lax.rem(step, nbuf)
        t = step * tf + r
        pltpu.make_async_copy(y_hbm.at[pl.ds(pos0_ref[t], 1)], buf.at[sl, 0, pl.ds(r, 1)],
                              sem.at[sl]).start()
        pltpu.make_async_copy(y_hbm.at[pl.ds(pos1_ref[t], 1)], buf.at[sl, 1, pl.ds(r, 1)],
                              sem.at[sl]).start(priority=1)

    @pl.when(i == 0)
    def _():
        for b in range(ahead):
            def body(r, carry, b=b):
                start_row(b, r)
                return carry
            lax.fori_loop(0, tf, body, 0, unroll=8)

    def wait_rows():
        for kk in range(TOPK_IN_GROUP):
            pltpu.make_async_copy(y_hbm.at[pl.ds(0, tf)], buf.at[slot, kk], sem.at[slot]).wait()

    def finish(y0, y1):
        gt = gt_ref[...]
        xo = x2_ref[...] + gt[:, 0:1] * y0 + gt[:, 1:2] * y1
        ms = jnp.mean(xo * xo, axis=-1, keepdims=True)
        o_ref[...] = xo * lax.rsqrt(ms + RMS_EPS) * nw_ref[...]

    @pl.when(i + ahead < n_steps)
    def _():
        wait_rows()
        y0 = buf[slot, 0]
        y1 = buf[slot, 1]
        for r in range(tf):
            start_row(i + ahead, r)
        finish(y0, y1)

    @pl.when(i + ahead >= n_steps)
    def _():
        wait_rows()
        finish(buf[slot, 0], buf[slot, 1])


def moe_combine(yb, pos0, pos1, x2, gates, nw):
    n, d = x2.shape
    tf = COMBINE_ROWS
    grid_spec = pltpu.PrefetchScalarGridSpec(
        num_scalar_prefetch=2,
        grid=(n // tf,),
        in_specs=[pl.BlockSpec(memory_space=pl.ANY),
                  pl.BlockSpec((tf, d), lambda i, *_: (i, 0)),
                  pl.BlockSpec((tf, LANES), lambda i, *_: (i, 0)),
                  pl.BlockSpec((1, d), lambda i, *_: (0, 0))],
        out_specs=pl.BlockSpec((tf, d), lambda i, *_: (i, 0)),
        scratch_shapes=[pltpu.VMEM((COMBINE_BUFFERS, 2, tf, d), F32), pltpu.SemaphoreType.DMA((COMBINE_BUFFERS,))])
    return pl.pallas_call(
        _combine_body,
        grid_spec=grid_spec,
        out_shape=jax.ShapeDtypeStruct((n, d), F32),
        compiler_params=_cparams(("arbitrary",)),
        name="moe_combine",
    )(pos0, pos1, yb, x2, gates, nw.reshape(1, d))


def _dispatch_plan(expert_id):
    n_tok, k = expert_id.shape
    rb = MOE_ROWS
    n_assign = n_tok * k
    e_flat = expert_id.reshape(-1)
    onehot = (e_flat[:, None] == jnp.arange(N_EXPERTS, dtype=I32)[None, :]).astype(I32)
    csum = jnp.cumsum(onehot, axis=0)
    counts = csum[-1]
    padded = (counts + rb - 1) // rb * rb
    pad_end = jnp.cumsum(padded)
    pad_start = pad_end - padded
    pos = jnp.sum(onehot * (pad_start[None, :] + csum - 1), axis=1).astype(I32).reshape(n_tok, k)
    n_rb = -(-n_assign // rb) + N_EXPERTS
    blk_start = jnp.arange(n_rb, dtype=I32) * rb
    blk_expert = jnp.minimum(jnp.sum((pad_end[None, :] <= blk_start[:, None]).astype(I32), axis=1),
                             N_EXPERTS - 1).astype(I32)
    n_used = (pad_end[-1] // rb).astype(I32).reshape(1)
    blk = jnp.arange(n_rb, dtype=I32)
    prev_expert = jnp.concatenate([jnp.full((1,), -1, I32), blk_expert[:-1]])
    first = jnp.logical_and(blk < n_used[0], blk_expert != prev_expert).astype(I32)
    slot = jnp.bitwise_and(jnp.cumsum(first) - 1, 1).astype(I32)
    ids = jnp.arange(N_EXPERTS, dtype=I32)
    later = jnp.logical_and(ids[None, :] > ids[:, None], counts[None, :] > 0)
    next_nonempty = jnp.min(jnp.where(later, ids[None, :], N_EXPERTS), axis=1)
    next_nonempty = jnp.where(next_nonempty < N_EXPERTS, next_nonempty, -1).astype(I32)
    next_e = next_nonempty[blk_expert]
    return pos, blk_expert, n_used, first, slot, next_e, n_rb * rb


def kernel(x, norm_mix_w, w_in, gdn_conv_w, gdn_A_log, gdn_dt_bias, gdn_out_norm_w, moba_out_norm_w, w_out, norm_ffn_w, w_router_group, b_router_group, w_router_expert, b_router_expert, w_expert_gate, w_expert_up, w_expert_down, norm_final_w):
    batch, seq, d = x.shape
    n = batch * seq
    assert w_in.shape[0] == 1, "the final norm is fused into the last layer's combine; one layer supported"
    l = 0
    xf = x.reshape(n, d).astype(F32)
    gw = 4 * GDN_WIDTH
    w_l = w_in[l]
    mq0 = gw + 2 * GDN_HEADS
    col_scale = jnp.where(jnp.arange(3 * MOBA_WIDTH) < MOBA_WIDTH, HEAD_DIM ** -0.5 * LOG2E, 1.0).astype(F32)
    w_t = jnp.swapaxes(w_l, 0, 1)
    w_gdn = w_t[:gw].astype(BF16)
    w_moba = (w_t[mq0:] * col_scale[:, None]).astype(BF16)
    w_ba = jnp.pad(w_t[gw:gw + 2 * GDN_HEADS], ((0, LANES - 2 * GDN_HEADS), (0, 0))).astype(BF16)
    proj_g, ba = norm_matmul(xf, norm_mix_w[l], w_gdn, BF16, IN_PROJ_ROWS, gw // 2, w_small=w_ba)
    proj_m = norm_matmul(xf, norm_mix_w[l], w_moba, BF16, IN_PROJ_ROWS, 3 * MOBA_WIDTH // 2)
    og = gdn_heads(proj_g, ba, gdn_conv_w[l].astype(F32), gdn_A_log[l], gdn_dt_bias[l],
                   gdn_out_norm_w[l].astype(F32), batch, seq)
    om = moba_heads(proj_m, moba_out_norm_w[l].astype(F32), batch, seq, 0)

    w_o = w_out[l].astype(BF16)
    n_r = N_GROUPS + N_EXPERTS
    w_router = jnp.pad(jnp.concatenate([w_router_group[l], w_router_expert[l]], axis=1).astype(F32),
                       ((0, 0), (0, LANES - n_r)))
    wr_hi = w_router.astype(BF16)
    wr_lo = (w_router - wr_hi.astype(F32)).astype(BF16)
    r_bias = jnp.pad(jnp.concatenate([b_router_group[l], b_router_expert[l]]).astype(F32),
                     (0, LANES - n_r)).reshape(1, LANES)
    x2, hp, gates, ids = out_proj(xf, og, om, w_o[:GDN_WIDTH], w_o[GDN_WIDTH:], norm_ffn_w[l].astype(F32),
                                  wr_hi, wr_lo, r_bias, 512)

    pos, blk_expert, n_used, first, slot, next_e, n_rows = _dispatch_plan(ids[:, :TOPK_IN_GROUP])
    pos0, pos1 = pos[:, 0], pos[:, 1]
    src_rows = moe_source_rows(pos0, pos1, n_rows)
    yb = moe_experts(hp, src_rows, blk_expert, n_used, first, slot, next_e, w_expert_gate[l].astype(F32),
                     w_expert_up[l].astype(F32), w_expert_down[l].astype(F32))
    out = moe_combine(yb, pos0, pos1, x2, gates, norm_final_w.astype(F32))
    return out.reshape(batch, seq, d).astype(x.dtype)
```

```python
import functools
import itertools

import jax
import jax.numpy as jnp
from jax import lax
from jax.experimental import pallas as pl
from jax.experimental.pallas import tpu as pltpu

F32 = jnp.float32
BF16 = jnp.bfloat16
U32 = jnp.uint32
I32 = jnp.int32

HEAD_DIM = 128
GDN_HEADS = 8
MOBA_HEADS = 8
GDN_WIDTH = GDN_HEADS * HEAD_DIM
MOBA_WIDTH = MOBA_HEADS * HEAD_DIM
CONV_WIDTH = 4
GDN_CHUNK = 64
MOBA_BLOCK = 256
MOBA_TOPK = 3
N_GROUPS = 4
EXPERTS_PER_GROUP = 8
N_EXPERTS = N_GROUPS * EXPERTS_PER_GROUP
TOPK_IN_GROUP = 2
RMS_EPS = 1e-6
NEG_INF = -1e30
MOBA_MASK = -(2.0 ** 99)
LOG2E = 1.4426950408889634
LANES = 128
VMEM_LIMIT = 56 * 1024 * 1024

IN_PROJ_ROWS = 1024
GDN_TILE = 128
GDN_HEADS_PER_STEP = 8
MOBA_HEADS_PER_STEP = 4
MOE_ROW_BUFFERS = 3
MOE_ROWS = 256
COMBINE_BUFFERS = 3
COMBINE_ROWS = 256
HI = lax.Precision.HIGHEST


def _cparams(sem, **kw):
    return pltpu.CompilerParams(dimension_semantics=sem, vmem_limit_bytes=VMEM_LIMIT, **kw)


def _dot(a, b):
    return jnp.dot(a, b, preferred_element_type=F32)


def _dot_nt(a, b, precision=None):
    return lax.dot_general(a, b, (((1,), (1,)), ((), ())), preferred_element_type=F32,
                           precision=precision)


def _sigmoid(x):
    return 1.0 / (1.0 + jnp.exp(-x))


def _norm_matmul_body(x_ref, nw_ref, w_ref, *rest):
    x = x_ref[...]
    ms = jnp.mean(x * x, axis=-1, keepdims=True)
    h = (x * lax.rsqrt(ms + RMS_EPS) * nw_ref[...]).astype(BF16)
    if len(rest) == 1:
        rest[0][...] = _dot_nt(h, w_ref[...]).astype(rest[0].dtype)
    else:
        ws_ref, o_ref, os_ref = rest
        o_ref[...] = _dot_nt(h, w_ref[...]).astype(o_ref.dtype)
        os_ref[0] = _dot_nt(h, ws_ref[...])


def norm_matmul(x, nw, w, out_dtype, tm, tn, w_small=None):
    n, d = x.shape
    width = w.shape[0]
    in_specs = [pl.BlockSpec((tm, d), lambda j, i: (i, 0)),
                pl.BlockSpec((1, d), lambda j, i: (0, 0)),
                pl.BlockSpec((tn, d), lambda j, i: (j, 0))]
    out_specs = pl.BlockSpec((tm, tn), lambda j, i: (i, j))
    out_shape = jax.ShapeDtypeStruct((n, width), out_dtype)
    operands = (x, nw.reshape(1, d), w)
    if w_small is not None:
        ws = w_small.shape[0]
        in_specs.append(pl.BlockSpec((ws, d), lambda j, i: (0, 0)))
        out_specs = [out_specs, pl.BlockSpec((1, tm, ws), lambda j, i: (j, i, 0))]
        out_shape = [out_shape, jax.ShapeDtypeStruct((width // tn, n, ws), F32)]
        operands = operands + (w_small,)
    outs = pl.pallas_call(
        _norm_matmul_body,
        grid=(width // tn, n // tm),
        in_specs=in_specs,
        out_specs=out_specs,
        out_shape=out_shape,
        compiler_params=_cparams(("arbitrary", "arbitrary")),
        name="norm_in_proj",
    )(*operands)
    if w_small is None:
        return outs
    return outs[0], outs[1][0]


def _gdn_body(alog_ref, dtb_ref, q_ref, k_ref, v_ref, z_ref, ba_ref, cwq_ref, cwk_ref, cwv_ref,
              onw_ref, o_ref, *scratch):
    tt = GDN_TILE
    c = GDN_CHUNK
    t = pl.program_id(2)

    @pl.when(t == 0)
    def _():
        for hh in range(GDN_HEADS_PER_STEP):
            s_ref, prev_ref, vn_ref, _ = scratch[4 * hh:4 * hh + 4]
            s_ref[...] = jnp.zeros_like(s_ref)
            prev_ref[...] = jnp.zeros_like(prev_ref)
            vn_ref[...] = jnp.zeros_like(vn_ref)

    row_8 = lax.broadcasted_iota(jnp.int32, (8, LANES), 0)
    lane_t = lax.broadcasted_iota(jnp.int32, (tt, LANES), 1)
    row = lax.broadcasted_iota(jnp.int32, (tt, tt), 0)
    col = lax.broadcasted_iota(jnp.int32, (tt, tt), 1)
    same = (row // c) == (col // c)
    causal = jnp.logical_and(same, col <= row)
    strict = jnp.logical_and(same, col < row)
    causal16 = jnp.where(causal, 1.0, 0.0).astype(BF16)
    same16 = jnp.where(same, 1.0, 0.0).astype(BF16)
    eye = jnp.where(row == col, 1.0, 0.0)
    colk = lax.broadcasted_iota(jnp.int32, (HEAD_DIM, tt), 1)
    ba = ba_ref[...]

    heads = [_gdn_one_head(hh, pl.program_id(1) * GDN_HEADS_PER_STEP + hh, alog_ref, dtb_ref, q_ref, k_ref,
                           v_ref, z_ref, ba, cwq_ref, cwk_ref, cwv_ref, onw_ref, o_ref,
                           *scratch[4 * hh:4 * hh + 4],
                           row_8, lane_t, causal, strict, causal16, same16, eye, colk)
             for hh in range(GDN_HEADS_PER_STEP)]
    for _ in itertools.zip_longest(*heads):
        pass


def _gdn_one_head(hh, h, alog_ref, dtb_ref, q_ref, k_ref, v_ref, z_ref, ba, cwq_ref, cwk_ref, cwv_ref,
                  onw_ref, o_ref, s_ref, prev_ref, vn_ref, oacc_ref,
                  row_8, lane_t, causal, strict, causal16, same16, eye, colk):
    tt = GDN_TILE
    c = GDN_CHUNK
    lanes = slice(hh * HEAD_DIM, (hh + 1) * HEAD_DIM)

    def conv_silu(x_ref, cw_ref, idx):
        x = x_ref[:, lanes].astype(F32)
        p = prev_ref[idx]
        w = cw_ref[:, lanes]
        acc = x * w[CONV_WIDTH - 1:CONV_WIDTH, :]
        for s in range(1, CONV_WIDTH):
            xs = pltpu.roll(x, s, axis=0)
            head = jnp.where(row_8 < s, pltpu.roll(p, s, axis=0), xs[:8])
            xs = jnp.concatenate([head, xs[8:]], axis=0)
            acc = acc + xs * w[CONV_WIDTH - 1 - s:CONV_WIDTH - s, :]
        prev_ref[idx] = x[tt - 8:tt, :]
        return acc * _sigmoid(acc)

    q = conv_silu(q_ref, cwq_ref, 0)
    k = conv_silu(k_ref, cwk_ref, 1)
    v = conv_silu(v_ref, cwv_ref, 2)
    q = q * lax.rsqrt(jnp.sum(q * q, axis=-1, keepdims=True) + 1e-6) * (HEAD_DIM ** -0.5)
    k = k * lax.rsqrt(jnp.sum(k * k, axis=-1, keepdims=True) + 1e-6)
    yield

    b_col = jnp.sum(jnp.where(lane_t == h, ba, 0.0), axis=-1, keepdims=True)
    a_col = jnp.sum(jnp.where(lane_t == h + GDN_HEADS, ba, 0.0), axis=-1, keepdims=True)
    beta = _sigmoid(b_col)
    xa = a_col + dtb_ref[h]
    softplus = jnp.maximum(xa, 0.0) + jnp.log(1.0 + jnp.exp(-jnp.abs(xa)))
    g = -jnp.exp(jnp.full((1, 1), alog_ref[h], F32)) * softplus

    g_hi = g.astype(BF16).astype(F32)
    g_mid = (g - g_hi).astype(BF16).astype(F32)
    g_lo = g - g_hi - g_mid
    g3 = jnp.where(lane_t == 0, g_hi, jnp.where(lane_t == 1, g_mid, jnp.where(lane_t == 2, g_lo, 0.0)))
    g3 = g3.astype(BF16)

    def three(r):
        return r[:, 0:1] + r[:, 1:2] + r[:, 2:3]

    gc_col = three(_dot(causal16, g3))
    gc_b = jnp.broadcast_to(gc_col, (tt, LANES))
    glast_b = jnp.broadcast_to(three(_dot(same16, g3)), (tt, LANES))
    gc_row = jnp.transpose(gc_b)[0:1, :]
    decay = jnp.exp(jnp.where(causal, gc_col - gc_row, NEG_INF))
    yield

    kb = k * beta
    k16 = k.astype(BF16)
    kk = _dot_nt(kb.astype(BF16), k16)
    lmat = jnp.where(strict, kk * decay, 0.0)
    yield
    attn = _dot_nt(q.astype(BF16), k16) * decay

    tinv = eye - lmat
    m16 = lmat.astype(BF16)
    for _ in range(5):
        m16 = _dot(m16, m16).astype(BF16)
        yield
        tinv = tinv + _dot(tinv.astype(BF16), m16)
        yield

    egc = jnp.exp(gc_b)
    rhs = jnp.concatenate([v * beta, kb * egc], axis=-1)
    sol = _dot(tinv.astype(BF16), rhs.astype(BF16))
    yield
    u = sol[:, :HEAD_DIM]
    w16 = sol[:, HEAD_DIM:].astype(BF16)
    qd16 = (q * egc).astype(BF16)
    kd = k * jnp.exp(glast_b - gc_b)
    kdt = jnp.transpose(kd)
    gtot_b = jnp.exp(glast_b)
    attn16 = attn.astype(BF16)

    s = s_ref[...]
    for n in range(tt // c):
        sl = slice(n * c, (n + 1) * c)
        s16 = s.astype(BF16)
        v_new = u[sl] - _dot(w16[sl], s16)
        vn_ref[sl, :] = v_new
        yield
        vn16 = vn_ref[...].astype(BF16)
        oacc_ref[sl, :] = _dot(qd16[sl], s16) + _dot(attn16[sl], vn16)
        kdt_n = jnp.where((colk // c) == n, kdt, 0.0).astype(BF16)
        s = s * gtot_b[n * c:n * c + 1, :] + _dot(kdt_n, vn16)
        yield
    s_ref[...] = s

    o = oacc_ref[...]
    o = o * lax.rsqrt(jnp.mean(o * o, axis=-1, keepdims=True) + RMS_EPS) * onw_ref[...]
    z = z_ref[:, lanes].astype(F32)
    o_ref[:, lanes] = (o * (z * _sigmoid(z))).astype(o_ref.dtype)


def gdn_heads(proj, ba, conv_w, a_log, dt_bias, out_norm_w, batch, seq):
    n = batch * seq
    tt = GDN_TILE
    nt = seq // tt
    hps = GDN_HEADS_PER_STEP
    ng = GDN_HEADS // hps
    wide = hps * HEAD_DIM

    def col_spec(section):
        return pl.BlockSpec((tt, wide), lambda b, h, t, *_: (b * nt + t, section * ng + h))

    def cw_spec(section):
        return pl.BlockSpec((CONV_WIDTH, wide), lambda b, h, t, *_: (0, section * ng + h))

    grid_spec = pltpu.PrefetchScalarGridSpec(
        num_scalar_prefetch=2,
        grid=(batch, ng, nt),
        in_specs=[col_spec(0), col_spec(1), col_spec(2), col_spec(3),
                  pl.BlockSpec((tt, LANES), lambda b, h, t, *_: (b * nt + t, 0)),
                  cw_spec(0), cw_spec(1), cw_spec(2),
                  pl.BlockSpec((1, HEAD_DIM), lambda b, h, t, *_: (0, 0))],
        out_specs=pl.BlockSpec((tt, wide), lambda b, h, t, *_: (b * nt + t, h)),
        scratch_shapes=[pltpu.VMEM((HEAD_DIM, HEAD_DIM), F32),
                        pltpu.VMEM((3, 8, HEAD_DIM), F32),
                        pltpu.VMEM((tt, HEAD_DIM), F32),
                        pltpu.VMEM((tt, HEAD_DIM), F32)] * hps)
    return pl.pallas_call(
        _gdn_body,
        grid_spec=grid_spec,
        out_shape=jax.ShapeDtypeStruct((n, GDN_WIDTH), BF16),
        compiler_params=_cparams(("arbitrary", "arbitrary", "arbitrary")),
        name="gdn_heads",
    )(a_log.astype(F32), dt_bias.astype(F32), proj, proj, proj, proj, ba,
      conv_w, conv_w, conv_w, out_norm_w.reshape(1, HEAD_DIM))


def _moba_body(slope_ref, q_ref, k_ref, v_ref, onw_ref, o_ref, *scratch, n_blocks):
    blk = MOBA_BLOCK
    tq = 2 * blk
    nb = n_blocks
    nbp = -(-nb // 8) * 8
    hps = MOBA_HEADS_PER_STEP
    cq = pl.program_id(2)
    c0 = 2 * cq
    lane = lax.broadcasted_iota(I32, (blk, LANES), 1)
    row = lax.broadcasted_iota(I32, (blk, blk), 0)
    col = lax.broadcasted_iota(I32, (blk, blk), 1)
    rid = lax.broadcasted_iota(I32, (LANES, tq), 0)
    bid = lax.broadcasted_iota(I32, (nbp, tq), 0)
    c_row = c0 + (lax.broadcasted_iota(I32, (nbp, tq), 1) >= blk).astype(I32)
    ones_ext_t = jnp.where(jnp.logical_and(rid >= nb, rid < nb + 3), 1.0, 0.0).astype(BF16)

    def head_lanes(hh):
        return slice(hh * HEAD_DIM, (hh + 1) * HEAD_DIM)

    def keys(hh, j, nblk=1):
        return scratch[3 * hh + 1][pl.ds(pl.multiple_of(j * blk, blk), nblk * blk), :]

    def values_t(hh, j, nblk=1):
        return scratch[3 * hh + 2][:, pl.ds(pl.multiple_of(j * blk, blk), nblk * blk)]

    @pl.when(cq == 0)
    def _():
        t_in = lax.broadcasted_iota(I32, (blk, LANES), 0)
        for hh in range(hps):
            kmean_ref, kx_ref, vt_ref = scratch[3 * hh:3 * hh + 3]
            slope2 = slope_ref[pl.program_id(1) * hps + hh] * LOG2E
            kmean_ref[...] = jnp.zeros_like(kmean_ref)
            for j in range(nb):
                kj = k_ref[j * blk:(j + 1) * blk, head_lanes(hh)]
                kmean_ref[j:j + 1, :] = jnp.mean(kj.astype(F32), axis=0, keepdims=True)
                bias = slope2 * (t_in + j * blk).astype(F32)
                b_hi = bias.astype(BF16)
                r1 = bias - b_hi.astype(F32)
                b_mid = r1.astype(BF16)
                b_lo = (r1 - b_mid.astype(F32)).astype(BF16)
                ext = jnp.where(lane == j, 1.0, 0.0).astype(BF16)
                ext = jnp.where(lane == nb, b_hi, ext)
                ext = jnp.where(lane == nb + 1, b_mid, ext)
                ext = jnp.where(lane == nb + 2, b_lo, ext)
                kx_ref[j * blk:(j + 1) * blk, :] = jnp.concatenate([kj, ext], axis=1)
                vj = v_ref[j * blk:(j + 1) * blk, head_lanes(hh)].astype(F32)
                vt_ref[:, j * blk:(j + 1) * blk] = jnp.transpose(vj).astype(BF16)

    def start(s, vals_t):
        m = jnp.max(s, axis=0, keepdims=True)
        p = jnp.exp2(s - m)
        return m, jnp.sum(p, axis=0, keepdims=True), _dot(vals_t, p.astype(BF16))

    def update(carry, s, vals_t):
        m_i, l_i, acc = carry
        m_new = jnp.maximum(m_i, jnp.max(s, axis=0, keepdims=True))
        alpha = jnp.exp2(m_i - m_new)
        p = jnp.exp2(s - m_new)
        return (m_new, l_i * alpha + jnp.sum(p, axis=0, keepdims=True),
                acc * alpha + _dot(vals_t, p.astype(BF16)))

    qx = [None] * hps
    carry0 = [None] * hps

    def prologue(hh):
        q_t = jnp.transpose(q_ref[:, head_lanes(hh)].astype(F32))
        q_t16 = q_t.astype(BF16)
        yield
        qx0_t = jnp.concatenate([q_t16, ones_ext_t], axis=0)
        state = []
        for half in range(2):
            s_own = _dot(keys(hh, c0 + half), qx0_t[:, half * blk:(half + 1) * blk])
            s_own = jnp.where(row <= col, s_own, MOBA_MASK)
            yield
            state.append(start(s_own, values_t(hh, c0 + half)))
            yield
        gate_t = jnp.dot(scratch[3 * hh][...], q_t, preferred_element_type=F32, precision=HI)[:nbp]
        gm = jnp.where(bid < c_row, gate_t, NEG_INF)
        sel = jnp.zeros((nbp, tq), F32)
        for s in range(MOBA_TOPK):
            mx = jnp.max(gm, axis=0, keepdims=True)
            idx = jnp.min(jnp.where(gm == mx, bid, LANES), axis=0, keepdims=True)
            pick = bid == idx
            sel = jnp.where(jnp.logical_and(pick, s < c_row), 1.0, sel)
            gm = jnp.where(pick, -3e38, gm)
        mask_t = jnp.where(sel > 0.5, 0.0, MOBA_MASK)
        ext_t = jnp.concatenate([mask_t, jnp.zeros((LANES - nbp, tq), F32)], axis=0)
        ext_t = jnp.where(rid < nb, ext_t, jnp.where(rid < nb + 3, 1.0, 0.0))
        qx[hh] = jnp.concatenate([q_t16, ext_t.astype(BF16)], axis=0)
        yield
        s_c0 = _dot(keys(hh, c0), qx[hh][:, blk:])
        yield
        second = update(state[1], s_c0, values_t(hh, c0))
        carry0[hh] = tuple(jnp.concatenate([a, b], axis=1) for a, b in zip(state[0], second))

    for _ in itertools.zip_longest(*[prologue(hh) for hh in range(hps)]):
        pass

    def body(p, carries):
        scores = [_dot(keys(hh, 2 * p, 2), qx[hh]) for hh in range(hps)]
        return tuple(update(carries[hh], scores[hh], values_t(hh, 2 * p, 2)) for hh in range(hps))

    final = lax.fori_loop(0, cq, body, tuple(carry0))
    for hh in range(hps):
        _, l_f, acc_f = final[hh]
        o_t = acc_f / l_f
        o_t = o_t * lax.rsqrt(jnp.mean(o_t * o_t, axis=0, keepdims=True) + RMS_EPS)
        o_ref[:, head_lanes(hh)] = (jnp.transpose(o_t) * onw_ref[...]).astype(o_ref.dtype)


def moba_heads(proj, out_norm_w, batch, seq, col_off):
    n = batch * seq
    blk = MOBA_BLOCK
    nb = seq // blk
    nq = nb // 2
    hps = MOBA_HEADS_PER_STEP
    ng = MOBA_HEADS // hps
    wide = hps * HEAD_DIM
    sec0 = col_off // hps
    slopes = jnp.exp2(-8.0 * jnp.arange(1, MOBA_HEADS + 1, dtype=F32) / MOBA_HEADS)
    grid_spec = pltpu.PrefetchScalarGridSpec(
        num_scalar_prefetch=1,
        grid=(batch, ng, nq),
        in_specs=[pl.BlockSpec((2 * blk, wide), lambda b, h, c, *_: (b * nq + c, sec0 + h)),
                  pl.BlockSpec((seq, wide), lambda b, h, c, *_: (b, sec0 + ng + h)),
                  pl.BlockSpec((seq, wide), lambda b, h, c, *_: (b, sec0 + 2 * ng + h)),
                  pl.BlockSpec((1, HEAD_DIM), lambda b, h, c, *_: (0, 0))],
        out_specs=pl.BlockSpec((2 * blk, wide), lambda b, h, c, *_: (b * nq + c, h)),
        scratch_shapes=[pltpu.VMEM((LANES, HEAD_DIM), F32),
                        pltpu.VMEM((nb * blk, HEAD_DIM + LANES), BF16),
                        pltpu.VMEM((HEAD_DIM, nb * blk), BF16)] * hps)
    assert nb % 2 == 0 and nb + 3 <= LANES and col_off % hps == 0
    return pl.pallas_call(
        functools.partial(_moba_body, n_blocks=nb),
        grid_spec=grid_spec,
        out_shape=jax.ShapeDtypeStruct((n, MOBA_WIDTH), BF16),
        compiler_params=_cparams(("arbitrary", "arbitrary", "arbitrary")),
        name="moba_heads",
    )(slopes, proj, proj, proj, out_norm_w.reshape(1, HEAD_DIM))


def _out_proj_body(x_ref, og_ref, om_ref, wg_ref, wm_ref, nw_ref, wrh_ref, wrl_ref, rb_ref,
                   x2_ref, hp_ref, gt_ref, id_ref):
    tm = x_ref.shape[0]
    x2 = x_ref[...] + _dot(og_ref[...], wg_ref[...]) + _dot(om_ref[...], wm_ref[...])
    x2_ref[...] = x2
    ms = jnp.mean(x2 * x2, axis=-1, keepdims=True)
    h2 = x2 * lax.rsqrt(ms + RMS_EPS) * nw_ref[...]
    h_hi = h2.astype(BF16)
    hp_ref[...] = h2
    h_lo = (h2 - h_hi.astype(F32)).astype(BF16)
    lg = _dot(h_hi, wrh_ref[...]) + _dot(h_hi, wrl_ref[...]) + _dot(h_lo, wrh_ref[...]) + rb_ref[...]

    lane = lax.broadcasted_iota(I32, (tm, LANES), 1)
    is_g = lane < N_GROUPS
    mg = jnp.max(jnp.where(is_g, lg, NEG_INF), axis=-1, keepdims=True)
    g_idx = jnp.min(jnp.where(jnp.logical_and(is_g, lg == mg), lane, LANES), axis=-1, keepdims=True)
    sum_g = jnp.sum(jnp.where(is_g, jnp.exp(lg - mg), 0.0), axis=-1, keepdims=True)
    p_top_g = 1.0 / sum_g
    lo = N_GROUPS + g_idx * EXPERTS_PER_GROUP
    in_grp = jnp.logical_and(lane >= lo, lane < lo + EXPERTS_PER_GROUP)
    m1 = jnp.max(jnp.where(in_grp, lg, NEG_INF), axis=-1, keepdims=True)
    i1 = jnp.min(jnp.where(jnp.logical_and(in_grp, lg == m1), lane, LANES), axis=-1, keepdims=True)
    rest = jnp.logical_and(in_grp, lane != i1)
    m2 = jnp.max(jnp.where(rest, lg, NEG_INF), axis=-1, keepdims=True)
    i2 = jnp.min(jnp.where(jnp.logical_and(rest, lg == m2), lane, LANES), axis=-1, keepdims=True)
    e2 = jnp.exp(m2 - m1)
    gate1 = p_top_g / (1.0 + e2)
    gate2 = p_top_g * e2 / (1.0 + e2)
    gt_ref[...] = jnp.where(lane == 0, gate1, jnp.where(lane == 1, gate2, 0.0))
    id_ref[...] = jnp.where(lane == 0, i1 - N_GROUPS, jnp.where(lane == 1, i2 - N_GROUPS, 0))


def out_proj(x, og, om, w_g, w_m, nw, wr_hi, wr_lo, r_bias, tm):
    n, d = x.shape
    const = lambda i: (0, 0)
    rows = lambda i: (i, 0)
    return pl.pallas_call(
        _out_proj_body,
        grid=(n // tm,),
        in_specs=[pl.BlockSpec((tm, d), rows),
                  pl.BlockSpec((tm, GDN_WIDTH), rows),
                  pl.BlockSpec((tm, MOBA_WIDTH), rows),
                  pl.BlockSpec((GDN_WIDTH, d), const),
                  pl.BlockSpec((MOBA_WIDTH, d), const),
                  pl.BlockSpec((1, d), const),
                  pl.BlockSpec((d, LANES), const),
                  pl.BlockSpec((d, LANES), const),
                  pl.BlockSpec((1, LANES), const)],
        out_specs=[pl.BlockSpec((tm, d), rows),
                   pl.BlockSpec((tm, d), rows),
                   pl.BlockSpec((tm, LANES), rows),
                   pl.BlockSpec((tm, LANES), rows)],
        out_shape=[jax.ShapeDtypeStruct((n, d), F32),
                   jax.ShapeDtypeStruct((n, d), F32),
                   jax.ShapeDtypeStruct((n, LANES), F32),
                   jax.ShapeDtypeStruct((n, LANES), I32)],
        compiler_params=_cparams(("arbitrary",)),
        name="out_proj_router",
    )(x, og, om, w_g, w_m, nw.reshape(1, d), wr_hi, wr_lo, r_bias)


def _invert_body(pos0_ref, pos1_ref, src_ref, zeros_vmem, sem):
    n_tok = pos0_ref.shape[0]
    zeros_vmem[...] = jnp.zeros_like(zeros_vmem)
    clear = pltpu.make_async_copy(zeros_vmem, src_ref, sem)
    clear.start()
    clear.wait()

    def fill(t, carry):
        src_ref[pos0_ref[t]] = t
        src_ref[pos1_ref[t]] = t
        return carry

    lax.fori_loop(0, n_tok, fill, 0, unroll=8)


def moe_source_rows(pos0, pos1, n_rows):
    grid_spec = pltpu.PrefetchScalarGridSpec(
        num_scalar_prefetch=2,
        grid=(1,),
        in_specs=[],
        out_specs=pl.BlockSpec(memory_space=pltpu.SMEM),
        scratch_shapes=[pltpu.VMEM((n_rows,), I32), pltpu.SemaphoreType.DMA(())])
    return pl.pallas_call(
        _invert_body,
        grid_spec=grid_spec,
        out_shape=jax.ShapeDtypeStruct((n_rows,), I32),
        compiler_params=_cparams(("arbitrary",)),
        name="moe_source_rows",
    )(pos0, pos1)


def _moe_body(blk_e_ref, nblk_ref, first_ref, slot_ref, next_e_ref, src_ref, h_hbm, wg_hbm, wu_hbm, wd_hbm,
              y_ref, xbuf, wg_buf, wu_buf, wd_buf, wg16, wu16, wd16, sem, gsem):
    rb = MOE_ROWS
    i = pl.program_id(0)
    e = blk_e_ref[i]
    slot = slot_ref[i]
    n_used = nblk_ref[0]
    nbuf = MOE_ROW_BUFFERS
    ahead = nbuf - 1
    xslot = lax.rem(i, nbuf)

    def row_copy(block, r):
        s = lax.rem(block, nbuf)
        tok = src_ref[block * rb + r]
        return pltpu.make_async_copy(h_hbm.at[pl.ds(tok, 1)], xbuf.at[s, pl.ds(r, 1)], gsem.at[s])

    def wait_rows(s):
        pltpu.make_async_copy(h_hbm.at[pl.ds(0, rb)], xbuf.at[s], gsem.at[s]).wait()

    @pl.when(i == 0)
    def _():
        for b in range(ahead):
            @pl.when(b < n_used)
            def _(b=b):
                def issue(r, carry):
                    row_copy(b, r).start()
                    return carry
                lax.fori_loop(0, rb, issue, 0, unroll=8)

    def weight_copies(expert, s):
        return (pltpu.make_async_copy(wg_hbm.at[expert], wg_buf.at[s], sem.at[s]),
                pltpu.make_async_copy(wu_hbm.at[expert], wu_buf.at[s], sem.at[s]),
                pltpu.make_async_copy(wd_hbm.at[expert], wd_buf.at[s], sem.at[s]))

    @pl.when(i == 0)
    def _():
        for cp in weight_copies(e, slot):
            cp.start(priority=1)

    @pl.when(first_ref[i] == 1)
    def _():
        for cp in weight_copies(e, slot):
            cp.wait()

        @pl.when(next_e_ref[i] >= 0)
        def _():
            for cp in weight_copies(next_e_ref[i], 1 - slot):
                cp.start(priority=1)

        wg16[...] = wg_buf[slot].astype(BF16)
        wu16[...] = wu_buf[slot].astype(BF16)
        wd16[...] = wd_buf[slot].astype(BF16)

    def compute(xb):
        g = _dot(xb, wg16[...])
        u = _dot(xb, wu16[...])
        hm = (g * _sigmoid(g) * u).astype(BF16)
        y_ref[...] = _dot(hm, wd16[...])

    @pl.when(i < n_used - ahead)
    def _():
        wait_rows(xslot)
        xb = xbuf[xslot].astype(BF16)
        for r in range(rb):
            row_copy(i + ahead, r).start()
        compute(xb)

    @pl.when(jnp.logical_and(i >= n_used - ahead, i < n_used))
    def _():
        wait_rows(xslot)
        compute(xbuf[xslot].astype(BF16))

    @pl.when(i >= n_used)
    def _():
        y_ref[...] = jnp.zeros_like(y_ref)


def moe_experts(h2, src_rows, blk_expert, n_used, first, slot, next_e, w_gate, w_up, w_down):
    rb = MOE_ROWS
    n_rows = src_rows.shape[0]
    d = h2.shape[1]
    de = w_gate.shape[-1]
    grid_spec = pltpu.PrefetchScalarGridSpec(
        num_scalar_prefetch=6,
        grid=(n_rows // rb,),
        in_specs=[pl.BlockSpec(memory_space=pl.ANY),
                  pl.BlockSpec(memory_space=pl.ANY),
                  pl.BlockSpec(memory_space=pl.ANY),
                  pl.BlockSpec(memory_space=pl.ANY)],
        out_specs=pl.BlockSpec((rb, d), lambda i, *_: (i, 0)),
        scratch_shapes=[pltpu.VMEM((MOE_ROW_BUFFERS, rb, d), F32),
                        pltpu.VMEM((2, d, de), F32), pltpu.VMEM((2, d, de), F32), pltpu.VMEM((2, de, d), F32),
                        pltpu.VMEM((d, de), BF16), pltpu.VMEM((d, de), BF16), pltpu.VMEM((de, d), BF16),
                        pltpu.SemaphoreType.DMA((2,)), pltpu.SemaphoreType.DMA((MOE_ROW_BUFFERS,))])
    return pl.pallas_call(
        _moe_body,
        grid_spec=grid_spec,
        out_shape=jax.ShapeDtypeStruct((n_rows, d), F32),
        compiler_params=_cparams(("arbitrary",)),
        name="moe_experts",
    )(blk_expert, n_used, first, slot, next_e, src_rows, h2, w_gate, w_up, w_down)


def _combine_body(pos0_ref, pos1_ref, y_hbm, x2_ref, gt_ref, nw_ref, o_ref, buf, sem):
    tf = COMBINE_ROWS
    nbuf = COMBINE_BUFFERS
    ahead = nbuf - 1
    i = pl.program_id(0)
    n_steps = pl.num_programs(0)
    slot = lax.rem(i, nbuf)

    def start_row(step, r):
        sl = lax.rem(step, nbuf)
        t = step * tf + r
        pltpu.make_async_copy(y_hbm.at[pl.ds(pos0_ref[t], 1)], buf.at[sl, 0, pl.ds(r, 1)],
                              sem.at[sl]).start()
        pltpu.make_async_copy(y_hbm.at[pl.ds(pos1_ref[t], 1)], buf.at[sl, 1, pl.ds(r, 1)],
                              sem.at[sl]).start(priority=1)

    @pl.when(i == 0)
    def _():
        for b in range(ahead):
            def body(r, carry, b=b):
                start_row(b, r)
                return carry
            lax.fori_loop(0, tf, body, 0, unroll=8)

    def wait_rows():
        for kk in range(TOPK_IN_GROUP):
            pltpu.make_async_copy(y_hbm.at[pl.ds(0, tf)], buf.at[slot, kk], sem.at[slot]).wait()

    def finish(y0, y1):
        gt = gt_ref[...]
        xo = x2_ref[...] + gt[:, 0:1] * y0 + gt[:, 1:2] * y1
        ms = jnp.mean(xo * xo, axis=-1, keepdims=True)
        o_ref[...] = xo * lax.rsqrt(ms + RMS_EPS) * nw_ref[...]

    @pl.when(i + ahead < n_steps)
    def _():
        wait_rows()
        y0 = buf[slot, 0]
        y1 = buf[slot, 1]
        for r in range(tf):
            start_row(i + ahead, r)
        finish(y0, y1)

    @pl.when(i + ahead >= n_steps)
    def _():
        wait_rows()
        finish(buf[slot, 0], buf[slot, 1])


def moe_combine(yb, pos0, pos1, x2, gates, nw):
    n, d = x2.shape
    tf = COMBINE_ROWS
    grid_spec = pltpu.PrefetchScalarGridSpec(
        num_scalar_prefetch=2,
        grid=(n // tf,),
        in_specs=[pl.BlockSpec(memory_space=pl.ANY),
                  pl.BlockSpec((tf, d), lambda i, *_: (i, 0)),
                  pl.BlockSpec((tf, LANES), lambda i, *_: (i, 0)),
                  pl.BlockSpec((1, d), lambda i, *_: (0, 0))],
        out_specs=pl.BlockSpec((tf, d), lambda i, *_: (i, 0)),
        scratch_shapes=[pltpu.VMEM((COMBINE_BUFFERS, 2, tf, d), F32), pltpu.SemaphoreType.DMA((COMBINE_BUFFERS,))])
    return pl.pallas_call(
        _combine_body,
        grid_spec=grid_spec,
        out_shape=jax.ShapeDtypeStruct((n, d), F32),
        compiler_params=_cparams(("arbitrary",)),
        name="moe_combine",
    )(pos0, pos1, yb, x2, gates, nw.reshape(1, d))


def _dispatch_plan(expert_id):
    n_tok, k = expert_id.shape
    rb = MOE_ROWS
    n_assign = n_tok * k
    e_flat = expert_id.reshape(-1)
    onehot = (e_flat[:, None] == jnp.arange(N_EXPERTS, dtype=I32)[None, :]).astype(I32)
    csum = jnp.cumsum(onehot, axis=0)
    counts = csum[-1]
    padded = (counts + rb - 1) // rb * rb
    pad_end = jnp.cumsum(padded)
    pad_start = pad_end - padded
    pos = jnp.sum(onehot * (pad_start[None, :] + csum - 1), axis=1).astype(I32).reshape(n_tok, k)
    n_rb = -(-n_assign // rb) + N_EXPERTS
    blk_start = jnp.arange(n_rb, dtype=I32) * rb
    blk_expert = jnp.minimum(jnp.sum((pad_end[None, :] <= blk_start[:, None]).astype(I32), axis=1),
                             N_EXPERTS - 1).astype(I32)
    n_used = (pad_end[-1] // rb).astype(I32).reshape(1)
    blk = jnp.arange(n_rb, dtype=I32)
    prev_expert = jnp.concatenate([jnp.full((1,), -1, I32), blk_expert[:-1]])
    first = jnp.logical_and(blk < n_used[0], blk_expert != prev_expert).astype(I32)
    slot = jnp.bitwise_and(jnp.cumsum(first) - 1, 1).astype(I32)
    ids = jnp.arange(N_EXPERTS, dtype=I32)
    later = jnp.logical_and(ids[None, :] > ids[:, None], counts[None, :] > 0)
    next_nonempty = jnp.min(jnp.where(later, ids[None, :], N_EXPERTS), axis=1)
    next_nonempty = jnp.where(next_nonempty < N_EXPERTS, next_nonempty, -1).astype(I32)
    next_e = next_nonempty[blk_expert]
    return pos, blk_expert, n_used, first, slot, next_e, n_rb * rb


def kernel(x, norm_mix_w, w_in, gdn_conv_w, gdn_A_log, gdn_dt_bias, gdn_out_norm_w, moba_out_norm_w, w_out, norm_ffn_w, w_router_group, b_router_group, w_router_expert, b_router_expert, w_expert_gate, w_expert_up, w_expert_down, norm_final_w):
    batch, seq, d = x.shape
    n = batch * seq
    assert w_in.shape[0] == 1, "the final norm is fused into the last layer's combine; one layer supported"
    l = 0
    xf = x.reshape(n, d).astype(F32)
    gw = 4 * GDN_WIDTH
    w_l = w_in[l]
    mq0 = gw + 2 * GDN_HEADS
    col_scale = jnp.where(jnp.arange(3 * MOBA_WIDTH) < MOBA_WIDTH, HEAD_DIM ** -0.5 * LOG2E, 1.0).astype(F32)
    w_t = jnp.swapaxes(w_l, 0, 1)
    w_gdn = w_t[:gw].astype(BF16)
    w_moba = (w_t[mq0:] * col_scale[:, None]).astype(BF16)
    w_ba = jnp.pad(w_t[gw:gw + 2 * GDN_HEADS], ((0, LANES - 2 * GDN_HEADS), (0, 0))).astype(BF16)
    proj_g, ba = norm_matmul(xf, norm_mix_w[l], w_gdn, BF16, IN_PROJ_ROWS, gw // 2, w_small=w_ba)
    proj_m = norm_matmul(xf, norm_mix_w[l], w_moba, BF16, IN_PROJ_ROWS, 3 * MOBA_WIDTH // 2)
    og = gdn_heads(proj_g, ba, gdn_conv_w[l].astype(F32), gdn_A_log[l], gdn_dt_bias[l],
                   gdn_out_norm_w[l].astype(F32), batch, seq)
    om = moba_heads(proj_m, moba_out_norm_w[l].astype(F32), batch, seq, 0)

    w_o = w_out[l].astype(BF16)
    n_r = N_GROUPS + N_EXPERTS
    w_router = jnp.pad(jnp.concatenate([w_router_group[l], w_router_expert[l]], axis=1).astype(F32),
                       ((0, 0), (0, LANES - n_r)))
    wr_hi = w_router.astype(BF16)
    wr_lo = (w_router - wr_hi.astype(F32)).astype(BF16)
    r_bias = jnp.pad(jnp.concatenate([b_router_group[l], b_router_expert[l]]).astype(F32),
                     (0, LANES - n_r)).reshape(1, LANES)
    x2, hp, gates, ids = out_proj(xf, og, om, w_o[:GDN_WIDTH], w_o[GDN_WIDTH:], norm_ffn_w[l].astype(F32),
                                  wr_hi, wr_lo, r_bias, 512)

    pos, blk_expert, n_used, first, slot, next_e, n_rows = _dispatch_plan(ids[:, :TOPK_IN_GROUP])
    pos0, pos1 = pos[:, 0], pos[:, 1]
    src_rows = moe_source_rows(pos0, pos1, n_rows)
    yb = moe_experts(hp, src_rows, blk_expert, n_used, first, slot, next_e, w_expert_gate[l].astype(F32),
                     w_expert_up[l].astype(F32), w_expert_down[l].astype(F32))
    out = moe_combine(yb, pos0, pos1, x2, gates, norm_final_w.astype(F32))
    return out.reshape(batch, seq, d).astype(x.dtype)
```

```python
import functools
import itertools

import jax
import jax.numpy as jnp
from jax import lax
from jax.experimental import pallas as pl
from jax.experimental.pallas import tpu as pltpu

F32 = jnp.float32
BF16 = jnp.bfloat16
U32 = jnp.uint32
I32 = jnp.int32

HEAD_DIM = 128
GDN_HEADS = 8
MOBA_HEADS = 8
GDN_WIDTH = GDN_HEADS * HEAD_DIM
MOBA_WIDTH = MOBA_HEADS * HEAD_DIM
CONV_WIDTH = 4
GDN_CHUNK = 64
MOBA_BLOCK = 256
MOBA_TOPK = 3
N_GROUPS = 4
EXPERTS_PER_GROUP = 8
N_EXPERTS = N_GROUPS * EXPERTS_PER_GROUP
TOPK_IN_GROUP = 2
RMS_EPS = 1e-6
NEG_INF = -1e30
MOBA_MASK = -(2.0 ** 99)
LOG2E = 1.4426950408889634
LANES = 128
VMEM_LIMIT = 56 * 1024 * 1024

IN_PROJ_ROWS = 1024
GDN_TILE = 128
GDN_HEADS_PER_STEP = 8
MOBA_HEADS_PER_STEP = 4
MOE_ROW_BUFFERS = 3
MOE_ROWS = 256
COMBINE_BUFFERS = 3
COMBINE_ROWS = 256
HI = lax.Precision.HIGHEST


def _cparams(sem, **kw):
    return pltpu.CompilerParams(dimension_semantics=sem, vmem_limit_bytes=VMEM_LIMIT, **kw)


def _dot(a, b):
    return jnp.dot(a, b, preferred_element_type=F32)


def _dot_nt(a, b, precision=None):
    return lax.dot_general(a, b, (((1,), (1,)), ((), ())), preferred_element_type=F32,
                           precision=precision)


def _sigmoid(x):
    return 1.0 / (1.0 + jnp.exp(-x))


def _norm_matmul_body(x_ref, nw_ref, w_ref, *rest):
    x = x_ref[...]
    ms = jnp.mean(x * x, axis=-1, keepdims=True)
    h = (x * lax.rsqrt(ms + RMS_EPS) * nw_ref[...]).astype(BF16)
    if len(rest) == 1:
        rest[0][...] = _dot_nt(h, w_ref[...]).astype(rest[0].dtype)
    else:
        ws_ref, o_ref, os_ref = rest
        o_ref[...] = _dot_nt(h, w_ref[...]).astype(o_ref.dtype)
        os_ref[0] = _dot_nt(h, ws_ref[...])


def norm_matmul(x, nw, w, out_dtype, tm, tn, w_small=None):
    n, d = x.shape
    width = w.shape[0]
    in_specs = [pl.BlockSpec((tm, d), lambda j, i: (i, 0)),
                pl.BlockSpec((1, d), lambda j, i: (0, 0)),
                pl.BlockSpec((tn, d), lambda j, i: (j, 0))]
    out_specs = pl.BlockSpec((tm, tn), lambda j, i: (i, j))
    out_shape = jax.ShapeDtypeStruct((n, width), out_dtype)
    operands = (x, nw.reshape(1, d), w)
    if w_small is not None:
        ws = w_small.shape[0]
        in_specs.append(pl.BlockSpec((ws, d), lambda j, i: (0, 0)))
        out_specs = [out_specs, pl.BlockSpec((1, tm, ws), lambda j, i: (j, i, 0))]
        out_shape = [out_shape, jax.ShapeDtypeStruct((width // tn, n, ws), F32)]
        operands = operands + (w_small,)
    outs = pl.pallas_call(
        _norm_matmul_body,
        grid=(width // tn, n // tm),
        in_specs=in_specs,
        out_specs=out_specs,
        out_shape=out_shape,
        compiler_params=_cparams(("arbitrary", "arbitrary")),
        name="norm_in_proj",
    )(*operands)
    if w_small is None:
        return outs
    return outs[0], outs[1][0]


def _gdn_body(alog_ref, dtb_ref, q_ref, k_ref, v_ref, z_ref, ba_ref, cwq_ref, cwk_ref, cwv_ref,
              onw_ref, o_ref, *scratch):
    tt = GDN_TILE
    c = GDN_CHUNK
    t = pl.program_id(2)

    @pl.when(t == 0)
    def _():
        for hh in range(GDN_HEADS_PER_STEP):
            s_ref, prev_ref, vn_ref, _ = scratch[4 * hh:4 * hh + 4]
            s_ref[...] = jnp.zeros_like(s_ref)
            prev_ref[...] = jnp.zeros_like(prev_ref)
            vn_ref[...] = jnp.zeros_like(vn_ref)

    row_8 = lax.broadcasted_iota(jnp.int32, (8, LANES), 0)
    lane_t = lax.broadcasted_iota(jnp.int32, (tt, LANES), 1)
    row = lax.broadcasted_iota(jnp.int32, (tt, tt), 0)
    col = lax.broadcasted_iota(jnp.int32, (tt, tt), 1)
    same = (row // c) == (col // c)
    causal = jnp.logical_and(same, col <= row)
    strict = jnp.logical_and(same, col < row)
    causal16 = jnp.where(causal, 1.0, 0.0).astype(BF16)
    same16 = jnp.where(same, 1.0, 0.0).astype(BF16)
    eye = jnp.where(row == col, 1.0, 0.0)
    colk = lax.broadcasted_iota(jnp.int32, (HEAD_DIM, tt), 1)
    ba = ba_ref[...]

    heads = [_gdn_one_head(hh, pl.program_id(1) * GDN_HEADS_PER_STEP + hh, alog_ref, dtb_ref, q_ref, k_ref,
                           v_ref, z_ref, ba, cwq_ref, cwk_ref, cwv_ref, onw_ref, o_ref,
                           *scratch[4 * hh:4 * hh + 4],
                           row_8, lane_t, causal, strict, causal16, same16, eye, colk)
             for hh in range(GDN_HEADS_PER_STEP)]
    for _ in itertools.zip_longest(*heads):
        pass


def _gdn_one_head(hh, h, alog_ref, dtb_ref, q_ref, k_ref, v_ref, z_ref, ba, cwq_ref, cwk_ref, cwv_ref,
                  onw_ref, o_ref, s_ref, prev_ref, vn_ref, oacc_ref,
                  row_8, lane_t, causal, strict, causal16, same16, eye, colk):
    tt = GDN_TILE
    c = GDN_CHUNK
    lanes = slice(hh * HEAD_DIM, (hh + 1) * HEAD_DIM)

    def conv_silu(x_ref, cw_ref, idx):
        x = x_ref[:, lanes].astype(F32)
        p = prev_ref[idx]
        w = cw_ref[:, lanes]
        acc = x * w[CONV_WIDTH - 1:CONV_WIDTH, :]
        for s in range(1, CONV_WIDTH):
            xs = pltpu.roll(x, s, axis=0)
            head = jnp.where(row_8 < s, pltpu.roll(p, s, axis=0), xs[:8])
            xs = jnp.concatenate([head, xs[8:]], axis=0)
            acc = acc + xs * w[CONV_WIDTH - 1 - s:CONV_WIDTH - s, :]
        prev_ref[idx] = x[tt - 8:tt, :]
        return acc * _sigmoid(acc)

    q = conv_silu(q_ref, cwq_ref, 0)
    k = conv_silu(k_ref, cwk_ref, 1)
    v = conv_silu(v_ref, cwv_ref, 2)
    q = q * lax.rsqrt(jnp.sum(q * q, axis=-1, keepdims=True) + 1e-6) * (HEAD_DIM ** -0.5)
    k = k * lax.rsqrt(jnp.sum(k * k, axis=-1, keepdims=True) + 1e-6)
    yield

    b_col = jnp.sum(jnp.where(lane_t == h, ba, 0.0), axis=-1, keepdims=True)
    a_col = jnp.sum(jnp.where(lane_t == h + GDN_HEADS, ba, 0.0), axis=-1, keepdims=True)
    beta = _sigmoid(b_col)
    xa = a_col + dtb_ref[h]
    softplus = jnp.maximum(xa, 0.0) + jnp.log(1.0 + jnp.exp(-jnp.abs(xa)))
    g = -jnp.exp(jnp.full((1, 1), alog_ref[h], F32)) * softplus

    g_hi = g.astype(BF16).astype(F32)
    g_mid = (g - g_hi).astype(BF16).astype(F32)
    g_lo = g - g_hi - g_mid
    g3 = jnp.where(lane_t == 0, g_hi, jnp.where(lane_t == 1, g_mid, jnp.where(lane_t == 2, g_lo, 0.0)))
    g3 = g3.astype(BF16)

    def three(r):
        return r[:, 0:1] + r[:, 1:2] + r[:, 2:3]

    gc_col = three(_dot(causal16, g3))
    gc_b = jnp.broadcast_to(gc_col, (tt, LANES))
    glast_b = jnp.broadcast_to(three(_dot(same16, g3)), (tt, LANES))
    gc_row = jnp.transpose(gc_b)[0:1, :]
    decay = jnp.exp(jnp.where(causal, gc_col - gc_row, NEG_INF))
    yield

    kb = k * beta
    k16 = k.astype(BF16)
    kk = _dot_nt(kb.astype(BF16), k16)
    lmat = jnp.where(strict, kk * decay, 0.0)
    yield
    attn = _dot_nt(q.astype(BF16), k16) * decay

    tinv = eye - lmat
    m16 = lmat.astype(BF16)
    for _ in range(5):
        m16 = _dot(m16, m16).astype(BF16)
        yield
        tinv = tinv + _dot(tinv.astype(BF16), m16)
        yield

    egc = jnp.exp(gc_b)
    rhs = jnp.concatenate([v * beta, kb * egc], axis=-1)
    sol = _dot(tinv.astype(BF16), rhs.astype(BF16))
    yield
    u = sol[:, :HEAD_DIM]
    w16 = sol[:, HEAD_DIM:].astype(BF16)
    qd16 = (q * egc).astype(BF16)
    kd = k * jnp.exp(glast_b - gc_b)
    kdt = jnp.transpose(kd)
    gtot_b = jnp.exp(glast_b)
    attn16 = attn.astype(BF16)

    s = s_ref[...]
    for n in range(tt // c):
        sl = slice(n * c, (n + 1) * c)
        s16 = s.astype(BF16)
        v_new = u[sl] - _dot(w16[sl], s16)
        vn_ref[sl, :] = v_new
        yield
        vn16 = vn_ref[...].astype(BF16)
        oacc_ref[sl, :] = _dot(qd16[sl], s16) + _dot(attn16[sl], vn16)
        kdt_n = jnp.where((colk // c) == n, kdt, 0.0).astype(BF16)
        s = s * gtot_b[n * c:n * c + 1, :] + _dot(kdt_n, vn16)
        yield
    s_ref[...] = s

    o = oacc_ref[...]
    o = o * lax.rsqrt(jnp.mean(o * o, axis=-1, keepdims=True) + RMS_EPS) * onw_ref[...]
    z = z_ref[:, lanes].astype(F32)
    o_ref[:, lanes] = (o * (z * _sigmoid(z))).astype(o_ref.dtype)


def gdn_heads(proj, ba, conv_w, a_log, dt_bias, out_norm_w, batch, seq):
    n = batch * seq
    tt = GDN_TILE
    nt = seq // tt
    hps = GDN_HEADS_PER_STEP
    ng = GDN_HEADS // hps
    wide = hps * HEAD_DIM

    def col_spec(section):
        return pl.BlockSpec((tt, wide), lambda b, h, t, *_: (b * nt + t, section * ng + h))

    def cw_spec(section):
        return pl.BlockSpec((CONV_WIDTH, wide), lambda b, h, t, *_: (0, section * ng + h))

    grid_spec = pltpu.PrefetchScalarGridSpec(
        num_scalar_prefetch=2,
        grid=(batch, ng, nt),
        in_specs=[col_spec(0), col_spec(1), col_spec(2), col_spec(3),
                  pl.BlockSpec((tt, LANES), lambda b, h, t, *_: (b * nt + t, 0)),
                  cw_spec(0), cw_spec(1), cw_spec(2),
                  pl.BlockSpec((1, HEAD_DIM), lambda b, h, t, *_: (0, 0))],
        out_specs=pl.BlockSpec((tt, wide), lambda b, h, t, *_: (b * nt + t, h)),
        scratch_shapes=[pltpu.VMEM((HEAD_DIM, HEAD_DIM), F32),
                        pltpu.VMEM((3, 8, HEAD_DIM), F32),
                        pltpu.VMEM((tt, HEAD_DIM), F32),
                        pltpu.VMEM((tt, HEAD_DIM), F32)] * hps)
    return pl.pallas_call(
        _gdn_body,
        grid_spec=grid_spec,
        out_shape=jax.ShapeDtypeStruct((n, GDN_WIDTH), BF16),
        compiler_params=_cparams(("arbitrary", "arbitrary", "arbitrary")),
        name="gdn_heads",
    )(a_log.astype(F32), dt_bias.astype(F32), proj, proj, proj, proj, ba,
      conv_w, conv_w, conv_w, out_norm_w.reshape(1, HEAD_DIM))


def _moba_body(slope_ref, q_ref, k_ref, v_ref, onw_ref, o_ref, *scratch, n_blocks):
    blk = MOBA_BLOCK
    tq = 2 * blk
    nb = n_blocks
    nbp = -(-nb // 8) * 8
    hps = MOBA_HEADS_PER_STEP
    cq = pl.program_id(2)
    c0 = 2 * cq
    lane = lax.broadcasted_iota(I32, (blk, LANES), 1)
    row = lax.broadcasted_iota(I32, (blk, blk), 0)
    col = lax.broadcasted_iota(I32, (blk, blk), 1)
    rid = lax.broadcasted_iota(I32, (LANES, tq), 0)
    bid = lax.broadcasted_iota(I32, (nbp, tq), 0)
    c_row = c0 + (lax.broadcasted_iota(I32, (nbp, tq), 1) >= blk).astype(I32)
    ones_ext_t = jnp.where(jnp.logical_and(rid >= nb, rid < nb + 3), 1.0, 0.0).astype(BF16)

    def head_lanes(hh):
        return slice(hh * HEAD_DIM, (hh + 1) * HEAD_DIM)

    def keys(hh, j, nblk=1):
        return scratch[3 * hh + 1][pl.ds(pl.multiple_of(j * blk, blk), nblk * blk), :]

    def values_t(hh, j, nblk=1):
        return scratch[3 * hh + 2][:, pl.ds(pl.multiple_of(j * blk, blk), nblk * blk)]

    @pl.when(cq == 0)
    def _():
        t_in = lax.broadcasted_iota(I32, (blk, LANES), 0)
        for hh in range(hps):
            kmean_ref, kx_ref, vt_ref = scratch[3 * hh:3 * hh + 3]
            slope2 = slope_ref[pl.program_id(1) * hps + hh] * LOG2E
            kmean_ref[...] = jnp.zeros_like(kmean_ref)
            for j in range(nb):
                kj = k_ref[j * blk:(j + 1) * blk, head_lanes(hh)]
                kmean_ref[j:j + 1, :] = jnp.mean(kj.astype(F32), axis=0, keepdims=True)
                bias = slope2 * (t_in + j * blk).astype(F32)
                b_hi = bias.astype(BF16)
                r1 = bias - b_hi.astype(F32)
                b_mid = r1.astype(BF16)
                b_lo = (r1 - b_mid.astype(F32)).astype(BF16)
                ext = jnp.where(lane == j, 1.0, 0.0).astype(BF16)
                ext = jnp.where(lane == nb, b_hi, ext)
                ext = jnp.where(lane == nb + 1, b_mid, ext)
                ext = jnp.where(lane == nb + 2, b_lo, ext)
                kx_ref[j * blk:(j + 1) * blk, :] = jnp.concatenate([kj, ext], axis=1)
                vj = v_ref[j * blk:(j + 1) * blk, head_lanes(hh)].astype(F32)
                vt_ref[:, j * blk:(j + 1) * blk] = jnp.transpose(vj).astype(BF16)

    def start(s, vals_t):
        m = jnp.max(s, axis=0, keepdims=True)
        p = jnp.exp2(s - m)
        return m, jnp.sum(p, axis=0, keepdims=True), _dot(vals_t, p.astype(BF16))

    def update(carry, s, vals_t):
        m_i, l_i, acc = carry
        m_new = jnp.maximum(m_i, jnp.max(s, axis=0, keepdims=True))
        alpha = jnp.exp2(m_i - m_new)
        p = jnp.exp2(s - m_new)
        return (m_new, l_i * alpha + jnp.sum(p, axis=0, keepdims=True),
                acc * alpha + _dot(vals_t, p.astype(BF16)))

    qx = [None] * hps
    carry0 = [None] * hps

    def prologue(hh):
        q_t = jnp.transpose(q_ref[:, head_lanes(hh)].astype(F32))
        q_t16 = q_t.astype(BF16)
        yield
        qx0_t = jnp.concatenate([q_t16, ones_ext_t], axis=0)
        state = []
        for half in range(2):
            s_own = _dot(keys(hh, c0 + half), qx0_t[:, half * blk:(half + 1) * blk])
            s_own = jnp.where(row <= col, s_own, MOBA_MASK)
            yield
            state.append(start(s_own, values_t(hh, c0 + half)))
            yield
        gate_t = jnp.dot(scratch[3 * hh][...], q_t, preferred_element_type=F32, precision=HI)[:nbp]
        gm = jnp.where(bid < c_row, gate_t, NEG_INF)
        sel = jnp.zeros((nbp, tq), F32)
        for s in range(MOBA_TOPK):
            mx = jnp.max(gm, axis=0, keepdims=True)
            idx = jnp.min(jnp.where(gm == mx, bid, LANES), axis=0, keepdims=True)
            pick = bid == idx
            sel = jnp.where(jnp.logical_and(pick, s < c_row), 1.0, sel)
            gm = jnp.where(pick, -3e38, gm)
        mask_t = jnp.where(sel > 0.5, 0.0, MOBA_MASK)
        ext_t = jnp.concatenate([mask_t, jnp.zeros((LANES - nbp, tq), F32)], axis=0)
        ext_t = jnp.where(rid < nb, ext_t, jnp.where(rid < nb + 3, 1.0, 0.0))
        qx[hh] = jnp.concatenate([q_t16, ext_t.astype(BF16)], axis=0)
        yield
        s_c0 = _dot(keys(hh, c0), qx[hh][:, blk:])
        yield
        second = update(state[1], s_c0, values_t(hh, c0))
        carry0[hh] = tuple(jnp.concatenate([a, b], axis=1) for a, b in zip(state[0], second))

    for _ in itertools.zip_longest(*[prologue(hh) for hh in range(hps)]):
        pass

    def body(p, carries):
        scores = [_dot(keys(hh, 2 * p, 2), qx[hh]) for hh in range(hps)]
        return tuple(update(carries[hh], scores[hh], values_t(hh, 2 * p, 2)) for hh in range(hps))

    final = lax.fori_loop(0, cq, body, tuple(carry0))
    for hh in range(hps):
        _, l_f, acc_f = final[hh]
        o_t = acc_f / l_f
        o_t = o_t * lax.rsqrt(jnp.mean(o_t * o_t, axis=0, keepdims=True) + RMS_EPS)
        o_ref[:, head_lanes(hh)] = (jnp.transpose(o_t) * onw_ref[...]).astype(o_ref.dtype)


def moba_heads(proj, out_norm_w, batch, seq, col_off):
    n = batch * seq
    blk = MOBA_BLOCK
    nb = seq // blk
    nq = nb // 2
    hps = MOBA_HEADS_PER_STEP
    ng = MOBA_HEADS // hps
    wide = hps * HEAD_DIM
    sec0 = col_off // hps
    slopes = jnp.exp2(-8.0 * jnp.arange(1, MOBA_HEADS + 1, dtype=F32) / MOBA_HEADS)
    grid_spec = pltpu.PrefetchScalarGridSpec(
        num_scalar_prefetch=1,
        grid=(batch, ng, nq),
        in_specs=[pl.BlockSpec((2 * blk, wide), lambda b, h, c, *_: (b * nq + c, sec0 + h)),
                  pl.BlockSpec((seq, wide), lambda b, h, c, *_: (b, sec0 + ng + h)),
                  pl.BlockSpec((seq, wide), lambda b, h, c, *_: (b, sec0 + 2 * ng + h)),
                  pl.BlockSpec((1, HEAD_DIM), lambda b, h, c, *_: (0, 0))],
        out_specs=pl.BlockSpec((2 * blk, wide), lambda b, h, c, *_: (b * nq + c, h)),
        scratch_shapes=[pltpu.VMEM((LANES, HEAD_DIM), F32),
                        pltpu.VMEM((nb * blk, HEAD_DIM + LANES), BF16),
                        pltpu.VMEM((HEAD_DIM, nb * blk), BF16)] * hps)
    assert nb % 2 == 0 and nb + 3 <= LANES and col_off % hps == 0
    return pl.pallas_call(
        functools.partial(_moba_body, n_blocks=nb),
        grid_spec=grid_spec,
        out_shape=jax.ShapeDtypeStruct((n, MOBA_WIDTH), BF16),
        compiler_params=_cparams(("arbitrary", "arbitrary", "arbitrary")),
        name="moba_heads",
    )(slopes, proj, proj, proj, out_norm_w.reshape(1, HEAD_DIM))


def _out_proj_body(x_ref, og_ref, om_ref, wg_ref, wm_ref, nw_ref, wrh_ref, wrl_ref, rb_ref,
                   x2_ref, hp_ref, gt_ref, id_ref):
    tm = x_ref.shape[0]
    x2 = x_ref[...] + _dot(og_ref[...], wg_ref[...]) + _dot(om_ref[...], wm_ref[...])
    x2_ref[...] = x2
    ms = jnp.mean(x2 * x2, axis=-1, keepdims=True)
    h2 = x2 * lax.rsqrt(ms + RMS_EPS) * nw_ref[...]
    h_hi = h2.astype(BF16)
    hp_ref[...] = h2
    h_lo = (h2 - h_hi.astype(F32)).astype(BF16)
    lg = _dot(h_hi, wrh_ref[...]) + _dot(h_hi, wrl_ref[...]) + _dot(h_lo, wrh_ref[...]) + rb_ref[...]

    lane = lax.broadcasted_iota(I32, (tm, LANES), 1)
    is_g = lane < N_GROUPS
    mg = jnp.max(jnp.where(is_g, lg, NEG_INF), axis=-1, keepdims=True)
    g_idx = jnp.min(jnp.where(jnp.logical_and(is_g, lg == mg), lane, LANES), axis=-1, keepdims=True)
    sum_g = jnp.sum(jnp.where(is_g, jnp.exp(lg - mg), 0.0), axis=-1, keepdims=True)
    p_top_g = 1.0 / sum_g
    lo = N_GROUPS + g_idx * EXPERTS_PER_GROUP
    in_grp = jnp.logical_and(lane >= lo, lane < lo + EXPERTS_PER_GROUP)
    m1 = jnp.max(jnp.where(in_grp, lg, NEG_INF), axis=-1, keepdims=True)
    i1 = jnp.min(jnp.where(jnp.logical_and(in_grp, lg == m1), lane, LANES), axis=-1, keepdims=True)
    rest = jnp.logical_and(in_grp, lane != i1)
    m2 = jnp.max(jnp.where(rest, lg, NEG_INF), axis=-1, keepdims=True)
    i2 = jnp.min(jnp.where(jnp.logical_and(rest, lg == m2), lane, LANES), axis=-1, keepdims=True)
    e2 = jnp.exp(m2 - m1)
    gate1 = p_top_g / (1.0 + e2)
    gate2 = p_top_g * e2 / (1.0 + e2)
    gt_ref[...] = jnp.where(lane == 0, gate1, jnp.where(lane == 1, gate2, 0.0))
    id_ref[...] = jnp.where(lane == 0, i1 - N_GROUPS, jnp.where(lane == 1, i2 - N_GROUPS, 0))


def out_proj(x, og, om, w_g, w_m, nw, wr_hi, wr_lo, r_bias, tm):
    n, d = x.shape
    const = lambda i: (0, 0)
    rows = lambda i: (i, 0)
    return pl.pallas_call(
        _out_proj_body,
        grid=(n // tm,),
        in_specs=[pl.BlockSpec((tm, d), rows),
                  pl.BlockSpec((tm, GDN_WIDTH), rows),
                  pl.BlockSpec((tm, MOBA_WIDTH), rows),
                  pl.BlockSpec((GDN_WIDTH, d), const),
                  pl.BlockSpec((MOBA_WIDTH, d), const),
                  pl.BlockSpec((1, d), const),
                  pl.BlockSpec((d, LANES), const),
                  pl.BlockSpec((d, LANES), const),
                  pl.BlockSpec((1, LANES), const)],
        out_specs=[pl.BlockSpec((tm, d), rows),
                   pl.BlockSpec((tm, d), rows),
                   pl.BlockSpec((tm, LANES), rows),
                   pl.BlockSpec((tm, LANES), rows)],
        out_shape=[jax.ShapeDtypeStruct((n, d), F32),
                   jax.ShapeDtypeStruct((n, d), F32),
                   jax.ShapeDtypeStruct((n, LANES), F32),
                   jax.ShapeDtypeStruct((n, LANES), I32)],
        compiler_params=_cparams(("arbitrary",)),
        name="out_proj_router",
    )(x, og, om, w_g, w_m, nw.reshape(1, d), wr_hi, wr_lo, r_bias)


def _invert_body(pos0_ref, pos1_ref, src_ref, zeros_vmem, sem):
    n_tok = pos0_ref.shape[0]
    zeros_vmem[...] = jnp.zeros_like(zeros_vmem)
    clear = pltpu.make_async_copy(zeros_vmem, src_ref, sem)
    clear.start()
    clear.wait()

    def fill(t, carry):
        src_ref[pos0_ref[t]] = t
        src_ref[pos1_ref[t]] = t
        return carry

    lax.fori_loop(0, n_tok, fill, 0, unroll=8)


def moe_source_rows(pos0, pos1, n_rows):
    grid_spec = pltpu.PrefetchScalarGridSpec(
        num_scalar_prefetch=2,
        grid=(1,),
        in_specs=[],
        out_specs=pl.BlockSpec(memory_space=pltpu.SMEM),
        scratch_shapes=[pltpu.VMEM((n_rows,), I32), pltpu.SemaphoreType.DMA(())])
    return pl.pallas_call(
        _invert_body,
        grid_spec=grid_spec,
        out_shape=jax.ShapeDtypeStruct((n_rows,), I32),
        compiler_params=_cparams(("arbitrary",)),
        name="moe_source_rows",
    )(pos0, pos1)


def _moe_body(blk_e_ref, nblk_ref, first_ref, slot_ref, next_e_ref, src_ref, h_hbm, wg_hbm, wu_hbm, wd_hbm,
              y_ref, xbuf, wg_buf, wu_buf, wd_buf, wg16, wu16, wd16, sem, gsem):
    rb = MOE_ROWS
    i = pl.program_id(0)
    e = blk_e_ref[i]
    slot = slot_ref[i]
    n_used = nblk_ref[0]
    nbuf = MOE_ROW_BUFFERS
    ahead = nbuf - 1
    xslot = lax.rem(i, nbuf)

    def row_copy(block, r):
        s = lax.rem(block, nbuf)
        tok = src_ref[block * rb + r]
        return pltpu.make_async_copy(h_hbm.at[pl.ds(tok, 1)], xbuf.at[s, pl.ds(r, 1)], gsem.at[s])

    def wait_rows(s):
        pltpu.make_async_copy(h_hbm.at[pl.ds(0, rb)], xbuf.at[s], gsem.at[s]).wait()

    @pl.when(i == 0)
    def _():
        for b in range(ahead):
            @pl.when(b < n_used)
            def _(b=b):
                def issue(r, carry):
                    row_copy(b, r).start(priority=1)
                    return carry
                lax.fori_loop(0, rb, issue, 0, unroll=8)

    def weight_copies(expert, s):
        return (pltpu.make_async_copy(wg_hbm.at[expert], wg_buf.at[s], sem.at[s]),
                pltpu.make_async_copy(wu_hbm.at[expert], wu_buf.at[s], sem.at[s]),
                pltpu.make_async_copy(wd_hbm.at[expert], wd_buf.at[s], sem.at[s]))

    @pl.when(i == 0)
    def _():
        for cp in weight_copies(e, slot):
            cp.start()

    @pl.when(first_ref[i] == 1)
    def _():
        for cp in weight_copies(e, slot):
            cp.wait()

        @pl.when(next_e_ref[i] >= 0)
        def _():
            for cp in weight_copies(next_e_ref[i], 1 - slot):
                cp.start()

        wg16[...] = wg_buf[slot].astype(BF16)
        wu16[...] = wu_buf[slot].astype(BF16)
        wd16[...] = wd_buf[slot].astype(BF16)

    def compute(xb):
        g = _dot(xb, wg16[...])
        u = _dot(xb, wu16[...])
        hm = (g * _sigmoid(g) * u).astype(BF16)
        y_ref[...] = _dot(hm, wd16[...])

    @pl.when(i < n_used - ahead)
    def _():
        wait_rows(xslot)
        xb = xbuf[xslot].astype(BF16)
        for r in range(rb):
            row_copy(i + ahead, r).start(priority=1)
        compute(xb)

    @pl.when(jnp.logical_and(i >= n_used - ahead, i < n_used))
    def _():
        wait_rows(xslot)
        compute(xbuf[xslot].astype(BF16))

    @pl.when(i >= n_used)
    def _():
        y_ref[...] = jnp.zeros_like(y_ref)


def moe_experts(h2, src_rows, blk_expert, n_used, first, slot, next_e, w_gate, w_up, w_down):
    rb = MOE_ROWS
    n_rows = src_rows.shape[0]
    d = h2.shape[1]
    de = w_gate.shape[-1]
    grid_spec = pltpu.PrefetchScalarGridSpec(
        num_scalar_prefetch=6,
        grid=(n_rows // rb,),
        in_specs=[pl.BlockSpec(memory_space=pl.ANY),
                  pl.BlockSpec(memory_space=pl.ANY),
                  pl.BlockSpec(memory_space=pl.ANY),
                  pl.BlockSpec(memory_space=pl.ANY)],
        out_specs=pl.BlockSpec((rb, d), lambda i, *_: (i, 0)),
        scratch_shapes=[pltpu.VMEM((MOE_ROW_BUFFERS, rb, d), F32),
                        pltpu.VMEM((2, d, de), F32), pltpu.VMEM((2, d, de), F32), pltpu.VMEM((2, de, d), F32),
                        pltpu.VMEM((d, de), BF16), pltpu.VMEM((d, de), BF16), pltpu.VMEM((de, d), BF16),
                        pltpu.SemaphoreType.DMA((2,)), pltpu.SemaphoreType.DMA((MOE_ROW_BUFFERS,))])
    return pl.pallas_call(
        _moe_body,
        grid_spec=grid_spec,
        out_shape=jax.ShapeDtypeStruct((n_rows, d), F32),
        compiler_params=_cparams(("arbitrary",)),
        name="moe_experts",
    )(blk_expert, n_used, first, slot, next_e, src_rows, h2, w_gate, w_up, w_down)


def _combine_body(pos0_ref, pos1_ref, y_hbm, x2_ref, gt_ref, nw_ref, o_ref, buf, sem):
    tf = COMBINE_ROWS
    nbuf = COMBINE_BUFFERS
    ahead = nbuf - 1
    i = pl.program_id(0)
    n_steps = pl.num_programs(0)
    slot = lax.rem(i, nbuf)

    def start_row(step, r):
        sl = lax.rem(step, nbuf)
        t = step * tf + r
        pltpu.make_async_copy(y_hbm.at[pl.ds(pos0_ref[t], 1)], buf.at[sl, 0, pl.ds(r, 1)],
                              sem.at[sl]).start()
        pltpu.make_async_copy(y_hbm.at[pl.ds(pos1_ref[t], 1)], buf.at[sl, 1, pl.ds(r, 1)],
                              sem.at[sl]).start(priority=1)

    @pl.when(i == 0)
    def _():
        for b in range(ahead):
            def body(r, carry, b=b):
                start_row(b, r)
                return carry
            lax.fori_loop(0, tf, body, 0, unroll=8)

    def wait_rows():
        for kk in range(TOPK_IN_GROUP):
            pltpu.make_async_copy(y_hbm.at[pl.ds(0, tf)], buf.at[slot, kk], sem.at[slot]).wait()

    def finish(y0, y1):
        gt = gt_ref[...]
        xo = x2_ref[...] + gt[:, 0:1] * y0 + gt[:, 1:2] * y1
        ms = jnp.mean(xo * xo, axis=-1, keepdims=True)
        o_ref[...] = xo * lax.rsqrt(ms + RMS_EPS) * nw_ref[...]

    @pl.when(i + ahead < n_steps)
    def _():
        wait_rows()
        y0 = buf[slot, 0]
        y1 = buf[slot, 1]
        for r in range(tf):
            start_row(i + ahead, r)
        finish(y0, y1)

    @pl.when(i + ahead >= n_steps)
    def _():
        wait_rows()
        finish(buf[slot, 0], buf[slot, 1])


def moe_combine(yb, pos0, pos1, x2, gates, nw):
    n, d = x2.shape
    tf = COMBINE_ROWS
    grid_spec = pltpu.PrefetchScalarGridSpec(
        num_scalar_prefetch=2,
        grid=(n // tf,),
        in_specs=[pl.BlockSpec(memory_space=pl.ANY),
                  pl.BlockSpec((tf, d), lambda i, *_: (i, 0)),
                  pl.BlockSpec((tf, LANES), lambda i, *_: (i, 0)),
                  pl.BlockSpec((1, d), lambda i, *_: (0, 0))],
        out_specs=pl.BlockSpec((tf, d), lambda i, *_: (i, 0)),
        scratch_shapes=[pltpu.VMEM((COMBINE_BUFFERS, 2, tf, d), F32), pltpu.SemaphoreType.DMA((COMBINE_BUFFERS,))])
    return pl.pallas_call(
        _combine_body,
        grid_spec=grid_spec,
        out_shape=jax.ShapeDtypeStruct((n, d), F32),
        compiler_params=_cparams(("arbitrary",)),
        name="moe_combine",
    )(pos0, pos1, yb, x2, gates, nw.reshape(1, d))


def _dispatch_plan(expert_id):
    n_tok, k = expert_id.shape
    rb = MOE_ROWS
    n_assign = n_tok * k
    e_flat = expert_id.reshape(-1)
    onehot = (e_flat[:, None] == jnp.arange(N_EXPERTS, dtype=I32)[None, :]).astype(I32)
    csum = jnp.cumsum(onehot, axis=0)
    counts = csum[-1]
    padded = (counts + rb - 1) // rb * rb
    pad_end = jnp.cumsum(padded)
    pad_start = pad_end - padded
    pos = jnp.sum(onehot * (pad_start[None, :] + csum - 1), axis=1).astype(I32).reshape(n_tok, k)
    n_rb = -(-n_assign // rb) + N_EXPERTS
    blk_start = jnp.arange(n_rb, dtype=I32) * rb
    blk_expert = jnp.minimum(jnp.sum((pad_end[None, :] <= blk_start[:, None]).astype(I32), axis=1),
                             N_EXPERTS - 1).astype(I32)
    n_used = (pad_end[-1] // rb).astype(I32).reshape(1)
    blk = jnp.arange(n_rb, dtype=I32)
    prev_expert = jnp.concatenate([jnp.full((1,), -1, I32), blk_expert[:-1]])
    first = jnp.logical_and(blk < n_used[0], blk_expert != prev_expert).astype(I32)
    slot = jnp.bitwise_and(jnp.cumsum(first) - 1, 1).astype(I32)
    ids = jnp.arange(N_EXPERTS, dtype=I32)
    later = jnp.logical_and(ids[None, :] > ids[:, None], counts[None, :] > 0)
    next_nonempty = jnp.min(jnp.where(later, ids[None, :], N_EXPERTS), axis=1)
    next_nonempty = jnp.where(next_nonempty < N_EXPERTS, next_nonempty, -1).astype(I32)
    next_e = next_nonempty[blk_expert]
    return pos, blk_expert, n_used, first, slot, next_e, n_rb * rb


def kernel(x, norm_mix_w, w_in, gdn_conv_w, gdn_A_log, gdn_dt_bias, gdn_out_norm_w, moba_out_norm_w, w_out, norm_ffn_w, w_router_group, b_router_group, w_router_expert, b_router_expert, w_expert_gate, w_expert_up, w_expert_down, norm_final_w):
    batch, seq, d = x.shape
    n = batch * seq
    assert w_in.shape[0] == 1, "the final norm is fused into the last layer's combine; one layer supported"
    l = 0
    xf = x.reshape(n, d).astype(F32)
    gw = 4 * GDN_WIDTH
    w_l = w_in[l]
    mq0 = gw + 2 * GDN_HEADS
    col_scale = jnp.where(jnp.arange(3 * MOBA_WIDTH) < MOBA_WIDTH, HEAD_DIM ** -0.5 * LOG2E, 1.0).astype(F32)
    w_t = jnp.swapaxes(w_l, 0, 1)
    w_gdn = w_t[:gw].astype(BF16)
    w_moba = (w_t[mq0:] * col_scale[:, None]).astype(BF16)
    w_ba = jnp.pad(w_t[gw:gw + 2 * GDN_HEADS], ((0, LANES - 2 * GDN_HEADS), (0, 0))).astype(BF16)
    proj_g, ba = norm_matmul(xf, norm_mix_w[l], w_gdn, BF16, IN_PROJ_ROWS, gw // 2, w_small=w_ba)
    proj_m = norm_matmul(xf, norm_mix_w[l], w_moba, BF16, IN_PROJ_ROWS, 3 * MOBA_WIDTH // 2)
    og = gdn_heads(proj_g, ba, gdn_conv_w[l].astype(F32), gdn_A_log[l], gdn_dt_bias[l],
                   gdn_out_norm_w[l].astype(F32), batch, seq)
    om = moba_heads(proj_m, moba_out_norm_w[l].astype(F32), batch, seq, 0)

    w_o = w_out[l].astype(BF16)
    n_r = N_GROUPS + N_EXPERTS
    w_router = jnp.pad(jnp.concatenate([w_router_group[l], w_router_expert[l]], axis=1).astype(F32),
                       ((0, 0), (0, LANES - n_r)))
    wr_hi = w_router.astype(BF16)
    wr_lo = (w_router - wr_hi.astype(F32)).astype(BF16)
    r_bias = jnp.pad(jnp.concatenate([b_router_group[l], b_router_expert[l]]).astype(F32),
                     (0, LANES - n_r)).reshape(1, LANES)
    x2, hp, gates, ids = out_proj(xf, og, om, w_o[:GDN_WIDTH], w_o[GDN_WIDTH:], norm_ffn_w[l].astype(F32),
                                  wr_hi, wr_lo, r_bias, 512)

    pos, blk_expert, n_used, first, slot, next_e, n_rows = _dispatch_plan(ids[:, :TOPK_IN_GROUP])
    pos0, pos1 = pos[:, 0], pos[:, 1]
    src_rows = moe_source_rows(pos0, pos1, n_rows)
    yb = moe_experts(hp, src_rows, blk_expert, n_used, first, slot, next_e, w_expert_gate[l].astype(F32),
                     w_expert_up[l].astype(F32), w_expert_down[l].astype(F32))
    out = moe_combine(yb, pos0, pos1, x2, gates, norm_final_w.astype(F32))
    return out.reshape(batch, seq, d).astype(x.dtype)
```

```python
import functools
import itertools

import jax
import jax.numpy as jnp
from jax import lax
from jax.experimental import pallas as pl
from jax.experimental.pallas import tpu as pltpu

F32 = jnp.float32
BF16 = jnp.bfloat16
U32 = jnp.uint32
I32 = jnp.int32

HEAD_DIM = 128
GDN_HEADS = 8
MOBA_HEADS = 8
GDN_WIDTH = GDN_HEADS * HEAD_DIM
MOBA_WIDTH = MOBA_HEADS * HEAD_DIM
CONV_WIDTH = 4
GDN_CHUNK = 64
MOBA_BLOCK = 256
MOBA_TOPK = 3
N_GROUPS = 4
EXPERTS_PER_GROUP = 8
N_EXPERTS = N_GROUPS * EXPERTS_PER_GROUP
TOPK_IN_GROUP = 2
RMS_EPS = 1e-6
NEG_INF = -1e30
MOBA_MASK = -(2.0 ** 99)
LOG2E = 1.4426950408889634
LANES = 128
VMEM_LIMIT = 56 * 1024 * 1024

IN_PROJ_ROWS = 1024
GDN_TILE = 128
GDN_HEADS_PER_STEP = 8
MOBA_HEADS_PER_STEP = 4
MOE_ROW_BUFFERS = 3
MOE_ROWS = 256
COMBINE_BUFFERS = 3
COMBINE_ROWS = 256
HI = lax.Precision.HIGHEST


def _cparams(sem, **kw):
    return pltpu.CompilerParams(dimension_semantics=sem, vmem_limit_bytes=VMEM_LIMIT, **kw)


def _dot(a, b):
    return jnp.dot(a, b, preferred_element_type=F32)


def _dot_nt(a, b, precision=None):
    return lax.dot_general(a, b, (((1,), (1,)), ((), ())), preferred_element_type=F32,
                           precision=precision)


def _sigmoid(x):
    return 1.0 / (1.0 + jnp.exp(-x))


def _norm_matmul_body(x_ref, nw_ref, w_ref, *rest):
    x = x_ref[...]
    ms = jnp.mean(x * x, axis=-1, keepdims=True)
    h = (x * lax.rsqrt(ms + RMS_EPS) * nw_ref[...]).astype(BF16)
    if len(rest) == 1:
        rest[0][...] = _dot_nt(h, w_ref[...]).astype(rest[0].dtype)
    else:
        ws_ref, o_ref, os_ref = rest
        o_ref[...] = _dot_nt(h, w_ref[...]).astype(o_ref.dtype)
        os_ref[0] = _dot_nt(h, ws_ref[...])


def norm_matmul(x, nw, w, out_dtype, tm, tn, w_small=None):
    n, d = x.shape
    width = w.shape[0]
    in_specs = [pl.BlockSpec((tm, d), lambda j, i: (i, 0)),
                pl.BlockSpec((1, d), lambda j, i: (0, 0)),
                pl.BlockSpec((tn, d), lambda j, i: (j, 0))]
    out_specs = pl.BlockSpec((tm, tn), lambda j, i: (i, j))
    out_shape = jax.ShapeDtypeStruct((n, width), out_dtype)
    operands = (x, nw.reshape(1, d), w)
    if w_small is not None:
        ws = w_small.shape[0]
        in_specs.append(pl.BlockSpec((ws, d), lambda j, i: (0, 0)))
        out_specs = [out_specs, pl.BlockSpec((1, tm, ws), lambda j, i: (j, i, 0))]
        out_shape = [out_shape, jax.ShapeDtypeStruct((width // tn, n, ws), F32)]
        operands = operands + (w_small,)
    outs = pl.pallas_call(
        _norm_matmul_body,
        grid=(width // tn, n // tm),
        in_specs=in_specs,
        out_specs=out_specs,
        out_shape=out_shape,
        compiler_params=_cparams(("arbitrary", "arbitrary")),
        name="norm_in_proj",
    )(*operands)
    if w_small is None:
        return outs
    return outs[0], outs[1][0]


def _gdn_body(alog_ref, dtb_ref, q_ref, k_ref, v_ref, z_ref, ba_ref, cwq_ref, cwk_ref, cwv_ref,
              onw_ref, o_ref, *scratch):
    tt = GDN_TILE
    c = GDN_CHUNK
    t = pl.program_id(2)

    @pl.when(t == 0)
    def _():
        for hh in range(GDN_HEADS_PER_STEP):
            s_ref, prev_ref, vn_ref, _ = scratch[4 * hh:4 * hh + 4]
            s_ref[...] = jnp.zeros_like(s_ref)
            prev_ref[...] = jnp.zeros_like(prev_ref)
            vn_ref[...] = jnp.zeros_like(vn_ref)

    row_8 = lax.broadcasted_iota(jnp.int32, (8, LANES), 0)
    lane_t = lax.broadcasted_iota(jnp.int32, (tt, LANES), 1)
    row = lax.broadcasted_iota(jnp.int32, (tt, tt), 0)
    col = lax.broadcasted_iota(jnp.int32, (tt, tt), 1)
    same = (row // c) == (col // c)
    causal = jnp.logical_and(same, col <= row)
    strict = jnp.logical_and(same, col < row)
    causal16 = jnp.where(causal, 1.0, 0.0).astype(BF16)
    same16 = jnp.where(same, 1.0, 0.0).astype(BF16)
    eye = jnp.where(row == col, 1.0, 0.0)
    colk = lax.broadcasted_iota(jnp.int32, (HEAD_DIM, tt), 1)
    ba = ba_ref[...]

    heads = [_gdn_one_head(hh, pl.program_id(1) * GDN_HEADS_PER_STEP + hh, alog_ref, dtb_ref, q_ref, k_ref,
                           v_ref, z_ref, ba, cwq_ref, cwk_ref, cwv_ref, onw_ref, o_ref,
                           *scratch[4 * hh:4 * hh + 4],
                           row_8, lane_t, causal, strict, causal16, same16, eye, colk)
             for hh in range(GDN_HEADS_PER_STEP)]
    for _ in itertools.zip_longest(*heads):
        pass


def _gdn_one_head(hh, h, alog_ref, dtb_ref, q_ref, k_ref, v_ref, z_ref, ba, cwq_ref, cwk_ref, cwv_ref,
                  onw_ref, o_ref, s_ref, prev_ref, vn_ref, oacc_ref,
                  row_8, lane_t, causal, strict, causal16, same16, eye, colk):
    tt = GDN_TILE
    c = GDN_CHUNK
    lanes = slice(hh * HEAD_DIM, (hh + 1) * HEAD_DIM)

    def conv_silu(x_ref, cw_ref, idx):
        x = x_ref[:, lanes].astype(F32)
        p = prev_ref[idx]
        w = cw_ref[:, lanes]
        acc = x * w[CONV_WIDTH - 1:CONV_WIDTH, :]
        for s in range(1, CONV_WIDTH):
            xs = pltpu.roll(x, s, axis=0)
            head = jnp.where(row_8 < s, pltpu.roll(p, s, axis=0), xs[:8])
            xs = jnp.concatenate([head, xs[8:]], axis=0)
            acc = acc + xs * w[CONV_WIDTH - 1 - s:CONV_WIDTH - s, :]
        prev_ref[idx] = x[tt - 8:tt, :]
        return acc * _sigmoid(acc)

    q = conv_silu(q_ref, cwq_ref, 0)
    k = conv_silu(k_ref, cwk_ref, 1)
    v = conv_silu(v_ref, cwv_ref, 2)
    q = q * lax.rsqrt(jnp.sum(q * q, axis=-1, keepdims=True) + 1e-6) * (HEAD_DIM ** -0.5)
    k = k * lax.rsqrt(jnp.sum(k * k, axis=-1, keepdims=True) + 1e-6)
    yield

    b_col = jnp.sum(jnp.where(lane_t == h, ba, 0.0), axis=-1, keepdims=True)
    a_col = jnp.sum(jnp.where(lane_t == h + GDN_HEADS, ba, 0.0), axis=-1, keepdims=True)
    beta = _sigmoid(b_col)
    xa = a_col + dtb_ref[h]
    softplus = jnp.maximum(xa, 0.0) + jnp.log(1.0 + jnp.exp(-jnp.abs(xa)))
    g = -jnp.exp(jnp.full((1, 1), alog_ref[h], F32)) * softplus

    g_hi = g.astype(BF16).astype(F32)
    g_mid = (g - g_hi).astype(BF16).astype(F32)
    g_lo = g - g_hi - g_mid
    g3 = jnp.where(lane_t == 0, g_hi, jnp.where(lane_t == 1, g_mid, jnp.where(lane_t == 2, g_lo, 0.0)))
    g3 = g3.astype(BF16)

    def three(r):
        return r[:, 0:1] + r[:, 1:2] + r[:, 2:3]

    gc_col = three(_dot(causal16, g3))
    gc_b = jnp.broadcast_to(gc_col, (tt, LANES))
    glast_b = jnp.broadcast_to(three(_dot(same16, g3)), (tt, LANES))
    gc_row = jnp.transpose(gc_b)[0:1, :]
    decay = jnp.exp(jnp.where(causal, gc_col - gc_row, NEG_INF))
    yield

    kb = k * beta
    k16 = k.astype(BF16)
    kk = _dot_nt(kb.astype(BF16), k16)
    lmat = jnp.where(strict, kk * decay, 0.0)
    yield
    attn = _dot_nt(q.astype(BF16), k16) * decay

    tinv = eye - lmat
    m16 = lmat.astype(BF16)
    for _ in range(5):
        m16 = _dot(m16, m16).astype(BF16)
        yield
        tinv = tinv + _dot(tinv.astype(BF16), m16)
        yield

    egc = jnp.exp(gc_b)
    rhs = jnp.concatenate([v * beta, kb * egc], axis=-1)
    sol = _dot(tinv.astype(BF16), rhs.astype(BF16))
    yield
    u = sol[:, :HEAD_DIM]
    w16 = sol[:, HEAD_DIM:].astype(BF16)
    qd16 = (q * egc).astype(BF16)
    kd = k * jnp.exp(glast_b - gc_b)
    kdt = jnp.transpose(kd)
    gtot_b = jnp.exp(glast_b)
    attn16 = attn.astype(BF16)

    s = s_ref[...]
    for n in range(tt // c):
        sl = slice(n * c, (n + 1) * c)
        s16 = s.astype(BF16)
        v_new = u[sl] - _dot(w16[sl], s16)
        vn_ref[sl, :] = v_new
        yield
        vn16 = vn_ref[...].astype(BF16)
        oacc_ref[sl, :] = _dot(qd16[sl], s16) + _dot(attn16[sl], vn16)
        kdt_n = jnp.where((colk // c) == n, kdt, 0.0).astype(BF16)
        s = s * gtot_b[n * c:n * c + 1, :] + _dot(kdt_n, vn16)
        yield
    s_ref[...] = s

    o = oacc_ref[...]
    o = o * lax.rsqrt(jnp.mean(o * o, axis=-1, keepdims=True) + RMS_EPS) * onw_ref[...]
    z = z_ref[:, lanes].astype(F32)
    o_ref[:, lanes] = (o * (z * _sigmoid(z))).astype(o_ref.dtype)


def gdn_heads(proj, ba, conv_w, a_log, dt_bias, out_norm_w, batch, seq):
    n = batch * seq
    tt = GDN_TILE
    nt = seq // tt
    hps = GDN_HEADS_PER_STEP
    ng = GDN_HEADS // hps
    wide = hps * HEAD_DIM

    def col_spec(section):
        return pl.BlockSpec((tt, wide), lambda b, h, t, *_: (b * nt + t, section * ng + h))

    def cw_spec(section):
        return pl.BlockSpec((CONV_WIDTH, wide), lambda b, h, t, *_: (0, section * ng + h))

    grid_spec = pltpu.PrefetchScalarGridSpec(
        num_scalar_prefetch=2,
        grid=(batch, ng, nt),
        in_specs=[col_spec(0), col_spec(1), col_spec(2), col_spec(3),
                  pl.BlockSpec((tt, LANES), lambda b, h, t, *_: (b * nt + t, 0)),
                  cw_spec(0), cw_spec(1), cw_spec(2),
                  pl.BlockSpec((1, HEAD_DIM), lambda b, h, t, *_: (0, 0))],
        out_specs=pl.BlockSpec((tt, wide), lambda b, h, t, *_: (b * nt + t, h)),
        scratch_shapes=[pltpu.VMEM((HEAD_DIM, HEAD_DIM), F32),
                        pltpu.VMEM((3, 8, HEAD_DIM), F32),
                        pltpu.VMEM((tt, HEAD_DIM), F32),
                        pltpu.VMEM((tt, HEAD_DIM), F32)] * hps)
    return pl.pallas_call(
        _gdn_body,
        grid_spec=grid_spec,
        out_shape=jax.ShapeDtypeStruct((n, GDN_WIDTH), BF16),
        compiler_params=_cparams(("arbitrary", "arbitrary", "arbitrary")),
        name="gdn_heads",
    )(a_log.astype(F32), dt_bias.astype(F32), proj, proj, proj, proj, ba,
      conv_w, conv_w, conv_w, out_norm_w.reshape(1, HEAD_DIM))


def _moba_body(slope_ref, q_ref, k_ref, v_ref, onw_ref, o_ref, *scratch, n_blocks):
    blk = MOBA_BLOCK
    tq = 2 * blk
    nb = n_blocks
    nbp = -(-nb // 8) * 8
    hps = MOBA_HEADS_PER_STEP
    cq = pl.program_id(2)
    c0 = 2 * cq
    lane = lax.broadcasted_iota(I32, (blk, LANES), 1)
    row = lax.broadcasted_iota(I32, (blk, blk), 0)
    col = lax.broadcasted_iota(I32, (blk, blk), 1)
    rid = lax.broadcasted_iota(I32, (LANES, tq), 0)
    bid = lax.broadcasted_iota(I32, (nbp, tq), 0)
    c_row = c0 + (lax.broadcasted_iota(I32, (nbp, tq), 1) >= blk).astype(I32)
    ones_ext_t = jnp.where(jnp.logical_and(rid >= nb, rid < nb + 3), 1.0, 0.0).astype(BF16)

    def head_lanes(hh):
        return slice(hh * HEAD_DIM, (hh + 1) * HEAD_DIM)

    def keys(hh, j, nblk=1):
        return scratch[3 * hh + 1][pl.ds(pl.multiple_of(j * blk, blk), nblk * blk), :]

    def values_t(hh, j, nblk=1):
        return scratch[3 * hh + 2][:, pl.ds(pl.multiple_of(j * blk, blk), nblk * blk)]

    @pl.when(cq == 0)
    def _():
        t_in = lax.broadcasted_iota(I32, (blk, LANES), 0)
        for hh in range(hps):
            kmean_ref, kx_ref, vt_ref = scratch[3 * hh:3 * hh + 3]
            slope2 = slope_ref[pl.program_id(1) * hps + hh] * LOG2E
            kmean_ref[...] = jnp.zeros_like(kmean_ref)
            for j in range(nb):
                kj = k_ref[j * blk:(j + 1) * blk, head_lanes(hh)]
                kmean_ref[j:j + 1, :] = jnp.mean(kj.astype(F32), axis=0, keepdims=True)
                bias = slope2 * (t_in + j * blk).astype(F32)
                b_hi = bias.astype(BF16)
                r1 = bias - b_hi.astype(F32)
                b_mid = r1.astype(BF16)
                b_lo = (r1 - b_mid.astype(F32)).astype(BF16)
                ext = jnp.where(lane == j, 1.0, 0.0).astype(BF16)
                ext = jnp.where(lane == nb, b_hi, ext)
                ext = jnp.where(lane == nb + 1, b_mid, ext)
                ext = jnp.where(lane == nb + 2, b_lo, ext)
                kx_ref[j * blk:(j + 1) * blk, :] = jnp.concatenate([kj, ext], axis=1)
                vj = v_ref[j * blk:(j + 1) * blk, head_lanes(hh)].astype(F32)
                vt_ref[:, j * blk:(j + 1) * blk] = jnp.transpose(vj).astype(BF16)

    def start(s, vals_t):
        m = jnp.max(s, axis=0, keepdims=True)
        p = jnp.exp2(s - m)
        return m, jnp.sum(p, axis=0, keepdims=True), _dot(vals_t, p.astype(BF16))

    def update(carry, s, vals_t):
        m_i, l_i, acc = carry
        m_new = jnp.maximum(m_i, jnp.max(s, axis=0, keepdims=True))
        alpha = jnp.exp2(m_i - m_new)
        p = jnp.exp2(s - m_new)
        return (m_new, l_i * alpha + jnp.sum(p, axis=0, keepdims=True),
                acc * alpha + _dot(vals_t, p.astype(BF16)))

    qx = [None] * hps
    carry0 = [None] * hps

    def prologue(hh):
        q_t = jnp.transpose(q_ref[:, head_lanes(hh)].astype(F32))
        q_t16 = q_t.astype(BF16)
        yield
        qx0_t = jnp.concatenate([q_t16, ones_ext_t], axis=0)
        state = []
        for half in range(2):
            s_own = _dot(keys(hh, c0 + half), qx0_t[:, half * blk:(half + 1) * blk])
            s_own = jnp.where(row <= col, s_own, MOBA_MASK)
            yield
            state.append(start(s_own, values_t(hh, c0 + half)))
            yield
        gate_t = jnp.dot(scratch[3 * hh][...], q_t, preferred_element_type=F32, precision=HI)[:nbp]
        gm = jnp.where(bid < c_row, gate_t, NEG_INF)
        sel = jnp.zeros((nbp, tq), F32)
        for s in range(MOBA_TOPK):
            mx = jnp.max(gm, axis=0, keepdims=True)
            idx = jnp.min(jnp.where(gm == mx, bid, LANES), axis=0, keepdims=True)
            pick = bid == idx
            sel = jnp.where(jnp.logical_and(pick, s < c_row), 1.0, sel)
            gm = jnp.where(pick, -3e38, gm)
        mask_t = jnp.where(sel > 0.5, 0.0, MOBA_MASK)
        ext_t = jnp.concatenate([mask_t, jnp.zeros((LANES - nbp, tq), F32)], axis=0)
        ext_t = jnp.where(rid < nb, ext_t, jnp.where(rid < nb + 3, 1.0, 0.0))
        qx[hh] = jnp.concatenate([q_t16, ext_t.astype(BF16)], axis=0)
        yield
        s_c0 = _dot(keys(hh, c0), qx[hh][:, blk:])
        yield
        second = update(state[1], s_c0, values_t(hh, c0))
        carry0[hh] = tuple(jnp.concatenate([a, b], axis=1) for a, b in zip(state[0], second))

    for _ in itertools.zip_longest(*[prologue(hh) for hh in range(hps)]):
        pass

    def body(p, carries):
        scores = [_dot(keys(hh, 2 * p, 2), qx[hh]) for hh in range(hps)]
        return tuple(update(carries[hh], scores[hh], values_t(hh, 2 * p, 2)) for hh in range(hps))

    final = lax.fori_loop(0, cq, body, tuple(carry0))
    for hh in range(hps):
        _, l_f, acc_f = final[hh]
        o_t = acc_f / l_f
        o_t = o_t * lax.rsqrt(jnp.mean(o_t * o_t, axis=0, keepdims=True) + RMS_EPS)
        o_ref[:, head_lanes(hh)] = (jnp.transpose(o_t) * onw_ref[...]).astype(o_ref.dtype)


def moba_heads(proj, out_norm_w, batch, seq, col_off):
    n = batch * seq
    blk = MOBA_BLOCK
    nb = seq // blk
    nq = nb // 2
    hps = MOBA_HEADS_PER_STEP
    ng = MOBA_HEADS // hps
    wide = hps * HEAD_DIM
    sec0 = col_off // hps
    slopes = jnp.exp2(-8.0 * jnp.arange(1, MOBA_HEADS + 1, dtype=F32) / MOBA_HEADS)
    grid_spec = pltpu.PrefetchScalarGridSpec(
        num_scalar_prefetch=1,
        grid=(batch, ng, nq),
        in_specs=[pl.BlockSpec((2 * blk, wide), lambda b, h, c, *_: (b * nq + c, sec0 + h)),
                  pl.BlockSpec((seq, wide), lambda b, h, c, *_: (b, sec0 + ng + h)),
                  pl.BlockSpec((seq, wide), lambda b, h, c, *_: (b, sec0 + 2 * ng + h)),
                  pl.BlockSpec((1, HEAD_DIM), lambda b, h, c, *_: (0, 0))],
        out_specs=pl.BlockSpec((2 * blk, wide), lambda b, h, c, *_: (b * nq + c, h)),
        scratch_shapes=[pltpu.VMEM((LANES, HEAD_DIM), F32),
                        pltpu.VMEM((nb * blk, HEAD_DIM + LANES), BF16),
                        pltpu.VMEM((HEAD_DIM, nb * blk), BF16)] * hps)
    assert nb % 2 == 0 and nb + 3 <= LANES and col_off % hps == 0
    return pl.pallas_call(
        functools.partial(_moba_body, n_blocks=nb),
        grid_spec=grid_spec,
        out_shape=jax.ShapeDtypeStruct((n, MOBA_WIDTH), BF16),
        compiler_params=_cparams(("arbitrary", "arbitrary", "arbitrary")),
        name="moba_heads",
    )(slopes, proj, proj, proj, out_norm_w.reshape(1, HEAD_DIM))


def _out_proj_body(x_ref, og_ref, om_ref, wg_ref, wm_ref, nw_ref, wrh_ref, wrl_ref, rb_ref,
                   x2_ref, hp_ref, gt_ref, id_ref):
    tm = x_ref.shape[0]
    x2 = x_ref[...] + _dot(og_ref[...], wg_ref[...]) + _dot(om_ref[...], wm_ref[...])
    x2_ref[...] = x2
    ms = jnp.mean(x2 * x2, axis=-1, keepdims=True)
    h2 = x2 * lax.rsqrt(ms + RMS_EPS) * nw_ref[...]
    h_hi = h2.astype(BF16)
    hp_ref[...] = h2
    h_lo = (h2 - h_hi.astype(F32)).astype(BF16)
    lg = _dot(h_hi, wrh_ref[...]) + _dot(h_hi, wrl_ref[...]) + _dot(h_lo, wrh_ref[...]) + rb_ref[...]

    lane = lax.broadcasted_iota(I32, (tm, LANES), 1)
    is_g = lane < N_GROUPS
    mg = jnp.max(jnp.where(is_g, lg, NEG_INF), axis=-1, keepdims=True)
    g_idx = jnp.min(jnp.where(jnp.logical_and(is_g, lg == mg), lane, LANES), axis=-1, keepdims=True)
    sum_g = jnp.sum(jnp.where(is_g, jnp.exp(lg - mg), 0.0), axis=-1, keepdims=True)
    p_top_g = 1.0 / sum_g
    lo = N_GROUPS + g_idx * EXPERTS_PER_GROUP
    in_grp = jnp.logical_and(lane >= lo, lane < lo + EXPERTS_PER_GROUP)
    m1 = jnp.max(jnp.where(in_grp, lg, NEG_INF), axis=-1, keepdims=True)
    i1 = jnp.min(jnp.where(jnp.logical_and(in_grp, lg == m1), lane, LANES), axis=-1, keepdims=True)
    rest = jnp.logical_and(in_grp, lane != i1)
    m2 = jnp.max(jnp.where(rest, lg, NEG_INF), axis=-1, keepdims=True)
    i2 = jnp.min(jnp.where(jnp.logical_and(rest, lg == m2), lane, LANES), axis=-1, keepdims=True)
    e2 = jnp.exp(m2 - m1)
    gate1 = p_top_g / (1.0 + e2)
    gate2 = p_top_g * e2 / (1.0 + e2)
    gt_ref[...] = jnp.where(lane == 0, gate1, jnp.where(lane == 1, gate2, 0.0))
    id_ref[...] = jnp.where(lane == 0, i1 - N_GROUPS, jnp.where(lane == 1, i2 - N_GROUPS, 0))


def out_proj(x, og, om, w_g, w_m, nw, wr_hi, wr_lo, r_bias, tm):
    n, d = x.shape
    const = lambda i: (0, 0)
    rows = lambda i: (i, 0)
    return pl.pallas_call(
        _out_proj_body,
        grid=(n // tm,),
        in_specs=[pl.BlockSpec((tm, d), rows),
                  pl.BlockSpec((tm, GDN_WIDTH), rows),
                  pl.BlockSpec((tm, MOBA_WIDTH), rows),
                  pl.BlockSpec((GDN_WIDTH, d), const),
                  pl.BlockSpec((MOBA_WIDTH, d), const),
                  pl.BlockSpec((1, d), const),
                  pl.BlockSpec((d, LANES), const),
                  pl.BlockSpec((d, LANES), const),
                  pl.BlockSpec((1, LANES), const)],
        out_specs=[pl.BlockSpec((tm, d), rows),
                   pl.BlockSpec((tm, d), rows),
                   pl.BlockSpec((tm, LANES), rows),
                   pl.BlockSpec((tm, LANES), rows)],
        out_shape=[jax.ShapeDtypeStruct((n, d), F32),
                   jax.ShapeDtypeStruct((n, d), F32),
                   jax.ShapeDtypeStruct((n, LANES), F32),
                   jax.ShapeDtypeStruct((n, LANES), I32)],
        compiler_params=_cparams(("arbitrary",)),
        name="out_proj_router",
    )(x, og, om, w_g, w_m, nw.reshape(1, d), wr_hi, wr_lo, r_bias)


def _invert_body(pos0_ref, pos1_ref, src_ref, zeros_vmem, sem):
    n_tok = pos0_ref.shape[0]
    zeros_vmem[...] = jnp.zeros_like(zeros_vmem)
    clear = pltpu.make_async_copy(zeros_vmem, src_ref, sem)
    clear.start()
    clear.wait()

    def fill(t, carry):
        src_ref[pos0_ref[t]] = t
        src_ref[pos1_ref[t]] = t
        return carry

    lax.fori_loop(0, n_tok, fill, 0, unroll=8)


def moe_source_rows(pos0, pos1, n_rows):
    grid_spec = pltpu.PrefetchScalarGridSpec(
        num_scalar_prefetch=2,
        grid=(1,),
        in_specs=[],
        out_specs=pl.BlockSpec(memory_space=pltpu.SMEM),
        scratch_shapes=[pltpu.VMEM((n_rows,), I32), pltpu.SemaphoreType.DMA(())])
    return pl.pallas_call(
        _invert_body,
        grid_spec=grid_spec,
        out_shape=jax.ShapeDtypeStruct((n_rows,), I32),
        compiler_params=_cparams(("arbitrary",)),
        name="moe_source_rows",
    )(pos0, pos1)


def _moe_body(blk_e_ref, nblk_ref, first_ref, slot_ref, next_e_ref, src_ref, h_hbm, wg_hbm, wu_hbm, wd_hbm,
              y_ref, xbuf, wg_buf, wu_buf, wd_buf, wg16, wu16, wd16, sem, gsem):
    rb = MOE_ROWS
    i = pl.program_id(0)
    e = blk_e_ref[i]
    slot = slot_ref[i]
    n_used = nblk_ref[0]
    nbuf = MOE_ROW_BUFFERS
    ahead = nbuf - 1
    xslot = lax.rem(i, nbuf)

    def row_copy(block, r):
        s = lax.rem(block, nbuf)
        tok = src_ref[block * rb + r]
        return pltpu.make_async_copy(h_hbm.at[pl.ds(tok, 1)], xbuf.at[s, pl.ds(r, 1)], gsem.at[s])

    def wait_rows(s):
        pltpu.make_async_copy(h_hbm.at[pl.ds(0, rb)], xbuf.at[s], gsem.at[s]).wait()

    @pl.when(i == 0)
    def _():
        for b in range(ahead):
            @pl.when(b < n_used)
            def _(b=b):
                def issue(r, carry):
                    row_copy(b, r).start()
                    return carry
                lax.fori_loop(0, rb, issue, 0, unroll=8)

    def weight_copies(expert, s):
        return (pltpu.make_async_copy(wg_hbm.at[expert], wg_buf.at[s], sem.at[s]),
                pltpu.make_async_copy(wu_hbm.at[expert], wu_buf.at[s], sem.at[s]),
                pltpu.make_async_copy(wd_hbm.at[expert], wd_buf.at[s], sem.at[s]))

    @pl.when(i == 0)
    def _():
        for cp in weight_copies(e, slot):
            cp.start(priority=1)

    @pl.when(first_ref[i] == 1)
    def _():
        for cp in weight_copies(e, slot):
            cp.wait()

        @pl.when(next_e_ref[i] >= 0)
        def _():
            for cp in weight_copies(next_e_ref[i], 1 - slot):
                cp.start(priority=1)

        wg16[...] = wg_buf[slot].astype(BF16)
        wu16[...] = wu_buf[slot].astype(BF16)
        wd16[...] = wd_buf[slot].astype(BF16)

    def compute(xb):
        g = _dot(xb, wg16[...])
        u = _dot(xb, wu16[...])
        hm = (g * _sigmoid(g) * u).astype(BF16)
        y_ref[...] = _dot(hm, wd16[...])

    @pl.when(i < n_used - ahead)
    def _():
        wait_rows(xslot)
        xb = xbuf[xslot].astype(BF16)
        for r in range(rb):
            row_copy(i + ahead, r).start()
        compute(xb)

    @pl.when(jnp.logical_and(i >= n_used - ahead, i < n_used))
    def _():
        wait_rows(xslot)
        compute(xbuf[xslot].astype(BF16))

    @pl.when(i >= n_used)
    def _():
        y_ref[...] = jnp.zeros_like(y_ref)


def moe_experts(h2, src_rows, blk_expert, n_used, first, slot, next_e, w_gate, w_up, w_down):
    rb = MOE_ROWS
    n_rows = src_rows.shape[0]
    d = h2.shape[1]
    de = w_gate.shape[-1]
    grid_spec = pltpu.PrefetchScalarGridSpec(
        num_scalar_prefetch=6,
        grid=(n_rows // rb,),
        in_specs=[pl.BlockSpec(memory_space=pl.ANY),
                  pl.BlockSpec(memory_space=pl.ANY),
                  pl.BlockSpec(memory_space=pl.ANY),
                  pl.BlockSpec(memory_space=pl.ANY)],
        out_specs=pl.BlockSpec((rb, d), lambda i, *_: (i, 0)),
        scratch_shapes=[pltpu.VMEM((MOE_ROW_BUFFERS, rb, d), F32),
                        pltpu.VMEM((2, d, de), F32), pltpu.VMEM((2, d, de), F32), pltpu.VMEM((2, de, d), F32),
                        pltpu.VMEM((d, de), BF16), pltpu.VMEM((d, de), BF16), pltpu.VMEM((de, d), BF16),
                        pltpu.SemaphoreType.DMA((2,)), pltpu.SemaphoreType.DMA((MOE_ROW_BUFFERS,))])
    return pl.pallas_call(
        _moe_body,
        grid_spec=grid_spec,
        out_shape=jax.ShapeDtypeStruct((n_rows, d), F32),
        compiler_params=_cparams(("arbitrary",)),
        name="moe_experts",
    )(blk_expert, n_used, first, slot, next_e, src_rows, h2, w_gate, w_up, w_down)


def _combine_body(pos0_ref, pos1_ref, y_hbm, x2_ref, gt_ref, nw_ref, o_ref, buf, sem):
    tf = COMBINE_ROWS
    nbuf = COMBINE_BUFFERS
    ahead = nbuf - 1
    i = pl.program_id(0)
    n_steps = pl.num_programs(0)
    slot = lax.rem(i, nbuf)

    def start_row(step, r):
        sl = lax.rem(step, nbuf)
        t = step * tf + r
        pltpu.make_async_copy(y_hbm.at[pl.ds(pos0_ref[t], 1)], buf.at[sl, 0, pl.ds(r, 1)],
                              sem.at[sl]).start()
        pltpu.make_async_copy(y_hbm.at[pl.ds(pos1_ref[t], 1)], buf.at[sl, 1, pl.ds(r, 1)],
                              sem.at[sl]).start(priority=1)

    @pl.when(i == 0)
    def _():
        for b in range(ahead):
            def body(r, carry, b=b):
                start_row(b, r)
                return carry
            lax.fori_loop(0, tf, body, 0, unroll=8)

    def wait_rows():
        for kk in range(TOPK_IN_GROUP):
            pltpu.make_async_copy(y_hbm.at[pl.ds(0, tf)], buf.at[slot, kk], sem.at[slot]).wait()

    def finish(y0, y1):
        gt = gt_ref[...]
        xo = x2_ref[...] + gt[:, 0:1] * y0 + gt[:, 1:2] * y1
        ms = jnp.mean(xo * xo, axis=-1, keepdims=True)
        o_ref[...] = xo * lax.rsqrt(ms + RMS_EPS) * nw_ref[...]

    @pl.when(i + ahead < n_steps)
    def _():
        wait_rows()
        y0 = buf[slot, 0]
        y1 = buf[slot, 1]
        for r in range(tf):
            start_row(i + ahead, r)
        finish(y0, y1)

    @pl.when(i + ahead >= n_steps)
    def _():
        wait_rows()
        finish(buf[slot, 0], buf[slot, 1])


def moe_combine(yb, pos0, pos1, x2, gates, nw):
    n, d = x2.shape
    tf = COMBINE_ROWS
    grid_spec = pltpu.PrefetchScalarGridSpec(
        num_scalar_prefetch=2,
        grid=(n // tf,),
        in_specs=[pl.BlockSpec(memory_space=pl.ANY),
                  pl.BlockSpec((tf, d), lambda i, *_: (i, 0)),
                  pl.BlockSpec((tf, LANES), lambda i, *_: (i, 0)),
                  pl.BlockSpec((1, d), lambda i, *_: (0, 0))],
        out_specs=pl.BlockSpec((tf, d), lambda i, *_: (i, 0)),
        scratch_shapes=[pltpu.VMEM((COMBINE_BUFFERS, 2, tf, d), F32), pltpu.SemaphoreType.DMA((COMBINE_BUFFERS,))])
    return pl.pallas_call(
        _combine_body,
        grid_spec=grid_spec,
        out_shape=jax.ShapeDtypeStruct((n, d), F32),
        compiler_params=_cparams(("arbitrary",)),
        name="moe_combine",
    )(pos0, pos1, yb, x2, gates, nw.reshape(1, d))


def _dispatch_plan(expert_id):
    n_tok, k = expert_id.shape
    rb = MOE_ROWS
    n_assign = n_tok * k
    e_flat = expert_id.reshape(-1)
    onehot = (e_flat[:, None] == jnp.arange(N_EXPERTS, dtype=I32)[None, :]).astype(I32)
    csum = jnp.cumsum(onehot, axis=0)
    counts = csum[-1]
    padded = (counts + rb - 1) // rb * rb
    pad_end = jnp.cumsum(padded)
    pad_start = pad_end - padded
    pos = jnp.sum(onehot * (pad_start[None, :] + csum - 1), axis=1).astype(I32).reshape(n_tok, k)
    n_rb = -(-n_assign // rb) + N_EXPERTS
    blk_start = jnp.arange(n_rb, dtype=I32) * rb
    blk_expert = jnp.minimum(jnp.sum((pad_end[None, :] <= blk_start[:, None]).astype(I32), axis=1),
                             N_EXPERTS - 1).astype(I32)
    n_used = (pad_end[-1] // rb).astype(I32).reshape(1)
    blk = jnp.arange(n_rb, dtype=I32)
    prev_expert = jnp.concatenate([jnp.full((1,), -1, I32), blk_expert[:-1]])
    first = jnp.logical_and(blk < n_used[0], blk_expert != prev_expert).astype(I32)
    slot = jnp.bitwise_and(jnp.cumsum(first) - 1, 1).astype(I32)
    ids = jnp.arange(N_EXPERTS, dtype=I32)
    later = jnp.logical_and(ids[None, :] > ids[:, None], counts[None, :] > 0)
    next_nonempty = jnp.min(jnp.where(later, ids[None, :], N_EXPERTS), axis=1)
    next_nonempty = jnp.where(next_nonempty < N_EXPERTS, next_nonempty, -1).astype(I32)
    next_e = next_nonempty[blk_expert]
    return pos, blk_expert, n_used, first, slot, next_e, n_rb * rb


def kernel(x, norm_mix_w, w_in, gdn_conv_w, gdn_A_log, gdn_dt_bias, gdn_out_norm_w, moba_out_norm_w, w_out, norm_ffn_w, w_router_group, b_router_group, w_router_expert, b_router_expert, w_expert_gate, w_expert_up, w_expert_down, norm_final_w):
    batch, seq, d = x.shape
    n = batch * seq
    assert w_in.shape[0] == 1, "the final norm is fused into the last layer's combine; one layer supported"
    l = 0
    xf = x.reshape(n, d).astype(F32)
    gw = 4 * GDN_WIDTH
    w_l = w_in[l]
    mq0 = gw + 2 * GDN_HEADS
    col_scale = jnp.where(jnp.arange(3 * MOBA_WIDTH) < MOBA_WIDTH, HEAD_DIM ** -0.5 * LOG2E, 1.0).astype(F32)
    w_t = jnp.swapaxes(w_l, 0, 1)
    w_gdn = w_t[:gw].astype(BF16)
    w_moba = (w_t[mq0:] * col_scale[:, None]).astype(BF16)
    w_ba = jnp.pad(w_t[gw:gw + 2 * GDN_HEADS], ((0, LANES - 2 * GDN_HEADS), (0, 0))).astype(BF16)
    proj_g, ba = norm_matmul(xf, norm_mix_w[l], w_gdn, BF16, IN_PROJ_ROWS, gw // 2, w_small=w_ba)
    proj_m = norm_matmul(xf, norm_mix_w[l], w_moba, BF16, IN_PROJ_ROWS, 3 * MOBA_WIDTH // 2)
    og = gdn_heads(proj_g, ba, gdn_conv_w[l].astype(F32), gdn_A_log[l], gdn_dt_bias[l],
                   gdn_out_norm_w[l].astype(F32), batch, seq)
    om = moba_heads(proj_m, moba_out_norm_w[l].astype(F32), batch, seq, 0)

    w_o = w_out[l].astype(BF16)
    n_r = N_GROUPS + N_EXPERTS
    w_router = jnp.pad(jnp.concatenate([w_router_group[l], w_router_expert[l]], axis=1).astype(F32),
                       ((0, 0), (0, LANES - n_r)))
    wr_hi = w_router.astype(BF16)
    wr_lo = (w_router - wr_hi.astype(F32)).astype(BF16)
    r_bias = jnp.pad(jnp.concatenate([b_router_group[l], b_router_expert[l]]).astype(F32),
                     (0, LANES - n_r)).reshape(1, LANES)
    x2, hp, gates, ids = out_proj(xf, og, om, w_o[:GDN_WIDTH], w_o[GDN_WIDTH:], norm_ffn_w[l].astype(F32),
                                  wr_hi, wr_lo, r_bias, 512)

    pos, blk_expert, n_used, first, slot, next_e, n_rows = _dispatch_plan(ids[:, :TOPK_IN_GROUP])
    pos0, pos1 = pos[:, 0], pos[:, 1]
    src_rows = moe_source_rows(pos0, pos1, n_rows)
    yb = moe_experts(hp, src_rows, blk_expert, n_used, first, slot, next_e, w_expert_gate[l].astype(F32),
                     w_expert_up[l].astype(F32), w_expert_down[l].astype(F32))
    out = moe_combine(yb, pos0, pos1, x2, gates, norm_final_w.astype(F32))
    return out.reshape(batch, seq, d).astype(x.dtype)
```

```python
import functools
import itertools

import jax
import jax.numpy as jnp
from jax import lax
from jax.experimental import pallas as pl
from jax.experimental.pallas import tpu as pltpu

F32 = jnp.float32
BF16 = jnp.bfloat16
U32 = jnp.uint32
I32 = jnp.int32

HEAD_DIM = 128
GDN_HEADS = 8
MOBA_HEADS = 8
GDN_WIDTH = GDN_HEADS * HEAD_DIM
MOBA_WIDTH = MOBA_HEADS * HEAD_DIM
CONV_WIDTH = 4
GDN_CHUNK = 64
MOBA_BLOCK = 256
MOBA_TOPK = 3
N_GROUPS = 4
EXPERTS_PER_GROUP = 8
N_EXPERTS = N_GROUPS * EXPERTS_PER_GROUP
TOPK_IN_GROUP = 2
RMS_EPS = 1e-6
NEG_INF = -1e30
MOBA_MASK = -(2.0 ** 99)
LOG2E = 1.4426950408889634
LANES = 128
VMEM_LIMIT = 56 * 1024 * 1024

IN_PROJ_ROWS = 1024
GDN_TILE = 128
GDN_HEADS_PER_STEP = 8
MOBA_HEADS_PER_STEP = 4
MOE_ROW_BUFFERS = 4
MOE_ROWS = 256
COMBINE_BUFFERS = 3
COMBINE_ROWS = 256
HI = lax.Precision.HIGHEST


def _cparams(sem, **kw):
    return pltpu.CompilerParams(dimension_semantics=sem, vmem_limit_bytes=VMEM_LIMIT, **kw)


def _dot(a, b):
    return jnp.dot(a, b, preferred_element_type=F32)


def _dot_nt(a, b, precision=None):
    return lax.dot_general(a, b, (((1,), (1,)), ((), ())), preferred_element_type=F32,
                           precision=precision)


def _sigmoid(x):
    return 1.0 / (1.0 + jnp.exp(-x))


def _norm_matmul_body(x_ref, nw_ref, w_ref, *rest):
    x = x_ref[...]
    ms = jnp.mean(x * x, axis=-1, keepdims=True)
    h = (x * lax.rsqrt(ms + RMS_EPS) * nw_ref[...]).astype(BF16)
    if len(rest) == 1:
        rest[0][...] = _dot_nt(h, w_ref[...]).astype(rest[0].dtype)
    else:
        ws_ref, o_ref, os_ref = rest
        o_ref[...] = _dot_nt(h, w_ref[...]).astype(o_ref.dtype)
        os_ref[0] = _dot_nt(h, ws_ref[...])


def norm_matmul(x, nw, w, out_dtype, tm, tn, w_small=None):
    n, d = x.shape
    width = w.shape[0]
    in_specs = [pl.BlockSpec((tm, d), lambda j, i: (i, 0)),
                pl.BlockSpec((1, d), lambda j, i: (0, 0)),
                pl.BlockSpec((tn, d), lambda j, i: (j, 0))]
    out_specs = pl.BlockSpec((tm, tn), lambda j, i: (i, j))
    out_shape = jax.ShapeDtypeStruct((n, width), out_dtype)
    operands = (x, nw.reshape(1, d), w)
    if w_small is not None:
        ws = w_small.shape[0]
        in_specs.append(pl.BlockSpec((ws, d), lambda j, i: (0, 0)))
        out_specs = [out_specs, pl.BlockSpec((1, tm, ws), lambda j, i: (j, i, 0))]
        out_shape = [out_shape, jax.ShapeDtypeStruct((width // tn, n, ws), F32)]
        operands = operands + (w_small,)
    outs = pl.pallas_call(
        _norm_matmul_body,
        grid=(width // tn, n // tm),
        in_specs=in_specs,
        out_specs=out_specs,
        out_shape=out_shape,
        compiler_params=_cparams(("arbitrary", "arbitrary")),
        name="norm_in_proj",
    )(*operands)
    if w_small is None:
        return outs
    return outs[0], outs[1][0]


def _gdn_body(alog_ref, dtb_ref, q_ref, k_ref, v_ref, z_ref, ba_ref, cwq_ref, cwk_ref, cwv_ref,
              onw_ref, o_ref, *scratch):
    tt = GDN_TILE
    c = GDN_CHUNK
    t = pl.program_id(2)

    @pl.when(t == 0)
    def _():
        for hh in range(GDN_HEADS_PER_STEP):
            s_ref, prev_ref, vn_ref, _ = scratch[4 * hh:4 * hh + 4]
            s_ref[...] = jnp.zeros_like(s_ref)
            prev_ref[...] = jnp.zeros_like(prev_ref)
            vn_ref[...] = jnp.zeros_like(vn_ref)

    row_8 = lax.broadcasted_iota(jnp.int32, (8, LANES), 0)
    lane_t = lax.broadcasted_iota(jnp.int32, (tt, LANES), 1)
    row = lax.broadcasted_iota(jnp.int32, (tt, tt), 0)
    col = lax.broadcasted_iota(jnp.int32, (tt, tt), 1)
    same = (row // c) == (col // c)
    causal = jnp.logical_and(same, col <= row)
    strict = jnp.logical_and(same, col < row)
    causal16 = jnp.where(causal, 1.0, 0.0).astype(BF16)
    same16 = jnp.where(same, 1.0, 0.0).astype(BF16)
    eye = jnp.where(row == col, 1.0, 0.0)
    colk = lax.broadcasted_iota(jnp.int32, (HEAD_DIM, tt), 1)
    ba = ba_ref[...]

    heads = [_gdn_one_head(hh, pl.program_id(1) * GDN_HEADS_PER_STEP + hh, alog_ref, dtb_ref, q_ref, k_ref,
                           v_ref, z_ref, ba, cwq_ref, cwk_ref, cwv_ref, onw_ref, o_ref,
                           *scratch[4 * hh:4 * hh + 4],
                           row_8, lane_t, causal, strict, causal16, same16, eye, colk)
             for hh in range(GDN_HEADS_PER_STEP)]
    for _ in itertools.zip_longest(*heads):
        pass


def _gdn_one_head(hh, h, alog_ref, dtb_ref, q_ref, k_ref, v_ref, z_ref, ba, cwq_ref, cwk_ref, cwv_ref,
                  onw_ref, o_ref, s_ref, prev_ref, vn_ref, oacc_ref,
                  row_8, lane_t, causal, strict, causal16, same16, eye, colk):
    tt = GDN_TILE
    c = GDN_CHUNK
    lanes = slice(hh * HEAD_DIM, (hh + 1) * HEAD_DIM)

    def conv_silu(x_ref, cw_ref, idx):
        x = x_ref[:, lanes].astype(F32)
        p = prev_ref[idx]
        w = cw_ref[:, lanes]
        acc = x * w[CONV_WIDTH - 1:CONV_WIDTH, :]
        for s in range(1, CONV_WIDTH):
            xs = pltpu.roll(x, s, axis=0)
            head = jnp.where(row_8 < s, pltpu.roll(p, s, axis=0), xs[:8])
            xs = jnp.concatenate([head, xs[8:]], axis=0)
            acc = acc + xs * w[CONV_WIDTH - 1 - s:CONV_WIDTH - s, :]
        prev_ref[idx] = x[tt - 8:tt, :]
        return acc * _sigmoid(acc)

    q = conv_silu(q_ref, cwq_ref, 0)
    k = conv_silu(k_ref, cwk_ref, 1)
    v = conv_silu(v_ref, cwv_ref, 2)
    q = q * lax.rsqrt(jnp.sum(q * q, axis=-1, keepdims=True) + 1e-6) * (HEAD_DIM ** -0.5)
    k = k * lax.rsqrt(jnp.sum(k * k, axis=-1, keepdims=True) + 1e-6)
    yield

    b_col = jnp.sum(jnp.where(lane_t == h, ba, 0.0), axis=-1, keepdims=True)
    a_col = jnp.sum(jnp.where(lane_t == h + GDN_HEADS, ba, 0.0), axis=-1, keepdims=True)
    beta = _sigmoid(b_col)
    xa = a_col + dtb_ref[h]
    softplus = jnp.maximum(xa, 0.0) + jnp.log(1.0 + jnp.exp(-jnp.abs(xa)))
    g = -jnp.exp(jnp.full((1, 1), alog_ref[h], F32)) * softplus

    g_hi = g.astype(BF16).astype(F32)
    g_mid = (g - g_hi).astype(BF16).astype(F32)
    g_lo = g - g_hi - g_mid
    g3 = jnp.where(lane_t == 0, g_hi, jnp.where(lane_t == 1, g_mid, jnp.where(lane_t == 2, g_lo, 0.0)))
    g3 = g3.astype(BF16)

    def three(r):
        return r[:, 0:1] + r[:, 1:2] + r[:, 2:3]

    gc_col = three(_dot(causal16, g3))
    gc_b = jnp.broadcast_to(gc_col, (tt, LANES))
    glast_b = jnp.broadcast_to(three(_dot(same16, g3)), (tt, LANES))
    gc_row = jnp.transpose(gc_b)[0:1, :]
    decay = jnp.exp(jnp.where(causal, gc_col - gc_row, NEG_INF))
    yield

    kb = k * beta
    k16 = k.astype(BF16)
    kk = _dot_nt(kb.astype(BF16), k16)
    lmat = jnp.where(strict, kk * decay, 0.0)
    yield
    attn = _dot_nt(q.astype(BF16), k16) * decay

    tinv = eye - lmat
    m16 = lmat.astype(BF16)
    for _ in range(5):
        m16 = _dot(m16, m16).astype(BF16)
        yield
        tinv = tinv + _dot(tinv.astype(BF16), m16)
        yield

    egc = jnp.exp(gc_b)
    rhs = jnp.concatenate([v * beta, kb * egc], axis=-1)
    sol = _dot(tinv.astype(BF16), rhs.astype(BF16))
    yield
    u = sol[:, :HEAD_DIM]
    w16 = sol[:, HEAD_DIM:].astype(BF16)
    qd16 = (q * egc).astype(BF16)
    kd = k * jnp.exp(glast_b - gc_b)
    kdt = jnp.transpose(kd)
    gtot_b = jnp.exp(glast_b)
    attn16 = attn.astype(BF16)

    s = s_ref[...]
    for n in range(tt // c):
        sl = slice(n * c, (n + 1) * c)
        s16 = s.astype(BF16)
        v_new = u[sl] - _dot(w16[sl], s16)
        vn_ref[sl, :] = v_new
        yield
        vn16 = vn_ref[...].astype(BF16)
        oacc_ref[sl, :] = _dot(qd16[sl], s16) + _dot(attn16[sl], vn16)
        kdt_n = jnp.where((colk // c) == n, kdt, 0.0).astype(BF16)
        s = s * gtot_b[n * c:n * c + 1, :] + _dot(kdt_n, vn16)
        yield
    s_ref[...] = s

    o = oacc_ref[...]
    o = o * lax.rsqrt(jnp.mean(o * o, axis=-1, keepdims=True) + RMS_EPS) * onw_ref[...]
    z = z_ref[:, lanes].astype(F32)
    o_ref[:, lanes] = (o * (z * _sigmoid(z))).astype(o_ref.dtype)


def gdn_heads(proj, ba, conv_w, a_log, dt_bias, out_norm_w, batch, seq):
    n = batch * seq
    tt = GDN_TILE
    nt = seq // tt
    hps = GDN_HEADS_PER_STEP
    ng = GDN_HEADS // hps
    wide = hps * HEAD_DIM

    def col_spec(section):
        return pl.BlockSpec((tt, wide), lambda b, h, t, *_: (b * nt + t, section * ng + h))

    def cw_spec(section):
        return pl.BlockSpec((CONV_WIDTH, wide), lambda b, h, t, *_: (0, section * ng + h))

    grid_spec = pltpu.PrefetchScalarGridSpec(
        num_scalar_prefetch=2,
        grid=(batch, ng, nt),
        in_specs=[col_spec(0), col_spec(1), col_spec(2), col_spec(3),
                  pl.BlockSpec((tt, LANES), lambda b, h, t, *_: (b * nt + t, 0)),
                  cw_spec(0), cw_spec(1), cw_spec(2),
                  pl.BlockSpec((1, HEAD_DIM), lambda b, h, t, *_: (0, 0))],
        out_specs=pl.BlockSpec((tt, wide), lambda b, h, t, *_: (b * nt + t, h)),
        scratch_shapes=[pltpu.VMEM((HEAD_DIM, HEAD_DIM), F32),
                        pltpu.VMEM((3, 8, HEAD_DIM), F32),
                        pltpu.VMEM((tt, HEAD_DIM), F32),
                        pltpu.VMEM((tt, HEAD_DIM), F32)] * hps)
    return pl.pallas_call(
        _gdn_body,
        grid_spec=grid_spec,
        out_shape=jax.ShapeDtypeStruct((n, GDN_WIDTH), BF16),
        compiler_params=_cparams(("arbitrary", "arbitrary", "arbitrary")),
        name="gdn_heads",
    )(a_log.astype(F32), dt_bias.astype(F32), proj, proj, proj, proj, ba,
      conv_w, conv_w, conv_w, out_norm_w.reshape(1, HEAD_DIM))


def _moba_body(slope_ref, q_ref, k_ref, v_ref, onw_ref, o_ref, *scratch, n_blocks):
    blk = MOBA_BLOCK
    tq = 2 * blk
    nb = n_blocks
    nbp = -(-nb // 8) * 8
    hps = MOBA_HEADS_PER_STEP
    cq = pl.program_id(2)
    c0 = 2 * cq
    lane = lax.broadcasted_iota(I32, (blk, LANES), 1)
    row = lax.broadcasted_iota(I32, (blk, blk), 0)
    col = lax.broadcasted_iota(I32, (blk, blk), 1)
    rid = lax.broadcasted_iota(I32, (LANES, tq), 0)
    bid = lax.broadcasted_iota(I32, (nbp, tq), 0)
    c_row = c0 + (lax.broadcasted_iota(I32, (nbp, tq), 1) >= blk).astype(I32)
    ones_ext_t = jnp.where(jnp.logical_and(rid >= nb, rid < nb + 3), 1.0, 0.0).astype(BF16)

    def head_lanes(hh):
        return slice(hh * HEAD_DIM, (hh + 1) * HEAD_DIM)

    def keys(hh, j, nblk=1):
        return scratch[3 * hh + 1][pl.ds(pl.multiple_of(j * blk, blk), nblk * blk), :]

    def values_t(hh, j, nblk=1):
        return scratch[3 * hh + 2][:, pl.ds(pl.multiple_of(j * blk, blk), nblk * blk)]

    @pl.when(cq == 0)
    def _():
        t_in = lax.broadcasted_iota(I32, (blk, LANES), 0)
        for hh in range(hps):
            kmean_ref, kx_ref, vt_ref = scratch[3 * hh:3 * hh + 3]
            slope2 = slope_ref[pl.program_id(1) * hps + hh] * LOG2E
            kmean_ref[...] = jnp.zeros_like(kmean_ref)
            for j in range(nb):
                kj = k_ref[j * blk:(j + 1) * blk, head_lanes(hh)]
                kmean_ref[j:j + 1, :] = jnp.mean(kj.astype(F32), axis=0, keepdims=True)
                bias = slope2 * (t_in + j * blk).astype(F32)
                b_hi = bias.astype(BF16)
                r1 = bias - b_hi.astype(F32)
                b_mid = r1.astype(BF16)
                b_lo = (r1 - b_mid.astype(F32)).astype(BF16)
                ext = jnp.where(lane == j, 1.0, 0.0).astype(BF16)
                ext = jnp.where(lane == nb, b_hi, ext)
                ext = jnp.where(lane == nb + 1, b_mid, ext)
                ext = jnp.where(lane == nb + 2, b_lo, ext)
                kx_ref[j * blk:(j + 1) * blk, :] = jnp.concatenate([kj, ext], axis=1)
                vj = v_ref[j * blk:(j + 1) * blk, head_lanes(hh)].astype(F32)
                vt_ref[:, j * blk:(j + 1) * blk] = jnp.transpose(vj).astype(BF16)

    def start(s, vals_t):
        m = jnp.max(s, axis=0, keepdims=True)
        p = jnp.exp2(s - m)
        return m, jnp.sum(p, axis=0, keepdims=True), _dot(vals_t, p.astype(BF16))

    def update(carry, s, vals_t):
        m_i, l_i, acc = carry
        m_new = jnp.maximum(m_i, jnp.max(s, axis=0, keepdims=True))
        alpha = jnp.exp2(m_i - m_new)
        p = jnp.exp2(s - m_new)
        return (m_new, l_i * alpha + jnp.sum(p, axis=0, keepdims=True),
                acc * alpha + _dot(vals_t, p.astype(BF16)))

    qx = [None] * hps
    carry0 = [None] * hps

    def prologue(hh):
        q_t = jnp.transpose(q_ref[:, head_lanes(hh)].astype(F32))
        q_t16 = q_t.astype(BF16)
        yield
        qx0_t = jnp.concatenate([q_t16, ones_ext_t], axis=0)
        state = []
        for half in range(2):
            s_own = _dot(keys(hh, c0 + half), qx0_t[:, half * blk:(half + 1) * blk])
            s_own = jnp.where(row <= col, s_own, MOBA_MASK)
            yield
            state.append(start(s_own, values_t(hh, c0 + half)))
            yield
        gate_t = jnp.dot(scratch[3 * hh][...], q_t, preferred_element_type=F32, precision=HI)[:nbp]
        gm = jnp.where(bid < c_row, gate_t, NEG_INF)
        sel = jnp.zeros((nbp, tq), F32)
        for s in range(MOBA_TOPK):
            mx = jnp.max(gm, axis=0, keepdims=True)
            idx = jnp.min(jnp.where(gm == mx, bid, LANES), axis=0, keepdims=True)
            pick = bid == idx
            sel = jnp.where(jnp.logical_and(pick, s < c_row), 1.0, sel)
            gm = jnp.where(pick, -3e38, gm)
        mask_t = jnp.where(sel > 0.5, 0.0, MOBA_MASK)
        ext_t = jnp.concatenate([mask_t, jnp.zeros((LANES - nbp, tq), F32)], axis=0)
        ext_t = jnp.where(rid < nb, ext_t, jnp.where(rid < nb + 3, 1.0, 0.0))
        qx[hh] = jnp.concatenate([q_t16, ext_t.astype(BF16)], axis=0)
        yield
        s_c0 = _dot(keys(hh, c0), qx[hh][:, blk:])
        yield
        second = update(state[1], s_c0, values_t(hh, c0))
        carry0[hh] = tuple(jnp.concatenate([a, b], axis=1) for a, b in zip(state[0], second))

    for _ in itertools.zip_longest(*[prologue(hh) for hh in range(hps)]):
        pass

    def body(p, carries):
        scores = [_dot(keys(hh, 2 * p, 2), qx[hh]) for hh in range(hps)]
        return tuple(update(carries[hh], scores[hh], values_t(hh, 2 * p, 2)) for hh in range(hps))

    final = lax.fori_loop(0, cq, body, tuple(carry0))
    for hh in range(hps):
        _, l_f, acc_f = final[hh]
        o_t = acc_f / l_f
        o_t = o_t * lax.rsqrt(jnp.mean(o_t * o_t, axis=0, keepdims=True) + RMS_EPS)
        o_ref[:, head_lanes(hh)] = (jnp.transpose(o_t) * onw_ref[...]).astype(o_ref.dtype)


def moba_heads(proj, out_norm_w, batch, seq, col_off):
    n = batch * seq
    blk = MOBA_BLOCK
    nb = seq // blk
    nq = nb // 2
    hps = MOBA_HEADS_PER_STEP
    ng = MOBA_HEADS // hps
    wide = hps * HEAD_DIM
    sec0 = col_off // hps
    slopes = jnp.exp2(-8.0 * jnp.arange(1, MOBA_HEADS + 1, dtype=F32) / MOBA_HEADS)
    grid_spec = pltpu.PrefetchScalarGridSpec(
        num_scalar_prefetch=1,
        grid=(batch, ng, nq),
        in_specs=[pl.BlockSpec((2 * blk, wide), lambda b, h, c, *_: (b * nq + c, sec0 + h)),
                  pl.BlockSpec((seq, wide), lambda b, h, c, *_: (b, sec0 + ng + h)),
                  pl.BlockSpec((seq, wide), lambda b, h, c, *_: (b, sec0 + 2 * ng + h)),
                  pl.BlockSpec((1, HEAD_DIM), lambda b, h, c, *_: (0, 0))],
        out_specs=pl.BlockSpec((2 * blk, wide), lambda b, h, c, *_: (b * nq + c, h)),
        scratch_shapes=[pltpu.VMEM((LANES, HEAD_DIM), F32),
                        pltpu.VMEM((nb * blk, HEAD_DIM + LANES), BF16),
                        pltpu.VMEM((HEAD_DIM, nb * blk), BF16)] * hps)
    assert nb % 2 == 0 and nb + 3 <= LANES and col_off % hps == 0
    return pl.pallas_call(
        functools.partial(_moba_body, n_blocks=nb),
        grid_spec=grid_spec,
        out_shape=jax.ShapeDtypeStruct((n, MOBA_WIDTH), BF16),
        compiler_params=_cparams(("arbitrary", "arbitrary", "arbitrary")),
        name="moba_heads",
    )(slopes, proj, proj, proj, out_norm_w.reshape(1, HEAD_DIM))


def _out_proj_body(x_ref, og_ref, om_ref, wg_ref, wm_ref, nw_ref, wrh_ref, wrl_ref, rb_ref,
                   x2_ref, hp_ref, gt_ref, id_ref):
    tm = x_ref.shape[0]
    x2 = x_ref[...] + _dot(og_ref[...], wg_ref[...]) + _dot(om_ref[...], wm_ref[...])
    x2_ref[...] = x2
    ms = jnp.mean(x2 * x2, axis=-1, keepdims=True)
    h2 = x2 * lax.rsqrt(ms + RMS_EPS) * nw_ref[...]
    h_hi = h2.astype(BF16)
    hp_ref[...] = h2
    h_lo = (h2 - h_hi.astype(F32)).astype(BF16)
    lg = _dot(h_hi, wrh_ref[...]) + _dot(h_hi, wrl_ref[...]) + _dot(h_lo, wrh_ref[...]) + rb_ref[...]

    lane = lax.broadcasted_iota(I32, (tm, LANES), 1)
    is_g = lane < N_GROUPS
    mg = jnp.max(jnp.where(is_g, lg, NEG_INF), axis=-1, keepdims=True)
    g_idx = jnp.min(jnp.where(jnp.logical_and(is_g, lg == mg), lane, LANES), axis=-1, keepdims=True)
    sum_g = jnp.sum(jnp.where(is_g, jnp.exp(lg - mg), 0.0), axis=-1, keepdims=True)
    p_top_g = 1.0 / sum_g
    lo = N_GROUPS + g_idx * EXPERTS_PER_GROUP
    in_grp = jnp.logical_and(lane >= lo, lane < lo + EXPERTS_PER_GROUP)
    m1 = jnp.max(jnp.where(in_grp, lg, NEG_INF), axis=-1, keepdims=True)
    i1 = jnp.min(jnp.where(jnp.logical_and(in_grp, lg == m1), lane, LANES), axis=-1, keepdims=True)
    rest = jnp.logical_and(in_grp, lane != i1)
    m2 = jnp.max(jnp.where(rest, lg, NEG_INF), axis=-1, keepdims=True)
    i2 = jnp.min(jnp.where(jnp.logical_and(rest, lg == m2), lane, LANES), axis=-1, keepdims=True)
    e2 = jnp.exp(m2 - m1)
    gate1 = p_top_g / (1.0 + e2)
    gate2 = p_top_g * e2 / (1.0 + e2)
    gt_ref[...] = jnp.where(lane == 0, gate1, jnp.where(lane == 1, gate2, 0.0))
    id_ref[...] = jnp.where(lane == 0, i1 - N_GROUPS, jnp.where(lane == 1, i2 - N_GROUPS, 0))


def out_proj(x, og, om, w_g, w_m, nw, wr_hi, wr_lo, r_bias, tm):
    n, d = x.shape
    const = lambda i: (0, 0)
    rows = lambda i: (i, 0)
    return pl.pallas_call(
        _out_proj_body,
        grid=(n // tm,),
        in_specs=[pl.BlockSpec((tm, d), rows),
                  pl.BlockSpec((tm, GDN_WIDTH), rows),
                  pl.BlockSpec((tm, MOBA_WIDTH), rows),
                  pl.BlockSpec((GDN_WIDTH, d), const),
                  pl.BlockSpec((MOBA_WIDTH, d), const),
                  pl.BlockSpec((1, d), const),
                  pl.BlockSpec((d, LANES), const),
                  pl.BlockSpec((d, LANES), const),
                  pl.BlockSpec((1, LANES), const)],
        out_specs=[pl.BlockSpec((tm, d), rows),
                   pl.BlockSpec((tm, d), rows),
                   pl.BlockSpec((tm, LANES), rows),
                   pl.BlockSpec((tm, LANES), rows)],
        out_shape=[jax.ShapeDtypeStruct((n, d), F32),
                   jax.ShapeDtypeStruct((n, d), F32),
                   jax.ShapeDtypeStruct((n, LANES), F32),
                   jax.ShapeDtypeStruct((n, LANES), I32)],
        compiler_params=_cparams(("arbitrary",)),
        name="out_proj_router",
    )(x, og, om, w_g, w_m, nw.reshape(1, d), wr_hi, wr_lo, r_bias)


def _invert_body(pos0_ref, pos1_ref, src_ref, zeros_vmem, sem):
    n_tok = pos0_ref.shape[0]
    zeros_vmem[...] = jnp.zeros_like(zeros_vmem)
    clear = pltpu.make_async_copy(zeros_vmem, src_ref, sem)
    clear.start()
    clear.wait()

    def fill(t, carry):
        src_ref[pos0_ref[t]] = t
        src_ref[pos1_ref[t]] = t
        return carry

    lax.fori_loop(0, n_tok, fill, 0, unroll=8)


def moe_source_rows(pos0, pos1, n_rows):
    grid_spec = pltpu.PrefetchScalarGridSpec(
        num_scalar_prefetch=2,
        grid=(1,),
        in_specs=[],
        out_specs=pl.BlockSpec(memory_space=pltpu.SMEM),
        scratch_shapes=[pltpu.VMEM((n_rows,), I32), pltpu.SemaphoreType.DMA(())])
    return pl.pallas_call(
        _invert_body,
        grid_spec=grid_spec,
        out_shape=jax.ShapeDtypeStruct((n_rows,), I32),
        compiler_params=_cparams(("arbitrary",)),
        name="moe_source_rows",
    )(pos0, pos1)


def _moe_body(blk_e_ref, nblk_ref, first_ref, slot_ref, next_e_ref, src_ref, h_hbm, wg_hbm, wu_hbm, wd_hbm,
              y_ref, xbuf, wg_buf, wu_buf, wd_buf, wg16, wu16, wd16, sem, gsem):
    rb = MOE_ROWS
    i = pl.program_id(0)
    e = blk_e_ref[i]
    slot = slot_ref[i]
    n_used = nblk_ref[0]
    nbuf = MOE_ROW_BUFFERS
    ahead = nbuf - 1
    xslot = lax.rem(i, nbuf)

    def row_copy(block, r):
        s = lax.rem(block, nbuf)
        tok = src_ref[block * rb + r]
        return pltpu.make_async_copy(h_hbm.at[pl.ds(tok, 1)], xbuf.at[s, pl.ds(r, 1)], gsem.at[s])

    def wait_rows(s):
        pltpu.make_async_copy(h_hbm.at[pl.ds(0, rb)], xbuf.at[s], gsem.at[s]).wait()

    @pl.when(i == 0)
    def _():
        for b in range(ahead):
            @pl.when(b < n_used)
            def _(b=b):
                def issue(r, carry):
                    row_copy(b, r).start()
                    return carry
                lax.fori_loop(0, rb, issue, 0, unroll=8)

    def weight_copies(expert, s):
        return (pltpu.make_async_copy(wg_hbm.at[expert], wg_buf.at[s], sem.at[s]),
                pltpu.make_async_copy(wu_hbm.at[expert], wu_buf.at[s], sem.at[s]),
                pltpu.make_async_copy(wd_hbm.at[expert], wd_buf.at[s], sem.at[s]))

    @pl.when(i == 0)
    def _():
        for cp in weight_copies(e, slot):
            cp.start(priority=1)

    @pl.when(first_ref[i] == 1)
    def _():
        for cp in weight_copies(e, slot):
            cp.wait()

        @pl.when(next_e_ref[i] >= 0)
        def _():
            for cp in weight_copies(next_e_ref[i], 1 - slot):
                cp.start(priority=1)

        wg16[...] = wg_buf[slot].astype(BF16)
        wu16[...] = wu_buf[slot].astype(BF16)
        wd16[...] = wd_buf[slot].astype(BF16)

    def compute(xb):
        g = _dot(xb, wg16[...])
        u = _dot(xb, wu16[...])
        hm = (g * _sigmoid(g) * u).astype(BF16)
        y_ref[...] = _dot(hm, wd16[...])

    @pl.when(i < n_used - ahead)
    def _():
        wait_rows(xslot)
        xb = xbuf[xslot].astype(BF16)
        for r in range(rb):
            row_copy(i + ahead, r).start()
        compute(xb)

    @pl.when(jnp.logical_and(i >= n_used - ahead, i < n_used))
    def _():
        wait_rows(xslot)
        compute(xbuf[xslot].astype(BF16))

    @pl.when(i >= n_used)
    def _():
        y_ref[...] = jnp.zeros_like(y_ref)


def moe_experts(h2, src_rows, blk_expert, n_used, first, slot, next_e, w_gate, w_up, w_down):
    rb = MOE_ROWS
    n_rows = src_rows.shape[0]
    d = h2.shape[1]
    de = w_gate.shape[-1]
    grid_spec = pltpu.PrefetchScalarGridSpec(
        num_scalar_prefetch=6,
        grid=(n_rows // rb,),
        in_specs=[pl.BlockSpec(memory_space=pl.ANY),
                  pl.BlockSpec(memory_space=pl.ANY),
                  pl.BlockSpec(memory_space=pl.ANY),
                  pl.BlockSpec(memory_space=pl.ANY)],
        out_specs=pl.BlockSpec((rb, d), lambda i, *_: (i, 0)),
        scratch_shapes=[pltpu.VMEM((MOE_ROW_BUFFERS, rb, d), F32),
                        pltpu.VMEM((2, d, de), F32), pltpu.VMEM((2, d, de), F32), pltpu.VMEM((2, de, d), F32),
                        pltpu.VMEM((d, de), BF16), pltpu.VMEM((d, de), BF16), pltpu.VMEM((de, d), BF16),
                        pltpu.SemaphoreType.DMA((2,)), pltpu.SemaphoreType.DMA((MOE_ROW_BUFFERS,))])
    return pl.pallas_call(
        _moe_body,
        grid_spec=grid_spec,
        out_shape=jax.ShapeDtypeStruct((n_rows, d), F32),
        compiler_params=_cparams(("arbitrary",)),
        name="moe_experts",
    )(blk_expert, n_used, first, slot, next_e, src_rows, h2, w_gate, w_up, w_down)


def _combine_body(pos0_ref, pos1_ref, y_hbm, x2_ref, gt_ref, nw_ref, o_ref, buf, sem):
    tf = COMBINE_ROWS
    nbuf = COMBINE_BUFFERS
    ahead = nbuf - 1
    i = pl.program_id(0)
    n_steps = pl.num_programs(0)
    slot = lax.rem(i, nbuf)

    def start_row(step, r):
        sl = lax.rem(step, nbuf)
        t = step * tf + r
        pltpu.make_async_copy(y_hbm.at[pl.ds(pos0_ref[t], 1)], buf.at[sl, 0, pl.ds(r, 1)],
                              sem.at[sl]).start()
        pltpu.make_async_copy(y_hbm.at[pl.ds(pos1_ref[t], 1)], buf.at[sl, 1, pl.ds(r, 1)],
                              sem.at[sl]).start(priority=1)

    @pl.when(i == 0)
    def _():
        for b in range(ahead):
            def body(r, carry, b=b):
                start_row(b, r)
                return carry
            lax.fori_loop(0, tf, body, 0, unroll=8)

    def wait_rows():
        for kk in range(TOPK_IN_GROUP):
            pltpu.make_async_copy(y_hbm.at[pl.ds(0, tf)], buf.at[slot, kk], sem.at[slot]).wait()

    def finish(y0, y1):
        gt = gt_ref[...]
        xo = x2_ref[...] + gt[:, 0:1] * y0 + gt[:, 1:2] * y1
        ms = jnp.mean(xo * xo, axis=-1, keepdims=True)
        o_ref[...] = xo * lax.rsqrt(ms + RMS_EPS) * nw_ref[...]

    @pl.when(i + ahead < n_steps)
    def _():
        wait_rows()
        y0 = buf[slot, 0]
        y1 = buf[slot, 1]
        for r in range(tf):
            start_row(i + ahead, r)
        finish(y0, y1)

    @pl.when(i + ahead >= n_steps)
    def _():
        wait_rows()
        finish(buf[slot, 0], buf[slot, 1])


def moe_combine(yb, pos0, pos1, x2, gates, nw):
    n, d = x2.shape
    tf = COMBINE_ROWS
    grid_spec = pltpu.PrefetchScalarGridSpec(
        num_scalar_prefetch=2,
        grid=(n // tf,),
        in_specs=[pl.BlockSpec(memory_space=pl.ANY),
                  pl.BlockSpec((tf, d), lambda i, *_: (i, 0)),
                  pl.BlockSpec((tf, LANES), lambda i, *_: (i, 0)),
                  pl.BlockSpec((1, d), lambda i, *_: (0, 0))],
        out_specs=pl.BlockSpec((tf, d), lambda i, *_: (i, 0)),
        scratch_shapes=[pltpu.VMEM((COMBINE_BUFFERS, 2, tf, d), F32), pltpu.SemaphoreType.DMA((COMBINE_BUFFERS,))])
    return pl.pallas_call(
        _combine_body,
        grid_spec=grid_spec,
        out_shape=jax.ShapeDtypeStruct((n, d), F32),
        compiler_params=_cparams(("arbitrary",)),
        name="moe_combine",
    )(pos0, pos1, yb, x2, gates, nw.reshape(1, d))


def _dispatch_plan(expert_id):
    n_tok, k = expert_id.shape
    rb = MOE_ROWS
    n_assign = n_tok * k
    e_flat = expert_id.reshape(-1)
    onehot = (e_flat[:, None] == jnp.arange(N_EXPERTS, dtype=I32)[None, :]).astype(I32)
    csum = jnp.cumsum(onehot, axis=0)
    counts = csum[-1]
    padded = (counts + rb - 1) // rb * rb
    pad_end = jnp.cumsum(padded)
    pad_start = pad_end - padded
    pos = jnp.sum(onehot * (pad_start[None, :] + csum - 1), axis=1).astype(I32).reshape(n_tok, k)
    n_rb = -(-n_assign // rb) + N_EXPERTS
    blk_start = jnp.arange(n_rb, dtype=I32) * rb
    blk_expert = jnp.minimum(jnp.sum((pad_end[None, :] <= blk_start[:, None]).astype(I32), axis=1),
                             N_EXPERTS - 1).astype(I32)
    n_used = (pad_end[-1] // rb).astype(I32).reshape(1)
    blk = jnp.arange(n_rb, dtype=I32)
    prev_expert = jnp.concatenate([jnp.full((1,), -1, I32), blk_expert[:-1]])
    first = jnp.logical_and(blk < n_used[0], blk_expert != prev_expert).astype(I32)
    slot = jnp.bitwise_and(jnp.cumsum(first) - 1, 1).astype(I32)
    ids = jnp.arange(N_EXPERTS, dtype=I32)
    later = jnp.logical_and(ids[None, :] > ids[:, None], counts[None, :] > 0)
    next_nonempty = jnp.min(jnp.where(later, ids[None, :], N_EXPERTS), axis=1)
    next_nonempty = jnp.where(next_nonempty < N_EXPERTS, next_nonempty, -1).astype(I32)
    next_e = next_nonempty[blk_expert]
    return pos, blk_expert, n_used, first, slot, next_e, n_rb * rb


def kernel(x, norm_mix_w, w_in, gdn_conv_w, gdn_A_log, gdn_dt_bias, gdn_out_norm_w, moba_out_norm_w, w_out, norm_ffn_w, w_router_group, b_router_group, w_router_expert, b_router_expert, w_expert_gate, w_expert_up, w_expert_down, norm_final_w):
    batch, seq, d = x.shape
    n = batch * seq
    assert w_in.shape[0] == 1, "the final norm is fused into the last layer's combine; one layer supported"
    l = 0
    xf = x.reshape(n, d).astype(F32)
    gw = 4 * GDN_WIDTH
    w_l = w_in[l]
    mq0 = gw + 2 * GDN_HEADS
    col_scale = jnp.where(jnp.arange(3 * MOBA_WIDTH) < MOBA_WIDTH, HEAD_DIM ** -0.5 * LOG2E, 1.0).astype(F32)
    w_t = jnp.swapaxes(w_l, 0, 1)
    w_gdn = w_t[:gw].astype(BF16)
    w_moba = (w_t[mq0:] * col_scale[:, None]).astype(BF16)
    w_ba = jnp.pad(w_t[gw:gw + 2 * GDN_HEADS], ((0, LANES - 2 * GDN_HEADS), (0, 0))).astype(BF16)
    proj_g, ba = norm_matmul(xf, norm_mix_w[l], w_gdn, BF16, IN_PROJ_ROWS, gw // 2, w_small=w_ba)
    proj_m = norm_matmul(xf, norm_mix_w[l], w_moba, BF16, IN_PROJ_ROWS, 3 * MOBA_WIDTH // 2)
    og = gdn_heads(proj_g, ba, gdn_conv_w[l].astype(F32), gdn_A_log[l], gdn_dt_bias[l],
                   gdn_out_norm_w[l].astype(F32), batch, seq)
    om = moba_heads(proj_m, moba_out_norm_w[l].astype(F32), batch, seq, 0)

    w_o = w_out[l].astype(BF16)
    n_r = N_GROUPS + N_EXPERTS
    w_router = jnp.pad(jnp.concatenate([w_router_group[l], w_router_expert[l]], axis=1).astype(F32),
                       ((0, 0), (0, LANES - n_r)))
    wr_hi = w_router.astype(BF16)
    wr_lo = (w_router - wr_hi.astype(F32)).astype(BF16)
    r_bias = jnp.pad(jnp.concatenate([b_router_group[l], b_router_expert[l]]).astype(F32),
                     (0, LANES - n_r)).reshape(1, LANES)
    x2, hp, gates, ids = out_proj(xf, og, om, w_o[:GDN_WIDTH], w_o[GDN_WIDTH:], norm_ffn_w[l].astype(F32),
                                  wr_hi, wr_lo, r_bias, 512)

    pos, blk_expert, n_used, first, slot, next_e, n_rows = _dispatch_plan(ids[:, :TOPK_IN_GROUP])
    pos0, pos1 = pos[:, 0], pos[:, 1]
    src_rows = moe_source_rows(pos0, pos1, n_rows)
    yb = moe_experts(hp, src_rows, blk_expert, n_used, first, slot, next_e, w_expert_gate[l].astype(F32),
                     w_expert_up[l].astype(F32), w_expert_down[l].astype(F32))
    out = moe_combine(yb, pos0, pos1, x2, gates, norm_final_w.astype(F32))
    return out.reshape(batch, seq, d).astype(x.dtype)
```
